```python
import math
import jax, jax.numpy as jnp
from jax import lax
import numpy as np

D_MODEL = 1024
BATCH = 8
SEQ = 4096
DEPTH = 1

CHUNK = 64
EPS = 1e-6

SSD_WIDTH = D_MODEL // 2
SSD_HEAD_DIM = 64
SSD_HEADS = SSD_WIDTH // SSD_HEAD_DIM
SSD_GROUPS = 2
SSD_HEADS_PER_GROUP = SSD_HEADS // SSD_GROUPS
SSD_STATE = 128
CONV_WIDTH = 4
SSD_CONV_CH = SSD_WIDTH + 2 * SSD_GROUPS * SSD_STATE

DIFF_WIDTH = D_MODEL - SSD_WIDTH
DIFF_HEADS = 4
DIFF_VDIM = DIFF_WIDTH // DIFF_HEADS
DIFF_QKDIM = DIFF_VDIM // 2
ROPE_DIM = DIFF_QKDIM // 4
ROPE_THETA = 500000.0
Q_BLOCK = 128

QK_COLS = DIFF_HEADS * 2 * DIFF_QKDIM
SPLITS = (SSD_WIDTH,
          SSD_WIDTH + SSD_CONV_CH,
          SSD_WIDTH + SSD_CONV_CH + SSD_HEADS,
          SSD_WIDTH + SSD_CONV_CH + SSD_HEADS + QK_COLS,
          SSD_WIDTH + SSD_CONV_CH + SSD_HEADS + 2 * QK_COLS)
IN_COLS = SPLITS[-1] + DIFF_WIDTH
MIX_WIDTH = SSD_WIDTH + DIFF_WIDTH

PEER_HEADS = 8
N_KEYS = 128
N_EXPERTS = N_KEYS * N_KEYS
PEER_KEY_DIM = 256
PEER_HALF = PEER_KEY_DIM // 2
PEER_TOPK = 16
PEER_TOKEN_BLOCK = 128

kernel_name = "hybrid_ssd_diffattn_peer_block"


def rmsnorm(x, w):
    xf = x.astype(jnp.float32)
    y = xf * lax.rsqrt(jnp.mean(xf * xf, axis=-1, keepdims=True) + EPS)
    return (y * w.astype(jnp.float32)).astype(x.dtype)


def rotary_tables(seq):
    inv = jnp.power(ROPE_THETA, -jnp.arange(0, ROPE_DIM, 2, dtype=jnp.float32) / ROPE_DIM)
    ang = jnp.arange(seq, dtype=jnp.float32)[:, None] * inv[None, :]
    return jnp.cos(ang), jnp.sin(ang)


def partial_rotary(x, cos, sin):
    half = ROPE_DIM // 2
    c = cos[:, None, None, :]
    s = sin[:, None, None, :]
    xf = x.astype(jnp.float32)
    x1, x2, xp = xf[..., :half], xf[..., half:ROPE_DIM], xf[..., ROPE_DIM:]
    out = jnp.concatenate([x1 * c - x2 * s, x1 * s + x2 * c, xp], axis=-1)
    return out.astype(x.dtype)


def causal_depthwise_conv(u, w, b):
    out = lax.conv_general_dilated(
        u, w[:, None, :].astype(u.dtype), window_strides=(1,),
        padding=[(CONV_WIDTH - 1, 0)],
        dimension_numbers=('NWC', 'WIO', 'NWC'),
        feature_group_count=u.shape[-1])
    return out + b.astype(u.dtype)


def ssd_branch(z, xbc, dt_raw, conv_w, conv_b, dt_bias, a_log, d_skip, norm_w):
    f32 = jnp.float32
    b, s, _ = z.shape
    nc = s // CHUNK
    G, HG = SSD_GROUPS, SSD_HEADS_PER_GROUP
    xbc = jax.nn.silu(causal_depthwise_conv(xbc, conv_w, conv_b))
    xs, bm, cm = jnp.split(xbc, [SSD_WIDTH, SSD_WIDTH + G * SSD_STATE], axis=-1)
    xs = xs.reshape(b, nc, CHUNK, G, HG, SSD_HEAD_DIM).astype(f32)
    bm = bm.reshape(b, nc, CHUNK, G, SSD_STATE).astype(f32)
    cm = cm.reshape(b, nc, CHUNK, G, SSD_STATE).astype(f32)
    dt = jax.nn.softplus(dt_raw.astype(f32) + dt_bias.astype(f32)).reshape(b, nc, CHUNK, G, HG)
    a = -jnp.exp(a_log.astype(f32)).reshape(G, HG)
    a_cs = jnp.cumsum(dt * a, axis=2)
    xdt = xs * dt[..., None]
    seg = a_cs[:, :, :, None] - a_cs[:, :, None, :]
    causal = jnp.tril(jnp.ones((CHUNK, CHUNK), dtype=bool))[:, :, None, None]
    decay = jnp.exp(jnp.where(causal, seg, -jnp.inf))
    cb = jnp.einsum('bclgn,bcsgn->bclsg', cm, bm)
    y_diag = jnp.einsum('bclsg,bclsgh,bcsghp->bclghp', cb, decay, xdt)
    decay_to_end = jnp.exp(a_cs[:, :, -1:] - a_cs)
    states = jnp.einsum('bclgn,bclgh,bclghp->bcghpn', bm, decay_to_end, xdt)
    chunk_decay = jnp.exp(a_cs[:, :, -1])

    def step(carry, inp):
        st, dc = inp
        return carry * dc[..., None, None] + st, carry

    init = jnp.zeros_like(states[:, 0])
    _, prev = lax.scan(step, init, (jnp.moveaxis(states, 1, 0), jnp.moveaxis(chunk_decay, 1, 0)))
    prev = jnp.moveaxis(prev, 0, 1)
    y_off = jnp.einsum('bclgn,bcghpn,bclgh->bclghp', cm, prev, jnp.exp(a_cs))
    y = y_diag + y_off + xs * d_skip.astype(f32).reshape(G, HG)[:, :, None]
    y = y.reshape(b, s, SSD_WIDTH)
    y = (y * jax.nn.silu(z.astype(f32))).reshape(b, s, G, SSD_WIDTH // G)
    y = y * lax.rsqrt(jnp.mean(y * y, axis=-1, keepdims=True) + EPS)
    y = y.reshape(b, s, SSD_WIDTH) * norm_w.astype(f32)
    return y.astype(z.dtype)


def diff_attention(q, k, v, qn_w, kn_w, lq1, lk1, lq2, lk2, subln_w, lambda_init, cos, sin):
    f32 = jnp.float32
    b, s, _ = q.shape
    nb = s // Q_BLOCK
    q = q.reshape(b, s, DIFF_HEADS, 2, DIFF_QKDIM)
    k = k.reshape(b, s, DIFF_HEADS, 2, DIFF_QKDIM)
    vf = v.reshape(b, s, DIFF_HEADS, DIFF_VDIM).astype(f32)
    q = partial_rotary(rmsnorm(q, qn_w), cos, sin).astype(f32)
    kf = partial_rotary(rmsnorm(k, kn_w), cos, sin).astype(f32)
    lam = (jnp.exp(jnp.sum(lq1.astype(f32) * lk1.astype(f32)))
           - jnp.exp(jnp.sum(lq2.astype(f32) * lk2.astype(f32))) + lambda_init)
    qb = jnp.moveaxis(q.reshape(b, nb, Q_BLOCK, DIFF_HEADS, 2, DIFF_QKDIM), 1, 0)
    k_chunk = jnp.arange(s) // CHUNK
    scale = DIFF_QKDIM ** -0.5

    def block(args):
        qi, i = args
        q_chunk = (i * Q_BLOCK + jnp.arange(Q_BLOCK)) // CHUNK
        mask = k_chunk[None, :] <= q_chunk[:, None]
        sc = jnp.einsum('bqhjd,bkhjd->bhjqk', qi, kf) * scale
        p = jax.nn.softmax(jnp.where(mask, sc, -jnp.inf), axis=-1)
        attn = p[:, :, 0] - lam * p[:, :, 1]
        return jnp.einsum('bhqk,bkhd->bqhd', attn, vf)

    o = lax.map(block, (qb, jnp.arange(nb)))
    o = jnp.moveaxis(o, 0, 1).reshape(b, s, DIFF_HEADS, DIFF_VDIM)
    o = rmsnorm(o, subln_w) * (1.0 - lambda_init)
    return o.reshape(b, s, DIFF_WIDTH).astype(v.dtype)


def peer_ffn(x, w_q, sub_keys, u, v):
    f32 = jnp.float32
    b, s, d = x.shape
    tb = PEER_TOKEN_BLOCK
    xt = x.reshape((b * s) // tb, tb, d)
    K = PEER_TOPK

    def block(xb):
        q = (xb @ w_q).reshape(tb, PEER_HEADS, 2, PEER_HALF).astype(f32)
        sc = jnp.einsum('thjd,hjnd->thjn', q, sub_keys.astype(f32))
        s_half, i_half = lax.top_k(sc, K)
        cand = (s_half[:, :, 0, :, None] + s_half[:, :, 1, None, :]).reshape(tb, PEER_HEADS, K * K)
        cand_idx = (i_half[:, :, 0, :, None] * N_KEYS + i_half[:, :, 1, None, :]).reshape(tb, PEER_HEADS, K * K)
        top_s, top_pos = lax.top_k(cand, K)
        idx = jnp.take_along_axis(cand_idx, top_pos, axis=-1)
        g = jax.nn.softmax(top_s, axis=-1)
        u_sel = u[idx]
        v_sel = v[idx]
        h = jax.nn.gelu(jnp.einsum('td,thkd->thk', xb, u_sel).astype(f32), approximate=False)
        return jnp.einsum('thk,thkd->td', (g * h).astype(xb.dtype), v_sel)

    out = lax.map(block, xt)
    return out.reshape(b, s, d).astype(x.dtype)


def setup_inputs(seed: int = 0) -> dict:
    key = jax.random.key(seed)
    ks = jax.random.split(key, 24)
    f32 = jnp.float32
    nrm = lambda k, shape, sc: jax.random.normal(k, shape, f32) * sc
    L = DEPTH
    dt0 = jnp.exp(jax.random.uniform(ks[5], (L, SSD_HEADS), f32) * (math.log(0.1) - math.log(0.001)) + math.log(0.001))
    return {
        'x': nrm(ks[0], (BATCH, SEQ, D_MODEL), 1.0),
        'mix_norm_w': 1.0 + nrm(ks[1], (L, D_MODEL), 0.05),
        'w_in': nrm(ks[2], (L, D_MODEL, IN_COLS), D_MODEL ** -0.5),
        'conv_w': nrm(ks[3], (L, CONV_WIDTH, SSD_CONV_CH), CONV_WIDTH ** -0.5),
        'conv_b': nrm(ks[4], (L, SSD_CONV_CH), 0.02),
        'dt_bias': dt0 + jnp.log(-jnp.expm1(-dt0)),
        'a_log': jnp.log(jax.random.uniform(ks[6], (L, SSD_HEADS), f32, 1.0, 16.0)),
        'd_skip': 1.0 + nrm(ks[7], (L, SSD_HEADS), 0.1),
        'ssd_norm_w': 1.0 + nrm(ks[8], (L, SSD_WIDTH), 0.05),
        'q_norm_w': 1.0 + nrm(ks[9], (L, DIFF_QKDIM), 0.05),
        'k_norm_w': 1.0 + nrm(ks[10], (L, DIFF_QKDIM), 0.05),
        'lambda_q1': nrm(ks[11], (L, DIFF_QKDIM), 0.1),
        'lambda_k1': nrm(ks[12], (L, DIFF_QKDIM), 0.1),
        'lambda_q2': nrm(ks[13], (L, DIFF_QKDIM), 0.1),
        'lambda_k2': nrm(ks[14], (L, DIFF_QKDIM), 0.1),
        'subln_w': 1.0 + nrm(ks[15], (L, DIFF_VDIM), 0.05),
        'w_out': nrm(ks[16], (L, MIX_WIDTH, D_MODEL), MIX_WIDTH ** -0.5),
        'ffn_norm_w': 1.0 + nrm(ks[17], (L, D_MODEL), 0.05),
        'peer_w_q': nrm(ks[18], (L, D_MODEL, PEER_HEADS * PEER_KEY_DIM), D_MODEL ** -0.5),
        'peer_sub_keys': nrm(ks[19], (L, PEER_HEADS, 2, N_KEYS, PEER_HALF), PEER_HALF ** -0.5),
        'peer_u': nrm(ks[20], (L, N_EXPERTS, D_MODEL), D_MODEL ** -0.5),
        'peer_v': nrm(ks[21], (L, N_EXPERTS, D_MODEL), 0.5),
    }


def reference(x, mix_norm_w, w_in, conv_w, conv_b, dt_bias, a_log, d_skip, ssd_norm_w,
              q_norm_w, k_norm_w, lambda_q1, lambda_k1, lambda_q2, lambda_k2, subln_w,
              w_out, ffn_norm_w, peer_w_q, peer_sub_keys, peer_u, peer_v):
    cos, sin = rotary_tables(x.shape[1])
    for layer in range(DEPTH):
        lambda_init = 0.8 - 0.6 * math.exp(-0.3 * layer)
        hn = rmsnorm(x, mix_norm_w[layer])
        proj = hn @ w_in[layer]
        z, xbc, dt_raw, q, k, v = jnp.split(proj, SPLITS, axis=-1)
        y_ssd = ssd_branch(z, xbc, dt_raw, conv_w[layer], conv_b[layer], dt_bias[layer],
                           a_log[layer], d_skip[layer], ssd_norm_w[layer])
        y_diff = diff_attention(q, k, v, q_norm_w[layer], k_norm_w[layer], lambda_q1[layer],
                                lambda_k1[layer], lambda_q2[layer], lambda_k2[layer],
                                subln_w[layer], lambda_init, cos, sin)
        x = x + jnp.concatenate([y_ssd, y_diff], axis=-1) @ w_out[layer]
        x = x + peer_ffn(rmsnorm(x, ffn_norm_w[layer]), peer_w_q[layer], peer_sub_keys[layer],
                         peer_u[layer], peer_v[layer])
    return x
```

```python
import functools
import math

import jax
import jax.numpy as jnp
from jax import lax
from jax.experimental import pallas as pl
from jax.experimental.pallas import tpu as pltpu

D_MODEL = 1024
CHUNK = 64
EPS = 1e-6

SSD_WIDTH = D_MODEL // 2
SSD_HEAD_DIM = 64
SSD_HEADS = SSD_WIDTH // SSD_HEAD_DIM
SSD_GROUPS = 2
SSD_HEADS_PER_GROUP = SSD_HEADS // SSD_GROUPS
SSD_STATE = 128
CONV_WIDTH = 4
SSD_CONV_CH = SSD_WIDTH + 2 * SSD_GROUPS * SSD_STATE

DIFF_WIDTH = D_MODEL - SSD_WIDTH
DIFF_HEADS = 4
DIFF_VDIM = DIFF_WIDTH // DIFF_HEADS
DIFF_QKDIM = DIFF_VDIM // 2
ROPE_DIM = DIFF_QKDIM // 4
ROPE_THETA = 500000.0
Q_BLOCK = 128

QK_COLS = DIFF_HEADS * 2 * DIFF_QKDIM
SPLITS = (SSD_WIDTH,
          SSD_WIDTH + SSD_CONV_CH,
          SSD_WIDTH + SSD_CONV_CH + SSD_HEADS,
          SSD_WIDTH + SSD_CONV_CH + SSD_HEADS + QK_COLS,
          SSD_WIDTH + SSD_CONV_CH + SSD_HEADS + 2 * QK_COLS)
IN_COLS = SPLITS[-1] + DIFF_WIDTH
MIX_WIDTH = SSD_WIDTH + DIFF_WIDTH

PEER_HEADS = 8
N_KEYS = 128
N_EXPERTS = N_KEYS * N_KEYS
PEER_KEY_DIM = 256
PEER_HALF = PEER_KEY_DIM // 2
PEER_TOPK = 16
PEER_TOKEN_BLOCK = 128


def _norm_proj_kernel(x_ref, nw_ref, w_ref, o_ref):
    xf = x_ref[...]
    y = xf * lax.rsqrt(jnp.mean(xf * xf, axis=-1, keepdims=True) + EPS)
    y = (y * nw_ref[...]).astype(jnp.bfloat16)
    o_ref[...] = jnp.dot(y, w_ref[...], preferred_element_type=jnp.float32)


def norm_proj(x2d, norm_w, w_bf16, tm=512):
    m, d = x2d.shape
    n = w_bf16.shape[1]
    return pl.pallas_call(
        _norm_proj_kernel,
        grid=(m // tm,),
        in_specs=[
            pl.BlockSpec((tm, d), lambda i: (i, 0)),
            pl.BlockSpec((1, d), lambda i: (0, 0)),
            pl.BlockSpec((d, n), lambda i: (0, 0)),
        ],
        out_specs=pl.BlockSpec((tm, n), lambda i: (i, 0)),
        out_shape=jax.ShapeDtypeStruct((m, n), jnp.float32),
        compiler_params=pltpu.CompilerParams(
            dimension_semantics=("arbitrary",),
            vmem_limit_bytes=56 * 1024 * 1024),
        name="norm_proj",
    )(x2d, norm_w.reshape(1, d), w_bf16)


def rmsnorm(x, w):
    xf = x.astype(jnp.float32)
    y = xf * lax.rsqrt(jnp.mean(xf * xf, axis=-1, keepdims=True) + EPS)
    return (y * w.astype(jnp.float32)).astype(x.dtype)


def rotary_tables(seq):
    inv = jnp.power(ROPE_THETA, -jnp.arange(0, ROPE_DIM, 2, dtype=jnp.float32) / ROPE_DIM)
    ang = jnp.arange(seq, dtype=jnp.float32)[:, None] * inv[None, :]
    return jnp.cos(ang), jnp.sin(ang)


def partial_rotary(x, cos, sin):
    half = ROPE_DIM // 2
    c = cos[:, None, None, :]
    s = sin[:, None, None, :]
    xf = x.astype(jnp.float32)
    x1, x2, xp = xf[..., :half], xf[..., half:ROPE_DIM], xf[..., ROPE_DIM:]
    out = jnp.concatenate([x1 * c - x2 * s, x1 * s + x2 * c, xp], axis=-1)
    return out.astype(x.dtype)


def causal_depthwise_conv(u, w, b):
    out = lax.conv_general_dilated(
        u, w[:, None, :].astype(u.dtype), window_strides=(1,),
        padding=[(CONV_WIDTH - 1, 0)],
        dimension_numbers=('NWC', 'WIO', 'NWC'),
        feature_group_count=u.shape[-1])
    return out + b.astype(u.dtype)


def ssd_branch(z, xbc, dt_raw, conv_w, conv_b, dt_bias, a_log, d_skip, norm_w):
    f32 = jnp.float32
    b, s, _ = z.shape
    nc = s // CHUNK
    G, HG = SSD_GROUPS, SSD_HEADS_PER_GROUP
    xbc = jax.nn.silu(causal_depthwise_conv(xbc, conv_w, conv_b))
    xs, bm, cm = jnp.split(xbc, [SSD_WIDTH, SSD_WIDTH + G * SSD_STATE], axis=-1)
    xs = xs.reshape(b, nc, CHUNK, G, HG, SSD_HEAD_DIM).astype(f32)
    bm = bm.reshape(b, nc, CHUNK, G, SSD_STATE).astype(f32)
    cm = cm.reshape(b, nc, CHUNK, G, SSD_STATE).astype(f32)
    dt = jax.nn.softplus(dt_raw.astype(f32) + dt_bias.astype(f32)).reshape(b, nc, CHUNK, G, HG)
    a = -jnp.exp(a_log.astype(f32)).reshape(G, HG)
    a_cs = jnp.cumsum(dt * a, axis=2)
    xdt = xs * dt[..., None]
    seg = a_cs[:, :, :, None] - a_cs[:, :, None, :]
    causal = jnp.tril(jnp.ones((CHUNK, CHUNK), dtype=bool))[:, :, None, None]
    decay = jnp.exp(jnp.where(causal, seg, -jnp.inf))
    cb = jnp.einsum('bclgn,bcsgn->bclsg', cm, bm)
    y_diag = jnp.einsum('bclsg,bclsgh,bcsghp->bclghp', cb, decay, xdt)
    decay_to_end = jnp.exp(a_cs[:, :, -1:] - a_cs)
    states = jnp.einsum('bclgn,bclgh,bclghp->bcghpn', bm, decay_to_end, xdt)
    chunk_decay = jnp.exp(a_cs[:, :, -1])

    def step(carry, inp):
        st, dc = inp
        return carry * dc[..., None, None] + st, carry

    init = jnp.zeros_like(states[:, 0])
    _, prev = lax.scan(step, init, (jnp.moveaxis(states, 1, 0), jnp.moveaxis(chunk_decay, 1, 0)))
    prev = jnp.moveaxis(prev, 0, 1)
    y_off = jnp.einsum('bclgn,bcghpn,bclgh->bclghp', cm, prev, jnp.exp(a_cs))
    y = y_diag + y_off + xs * d_skip.astype(f32).reshape(G, HG)[:, :, None]
    y = y.reshape(b, s, SSD_WIDTH)
    y = (y * jax.nn.silu(z.astype(f32))).reshape(b, s, G, SSD_WIDTH // G)
    y = y * lax.rsqrt(jnp.mean(y * y, axis=-1, keepdims=True) + EPS)
    y = y.reshape(b, s, SSD_WIDTH) * norm_w.astype(f32)
    return y.astype(z.dtype)


def diff_attention(q, k, v, qn_w, kn_w, lq1, lk1, lq2, lk2, subln_w, lambda_init, cos, sin):
    f32 = jnp.float32
    b, s, _ = q.shape
    nb = s // Q_BLOCK
    q = q.reshape(b, s, DIFF_HEADS, 2, DIFF_QKDIM)
    k = k.reshape(b, s, DIFF_HEADS, 2, DIFF_QKDIM)
    vf = v.reshape(b, s, DIFF_HEADS, DIFF_VDIM).astype(f32)
    q = partial_rotary(rmsnorm(q, qn_w), cos, sin).astype(f32)
    kf = partial_rotary(rmsnorm(k, kn_w), cos, sin).astype(f32)
    lam = (jnp.exp(jnp.sum(lq1.astype(f32) * lk1.astype(f32)))
           - jnp.exp(jnp.sum(lq2.astype(f32) * lk2.astype(f32))) + lambda_init)
    qb = jnp.moveaxis(q.reshape(b, nb, Q_BLOCK, DIFF_HEADS, 2, DIFF_QKDIM), 1, 0)
    k_chunk = jnp.arange(s) // CHUNK
    scale = DIFF_QKDIM ** -0.5

    def block(args):
        qi, i = args
        q_chunk = (i * Q_BLOCK + jnp.arange(Q_BLOCK)) // CHUNK
        mask = k_chunk[None, :] <= q_chunk[:, None]
        sc = jnp.einsum('bqhjd,bkhjd->bhjqk', qi, kf) * scale
        p = jax.nn.softmax(jnp.where(mask, sc, -jnp.inf), axis=-1)
        attn = p[:, :, 0] - lam * p[:, :, 1]
        return jnp.einsum('bhqk,bkhd->bqhd', attn, vf)

    o = lax.map(block, (qb, jnp.arange(nb)))
    o = jnp.moveaxis(o, 0, 1).reshape(b, s, DIFF_HEADS, DIFF_VDIM)
    o = rmsnorm(o, subln_w) * (1.0 - lambda_init)
    return o.reshape(b, s, DIFF_WIDTH).astype(v.dtype)


def peer_ffn(x, w_q, sub_keys, u, v):
    f32 = jnp.float32
    b, s, d = x.shape
    tb = PEER_TOKEN_BLOCK
    xt = x.reshape((b * s) // tb, tb, d)
    K = PEER_TOPK

    def block(xb):
        q = (xb @ w_q).reshape(tb, PEER_HEADS, 2, PEER_HALF).astype(f32)
        sc = jnp.einsum('thjd,hjnd->thjn', q, sub_keys.astype(f32))
        s_half, i_half = lax.top_k(sc, K)
        cand = (s_half[:, :, 0, :, None] + s_half[:, :, 1, None, :]).reshape(tb, PEER_HEADS, K * K)
        cand_idx = (i_half[:, :, 0, :, None] * N_KEYS + i_half[:, :, 1, None, :]).reshape(tb, PEER_HEADS, K * K)
        top_s, top_pos = lax.top_k(cand, K)
        idx = jnp.take_along_axis(cand_idx, top_pos, axis=-1)
        g = jax.nn.softmax(top_s, axis=-1)
        u_sel = u[idx]
        v_sel = v[idx]
        h = jax.nn.gelu(jnp.einsum('td,thkd->thk', xb, u_sel).astype(f32), approximate=False)
        return jnp.einsum('thk,thkd->td', (g * h).astype(xb.dtype), v_sel)

    out = lax.map(block, xt)
    return out.reshape(b, s, d).astype(x.dtype)


def kernel(x, mix_norm_w, w_in, conv_w, conv_b, dt_bias, a_log, d_skip, ssd_norm_w, q_norm_w, k_norm_w, lambda_q1, lambda_k1, lambda_q2, lambda_k2, subln_w, w_out, ffn_norm_w, peer_w_q, peer_sub_keys, peer_u, peer_v):
    b, s, d = x.shape
    cos, sin = rotary_tables(s)
    layer = 0
    lambda_init = 0.8 - 0.6 * math.exp(-0.3 * layer)
    proj = norm_proj(x.reshape(b * s, d), mix_norm_w[layer],
                     w_in[layer].astype(jnp.bfloat16)).reshape(b, s, IN_COLS)
    z, xbc, dt_raw, q, k, v = jnp.split(proj, SPLITS, axis=-1)
    y_ssd = ssd_branch(z, xbc, dt_raw, conv_w[layer], conv_b[layer], dt_bias[layer],
                       a_log[layer], d_skip[layer], ssd_norm_w[layer])
    y_diff = diff_attention(q, k, v, q_norm_w[layer], k_norm_w[layer], lambda_q1[layer],
                            lambda_k1[layer], lambda_q2[layer], lambda_k2[layer],
                            subln_w[layer], lambda_init, cos, sin)
    x = x + jnp.concatenate([y_ssd, y_diff], axis=-1) @ w_out[layer]
    x = x + peer_ffn(rmsnorm(x, ffn_norm_w[layer]), peer_w_q[layer], peer_sub_keys[layer],
                     peer_u[layer], peer_v[layer])
    return x
```

```python
import functools
import math

import jax
import jax.numpy as jnp
from jax import lax
from jax.experimental import pallas as pl
from jax.experimental.pallas import tpu as pltpu

D_MODEL = 1024
CHUNK = 64
EPS = 1e-6

SSD_WIDTH = D_MODEL // 2
SSD_HEAD_DIM = 64
SSD_HEADS = SSD_WIDTH // SSD_HEAD_DIM
SSD_GROUPS = 2
SSD_HEADS_PER_GROUP = SSD_HEADS // SSD_GROUPS
SSD_STATE = 128
CONV_WIDTH = 4
SSD_CONV_CH = SSD_WIDTH + 2 * SSD_GROUPS * SSD_STATE

DIFF_WIDTH = D_MODEL - SSD_WIDTH
DIFF_HEADS = 4
DIFF_VDIM = DIFF_WIDTH // DIFF_HEADS
DIFF_QKDIM = DIFF_VDIM // 2
ROPE_DIM = DIFF_QKDIM // 4
ROPE_THETA = 500000.0
Q_BLOCK = 128

QK_COLS = DIFF_HEADS * 2 * DIFF_QKDIM
SPLITS = (SSD_WIDTH,
          SSD_WIDTH + SSD_CONV_CH,
          SSD_WIDTH + SSD_CONV_CH + SSD_HEADS,
          SSD_WIDTH + SSD_CONV_CH + SSD_HEADS + QK_COLS,
          SSD_WIDTH + SSD_CONV_CH + SSD_HEADS + 2 * QK_COLS)
IN_COLS = SPLITS[-1] + DIFF_WIDTH
MIX_WIDTH = SSD_WIDTH + DIFF_WIDTH

PEER_HEADS = 8
N_KEYS = 128
N_EXPERTS = N_KEYS * N_KEYS
PEER_KEY_DIM = 256
PEER_HALF = PEER_KEY_DIM // 2
PEER_TOPK = 16
PEER_TOKEN_BLOCK = 128


def _norm_proj_kernel(x_ref, nw_ref, w_ref, o_ref):
    xf = x_ref[...]
    y = xf * lax.rsqrt(jnp.mean(xf * xf, axis=-1, keepdims=True) + EPS)
    y = (y * nw_ref[...]).astype(jnp.bfloat16)
    o_ref[...] = jnp.dot(y, w_ref[...], preferred_element_type=jnp.float32)


def norm_proj(x2d, norm_w, w_bf16, tm=512):
    m, d = x2d.shape
    n = w_bf16.shape[1]
    return pl.pallas_call(
        _norm_proj_kernel,
        grid=(m // tm,),
        in_specs=[
            pl.BlockSpec((tm, d), lambda i: (i, 0)),
            pl.BlockSpec((1, d), lambda i: (0, 0)),
            pl.BlockSpec((d, n), lambda i: (0, 0)),
        ],
        out_specs=pl.BlockSpec((tm, n), lambda i: (i, 0)),
        out_shape=jax.ShapeDtypeStruct((m, n), jnp.float32),
        compiler_params=pltpu.CompilerParams(
            dimension_semantics=("arbitrary",),
            vmem_limit_bytes=56 * 1024 * 1024),
        name="norm_proj",
    )(x2d, norm_w.reshape(1, d), w_bf16)


PEER_NSEL = PEER_HEADS * PEER_TOPK
PEER_TB = 8
LANES = 128
SUBLANES = 8
_SQRT_HALF = 0.7071067811865476


def _peer_gather_kernel(idx_cur, idx_nxt, xn_ref, g_ref, xres_ref, u_hbm, v_hbm,
                        o_ref, ubuf, vbuf, sem):
    i = pl.program_id(0)
    nb = pl.num_programs(0)
    slot = i % 2

    def issue(idx_ref, s):
        for tt in range(PEER_TB):
            def body(kk, carry, tt=tt):
                k0 = kk * SUBLANES
                for j in range(SUBLANES):
                    e = idx_ref[tt * PEER_NSEL + k0 + j]
                    pltpu.make_async_copy(u_hbm.at[pl.ds(e, 1)],
                                          ubuf.at[s, tt, kk, pl.ds(j, 1)],
                                          sem.at[0, s]).start()
                    pltpu.make_async_copy(v_hbm.at[pl.ds(e, 1)],
                                          vbuf.at[s, tt, kk, pl.ds(j, 1)],
                                          sem.at[1, s]).start()
                return carry
            lax.fori_loop(0, PEER_NSEL // SUBLANES, body, 0)

    @pl.when(i == 0)
    def _():
        issue(idx_cur, 0)

    @pl.when(i + 1 < nb)
    def _():
        issue(idx_nxt, 1 - slot)

    pltpu.make_async_copy(ubuf.at[slot], ubuf.at[slot], sem.at[0, slot]).wait()
    pltpu.make_async_copy(vbuf.at[slot], vbuf.at[slot], sem.at[1, slot]).wait()

    d = xn_ref.shape[-1]
    lane = lax.broadcasted_iota(jnp.int32, (PEER_NSEL, PEER_TB), 1)
    hm = jnp.zeros((PEER_NSEL, PEER_TB), jnp.float32)
    for tt in range(PEER_TB):
        prod = ubuf[slot, tt].reshape(PEER_NSEL, d) * xn_ref[tt:tt + 1, :]
        part = prod[:, 0:LANES]
        for c in range(1, d // LANES):
            part = part + prod[:, c * LANES:(c + 1) * LANES]
        h = jnp.sum(part, axis=1, keepdims=True)
        hm = jnp.where(lane == tt, h, hm)
    eye = (lax.broadcasted_iota(jnp.int32, (PEER_NSEL, PEER_NSEL), 0)
           == lax.broadcasted_iota(jnp.int32, (PEER_NSEL, PEER_NSEL), 1)).astype(jnp.float32)
    g_t = lax.dot_general(eye, g_ref[...], (((1,), (1,)), ((), ())),
                          precision=lax.Precision.HIGHEST,
                          preferred_element_type=jnp.float32)
    w = g_t * (0.5 * hm * (1.0 + lax.erf(hm * _SQRT_HALF)))
    for tt in range(PEER_TB):
        acc = jnp.sum(vbuf[slot, tt].reshape(PEER_NSEL, d) * w[:, tt:tt + 1],
                      axis=0, keepdims=True)
        o_ref[tt:tt + 1, :] = xres_ref[tt:tt + 1, :] + acc


def peer_gather(xn, idx, gates, xres, u, v):
    m, d = xn.shape
    nb = m // PEER_TB
    tok = lambda i: (i, 0)
    idx_flat = idx.reshape(m * PEER_NSEL)
    return pl.pallas_call(
        _peer_gather_kernel,
        grid=(nb,),
        in_specs=[
            pl.BlockSpec((PEER_TB * PEER_NSEL,), lambda i: (i,), memory_space=pltpu.SMEM),
            pl.BlockSpec((PEER_TB * PEER_NSEL,), lambda i: (jnp.minimum(i + 1, nb - 1),),
                         memory_space=pltpu.SMEM),
            pl.BlockSpec((PEER_TB, d), tok),
            pl.BlockSpec((PEER_TB, PEER_NSEL), tok),
            pl.BlockSpec((PEER_TB, d), tok),
            pl.BlockSpec(memory_space=pl.ANY),
            pl.BlockSpec(memory_space=pl.ANY),
        ],
        out_specs=pl.BlockSpec((PEER_TB, d), tok),
        out_shape=jax.ShapeDtypeStruct((m, d), jnp.float32),
        scratch_shapes=[
            pltpu.VMEM((2, PEER_TB, PEER_NSEL // SUBLANES, SUBLANES, d), jnp.float32),
            pltpu.VMEM((2, PEER_TB, PEER_NSEL // SUBLANES, SUBLANES, d), jnp.float32),
            pltpu.SemaphoreType.DMA((2, 2)),
        ],
        compiler_params=pltpu.CompilerParams(
            dimension_semantics=("arbitrary",),
            vmem_limit_bytes=40 * 1024 * 1024,
            disable_bounds_checks=True),
        name="peer_gather",
    )(idx_flat, idx_flat, xn, gates, xres, u, v)


def rmsnorm(x, w):
    xf = x.astype(jnp.float32)
    y = xf * lax.rsqrt(jnp.mean(xf * xf, axis=-1, keepdims=True) + EPS)
    return (y * w.astype(jnp.float32)).astype(x.dtype)


def rotary_tables(seq):
    inv = jnp.power(ROPE_THETA, -jnp.arange(0, ROPE_DIM, 2, dtype=jnp.float32) / ROPE_DIM)
    ang = jnp.arange(seq, dtype=jnp.float32)[:, None] * inv[None, :]
    return jnp.cos(ang), jnp.sin(ang)


def partial_rotary(x, cos, sin):
    half = ROPE_DIM // 2
    c = cos[:, None, None, :]
    s = sin[:, None, None, :]
    xf = x.astype(jnp.float32)
    x1, x2, xp = xf[..., :half], xf[..., half:ROPE_DIM], xf[..., ROPE_DIM:]
    out = jnp.concatenate([x1 * c - x2 * s, x1 * s + x2 * c, xp], axis=-1)
    return out.astype(x.dtype)


def causal_depthwise_conv(u, w, b):
    out = lax.conv_general_dilated(
        u, w[:, None, :].astype(u.dtype), window_strides=(1,),
        padding=[(CONV_WIDTH - 1, 0)],
        dimension_numbers=('NWC', 'WIO', 'NWC'),
        feature_group_count=u.shape[-1])
    return out + b.astype(u.dtype)


def ssd_branch(z, xbc, dt_raw, conv_w, conv_b, dt_bias, a_log, d_skip, norm_w):
    f32 = jnp.float32
    b, s, _ = z.shape
    nc = s // CHUNK
    G, HG = SSD_GROUPS, SSD_HEADS_PER_GROUP
    xbc = jax.nn.silu(causal_depthwise_conv(xbc, conv_w, conv_b))
    xs, bm, cm = jnp.split(xbc, [SSD_WIDTH, SSD_WIDTH + G * SSD_STATE], axis=-1)
    xs = xs.reshape(b, nc, CHUNK, G, HG, SSD_HEAD_DIM).astype(f32)
    bm = bm.reshape(b, nc, CHUNK, G, SSD_STATE).astype(f32)
    cm = cm.reshape(b, nc, CHUNK, G, SSD_STATE).astype(f32)
    dt = jax.nn.softplus(dt_raw.astype(f32) + dt_bias.astype(f32)).reshape(b, nc, CHUNK, G, HG)
    a = -jnp.exp(a_log.astype(f32)).reshape(G, HG)
    a_cs = jnp.cumsum(dt * a, axis=2)
    xdt = xs * dt[..., None]
    seg = a_cs[:, :, :, None] - a_cs[:, :, None, :]
    causal = jnp.tril(jnp.ones((CHUNK, CHUNK), dtype=bool))[:, :, None, None]
    decay = jnp.exp(jnp.where(causal, seg, -jnp.inf))
    cb = jnp.einsum('bclgn,bcsgn->bclsg', cm, bm)
    y_diag = jnp.einsum('bclsg,bclsgh,bcsghp->bclghp', cb, decay, xdt)
    decay_to_end = jnp.exp(a_cs[:, :, -1:] - a_cs)
    states = jnp.einsum('bclgn,bclgh,bclghp->bcghpn', bm, decay_to_end, xdt)
    chunk_decay = jnp.exp(a_cs[:, :, -1])

    def step(carry, inp):
        st, dc = inp
        return carry * dc[..., None, None] + st, carry

    init = jnp.zeros_like(states[:, 0])
    _, prev = lax.scan(step, init, (jnp.moveaxis(states, 1, 0), jnp.moveaxis(chunk_decay, 1, 0)))
    prev = jnp.moveaxis(prev, 0, 1)
    y_off = jnp.einsum('bclgn,bcghpn,bclgh->bclghp', cm, prev, jnp.exp(a_cs))
    y = y_diag + y_off + xs * d_skip.astype(f32).reshape(G, HG)[:, :, None]
    y = y.reshape(b, s, SSD_WIDTH)
    y = (y * jax.nn.silu(z.astype(f32))).reshape(b, s, G, SSD_WIDTH // G)
    y = y * lax.rsqrt(jnp.mean(y * y, axis=-1, keepdims=True) + EPS)
    y = y.reshape(b, s, SSD_WIDTH) * norm_w.astype(f32)
    return y.astype(z.dtype)


def diff_attention(q, k, v, qn_w, kn_w, lq1, lk1, lq2, lk2, subln_w, lambda_init, cos, sin):
    f32 = jnp.float32
    b, s, _ = q.shape
    nb = s // Q_BLOCK
    q = q.reshape(b, s, DIFF_HEADS, 2, DIFF_QKDIM)
    k = k.reshape(b, s, DIFF_HEADS, 2, DIFF_QKDIM)
    vf = v.reshape(b, s, DIFF_HEADS, DIFF_VDIM).astype(f32)
    q = partial_rotary(rmsnorm(q, qn_w), cos, sin).astype(f32)
    kf = partial_rotary(rmsnorm(k, kn_w), cos, sin).astype(f32)
    lam = (jnp.exp(jnp.sum(lq1.astype(f32) * lk1.astype(f32)))
           - jnp.exp(jnp.sum(lq2.astype(f32) * lk2.astype(f32))) + lambda_init)
    qb = jnp.moveaxis(q.reshape(b, nb, Q_BLOCK, DIFF_HEADS, 2, DIFF_QKDIM), 1, 0)
    k_chunk = jnp.arange(s) // CHUNK
    scale = DIFF_QKDIM ** -0.5

    def block(args):
        qi, i = args
        q_chunk = (i * Q_BLOCK + jnp.arange(Q_BLOCK)) // CHUNK
        mask = k_chunk[None, :] <= q_chunk[:, None]
        sc = jnp.einsum('bqhjd,bkhjd->bhjqk', qi, kf) * scale
        p = jax.nn.softmax(jnp.where(mask, sc, -jnp.inf), axis=-1)
        attn = p[:, :, 0] - lam * p[:, :, 1]
        return jnp.einsum('bhqk,bkhd->bqhd', attn, vf)

    o = lax.map(block, (qb, jnp.arange(nb)))
    o = jnp.moveaxis(o, 0, 1).reshape(b, s, DIFF_HEADS, DIFF_VDIM)
    o = rmsnorm(o, subln_w) * (1.0 - lambda_init)
    return o.reshape(b, s, DIFF_WIDTH).astype(v.dtype)


def peer_route(xn, w_q, sub_keys):
    f32 = jnp.float32
    m = xn.shape[0]
    K = PEER_TOPK
    q = (xn @ w_q).reshape(m, PEER_HEADS, 2, PEER_HALF).astype(f32)
    sc = jnp.einsum('thjd,hjnd->thjn', q, sub_keys.astype(f32))
    s_half, i_half = lax.top_k(sc, K)
    cand = (s_half[:, :, 0, :, None] + s_half[:, :, 1, None, :]).reshape(m, PEER_HEADS, K * K)
    cand_idx = (i_half[:, :, 0, :, None] * N_KEYS + i_half[:, :, 1, None, :]).reshape(m, PEER_HEADS, K * K)
    top_s, top_pos = lax.top_k(cand, K)
    idx = jnp.take_along_axis(cand_idx, top_pos, axis=-1)
    g = jax.nn.softmax(top_s, axis=-1)
    return idx.reshape(m, PEER_NSEL).astype(jnp.int32), g.reshape(m, PEER_NSEL)


def kernel(x, mix_norm_w, w_in, conv_w, conv_b, dt_bias, a_log, d_skip, ssd_norm_w, q_norm_w, k_norm_w, lambda_q1, lambda_k1, lambda_q2, lambda_k2, subln_w, w_out, ffn_norm_w, peer_w_q, peer_sub_keys, peer_u, peer_v):
    b, s, d = x.shape
    cos, sin = rotary_tables(s)
    layer = 0
    lambda_init = 0.8 - 0.6 * math.exp(-0.3 * layer)
    proj = norm_proj(x.reshape(b * s, d), mix_norm_w[layer],
                     w_in[layer].astype(jnp.bfloat16)).reshape(b, s, IN_COLS)
    z, xbc, dt_raw, q, k, v = jnp.split(proj, SPLITS, axis=-1)
    y_ssd = ssd_branch(z, xbc, dt_raw, conv_w[layer], conv_b[layer], dt_bias[layer],
                       a_log[layer], d_skip[layer], ssd_norm_w[layer])
    y_diff = diff_attention(q, k, v, q_norm_w[layer], k_norm_w[layer], lambda_q1[layer],
                            lambda_k1[layer], lambda_q2[layer], lambda_k2[layer],
                            subln_w[layer], lambda_init, cos, sin)
    x = x + jnp.concatenate([y_ssd, y_diff], axis=-1) @ w_out[layer]
    x2 = x.reshape(b * s, d)
    xn = rmsnorm(x2, ffn_norm_w[layer])
    idx, gates = peer_route(xn, peer_w_q[layer], peer_sub_keys[layer])
    out = peer_gather(xn, idx, gates, x2, peer_u[layer], peer_v[layer])
    return out.reshape(b, s, d)
```

```python
import functools
import math

import jax
import jax.numpy as jnp
from jax import lax
from jax.experimental import pallas as pl
from jax.experimental.pallas import tpu as pltpu

D_MODEL = 1024
CHUNK = 64
EPS = 1e-6

SSD_WIDTH = D_MODEL // 2
SSD_HEAD_DIM = 64
SSD_HEADS = SSD_WIDTH // SSD_HEAD_DIM
SSD_GROUPS = 2
SSD_HEADS_PER_GROUP = SSD_HEADS // SSD_GROUPS
SSD_STATE = 128
CONV_WIDTH = 4
SSD_CONV_CH = SSD_WIDTH + 2 * SSD_GROUPS * SSD_STATE

DIFF_WIDTH = D_MODEL - SSD_WIDTH
DIFF_HEADS = 4
DIFF_VDIM = DIFF_WIDTH // DIFF_HEADS
DIFF_QKDIM = DIFF_VDIM // 2
ROPE_DIM = DIFF_QKDIM // 4
ROPE_THETA = 500000.0
Q_BLOCK = 128

QK_COLS = DIFF_HEADS * 2 * DIFF_QKDIM
SPLITS = (SSD_WIDTH,
          SSD_WIDTH + SSD_CONV_CH,
          SSD_WIDTH + SSD_CONV_CH + SSD_HEADS,
          SSD_WIDTH + SSD_CONV_CH + SSD_HEADS + QK_COLS,
          SSD_WIDTH + SSD_CONV_CH + SSD_HEADS + 2 * QK_COLS)
IN_COLS = SPLITS[-1] + DIFF_WIDTH
MIX_WIDTH = SSD_WIDTH + DIFF_WIDTH

PEER_HEADS = 8
N_KEYS = 128
N_EXPERTS = N_KEYS * N_KEYS
PEER_KEY_DIM = 256
PEER_HALF = PEER_KEY_DIM // 2
PEER_TOPK = 16
PEER_TOKEN_BLOCK = 128


def _norm_proj_kernel(x_ref, nw_ref, w_ref, o_ref):
    xf = x_ref[...]
    y = xf * lax.rsqrt(jnp.mean(xf * xf, axis=-1, keepdims=True) + EPS)
    y = (y * nw_ref[...]).astype(jnp.bfloat16)
    o_ref[...] = jnp.dot(y, w_ref[...], preferred_element_type=jnp.float32)


def norm_proj(x2d, norm_w, w_bf16, tm=512):
    m, d = x2d.shape
    n = w_bf16.shape[1]
    return pl.pallas_call(
        _norm_proj_kernel,
        grid=(m // tm,),
        in_specs=[
            pl.BlockSpec((tm, d), lambda i: (i, 0)),
            pl.BlockSpec((1, d), lambda i: (0, 0)),
            pl.BlockSpec((d, n), lambda i: (0, 0)),
        ],
        out_specs=pl.BlockSpec((tm, n), lambda i: (i, 0)),
        out_shape=jax.ShapeDtypeStruct((m, n), jnp.float32),
        compiler_params=pltpu.CompilerParams(
            dimension_semantics=("arbitrary",),
            vmem_limit_bytes=56 * 1024 * 1024),
        name="norm_proj",
    )(x2d, norm_w.reshape(1, d), w_bf16)


PEER_NSEL = PEER_HEADS * PEER_TOPK
PEER_TB = 8
LANES = 128
SUBLANES = 8
_SQRT_HALF = 0.7071067811865476


def _peer_gather_kernel(idx_cur, idx_nxt, xn_ref, g_ref, xres_ref, uv_hbm,
                        o_ref, buf, sem):
    i = pl.program_id(0)
    nb = pl.num_programs(0)
    slot = i % 2

    def issue(idx_ref, s):
        for tt in range(PEER_TB):
            def body(kk, carry, tt=tt):
                k0 = kk * SUBLANES
                for j in range(SUBLANES):
                    e = idx_ref[tt * PEER_NSEL + k0 + j]
                    pltpu.make_async_copy(uv_hbm.at[pl.ds(e, 1)],
                                          buf.at[s, tt, kk, pl.ds(j, 1)],
                                          sem.at[s]).start()
                return carry
            lax.fori_loop(0, PEER_NSEL // SUBLANES, body, 0)

    @pl.when(i == 0)
    def _():
        issue(idx_cur, 0)

    @pl.when(i + 1 < nb)
    def _():
        issue(idx_nxt, 1 - slot)

    pltpu.make_async_copy(buf.at[slot], buf.at[slot], sem.at[slot]).wait()

    d = xn_ref.shape[-1]
    lane = lax.broadcasted_iota(jnp.int32, (PEER_NSEL, PEER_TB), 1)
    hm = jnp.zeros((PEER_NSEL, PEER_TB), jnp.float32)
    for tt in range(PEER_TB):
        urows = buf[slot, tt, :, :, 0:d].reshape(PEER_NSEL, d)
        prod = urows * xn_ref[tt:tt + 1, :]
        part = prod[:, 0:LANES]
        for c in range(1, d // LANES):
            part = part + prod[:, c * LANES:(c + 1) * LANES]
        h = jnp.sum(part, axis=1, keepdims=True)
        hm = jnp.where(lane == tt, h, hm)
    eye = (lax.broadcasted_iota(jnp.int32, (PEER_NSEL, PEER_NSEL), 0)
           == lax.broadcasted_iota(jnp.int32, (PEER_NSEL, PEER_NSEL), 1)).astype(jnp.float32)
    g_t = lax.dot_general(eye, g_ref[...], (((1,), (1,)), ((), ())),
                          precision=lax.Precision.HIGHEST,
                          preferred_element_type=jnp.float32)
    w = g_t * (0.5 * hm * (1.0 + lax.erf(hm * _SQRT_HALF)))
    for tt in range(PEER_TB):
        vrows = buf[slot, tt, :, :, d:2 * d].reshape(PEER_NSEL, d)
        acc = jnp.sum(vrows * w[:, tt:tt + 1], axis=0, keepdims=True)
        o_ref[tt:tt + 1, :] = xres_ref[tt:tt + 1, :] + acc


def peer_gather(xn, idx, gates, xres, uv):
    m, d = xn.shape
    nb = m // PEER_TB
    tok = lambda i: (i, 0)
    idx_flat = idx.reshape(m * PEER_NSEL)
    return pl.pallas_call(
        _peer_gather_kernel,
        grid=(nb,),
        in_specs=[
            pl.BlockSpec((PEER_TB * PEER_NSEL,), lambda i: (i,), memory_space=pltpu.SMEM),
            pl.BlockSpec((PEER_TB * PEER_NSEL,), lambda i: (jnp.minimum(i + 1, nb - 1),),
                         memory_space=pltpu.SMEM),
            pl.BlockSpec((PEER_TB, d), tok),
            pl.BlockSpec((PEER_TB, PEER_NSEL), tok),
            pl.BlockSpec((PEER_TB, d), tok),
            pl.BlockSpec(memory_space=pl.ANY),
        ],
        out_specs=pl.BlockSpec((PEER_TB, d), tok),
        out_shape=jax.ShapeDtypeStruct((m, d), jnp.float32),
        scratch_shapes=[
            pltpu.VMEM((2, PEER_TB, PEER_NSEL // SUBLANES, SUBLANES, 2 * d), jnp.float32),
            pltpu.SemaphoreType.DMA((2,)),
        ],
        compiler_params=pltpu.CompilerParams(
            dimension_semantics=("arbitrary",),
            vmem_limit_bytes=40 * 1024 * 1024,
            disable_bounds_checks=True),
        name="peer_gather",
    )(idx_flat, idx_flat, xn, gates, xres, uv)


ROUTE_TM = 256
ROUTE_GROUP = LANES
N_SUBSETS = PEER_HEADS * 2


def _take_top(vals, payload, n_rows):
    row = lax.broadcasted_iota(jnp.int32, vals.shape, 0)
    m = jnp.max(vals, axis=0, keepdims=True)
    pos = jnp.min(jnp.where(vals == m, row, n_rows), axis=0, keepdims=True)
    hit = row == pos
    picked = jnp.max(jnp.where(hit, payload, -1), axis=0, keepdims=True)
    return m, picked, jnp.where(hit, -jnp.inf, vals)


def _route_kernel(x_ref, y_ref, wo_ref, nw_ref, wq_ref, keys_ref,
                  xo_ref, xn_ref, idx_ref, g_ref,
                  q_scr, sh_scr, ih_scr, idx_scr, g_scr):
    xa = x_ref[...] + jnp.dot(y_ref[...].astype(jnp.bfloat16), wo_ref[...],
                              preferred_element_type=jnp.float32)
    xo_ref[...] = xa
    xn = xa * lax.rsqrt(jnp.mean(xa * xa, axis=-1, keepdims=True) + EPS) * nw_ref[...]
    xn_ref[...] = xn
    q_scr[...] = jnp.dot(xn.astype(jnp.bfloat16), wq_ref[...],
                         preferred_element_type=jnp.float32).astype(jnp.bfloat16)

    key_row = lax.broadcasted_iota(jnp.int32, (N_KEYS, ROUTE_GROUP), 0)

    def group_body(gi, carry):
        t0 = pl.multiple_of(gi * ROUTE_GROUP, ROUTE_GROUP)

        def subset_body(c, carry2):
            c0 = pl.multiple_of(c * PEER_HALF, PEER_HALF)
            qc = q_scr[pl.ds(t0, ROUTE_GROUP), pl.ds(c0, PEER_HALF)]
            sc = lax.dot_general(keys_ref[c], qc, (((1,), (1,)), ((), ())),
                                 preferred_element_type=jnp.float32)
            for r in range(PEER_TOPK):
                m, ki, sc = _take_top(sc, key_row, N_KEYS)
                sh_scr[c, r:r + 1, :] = m
                ih_scr[c, r:r + 1, :] = ki
            return carry2
        lax.fori_loop(0, N_SUBSETS, subset_body, 0)

        def head_body(h, carry2):
            s0 = sh_scr[2 * h]
            s1 = sh_scr[2 * h + 1]
            i0 = ih_scr[2 * h]
            i1 = ih_scr[2 * h + 1]
            kk = PEER_TOPK
            cand = (s0[:, None, :] + s1[None, :, :]).reshape(kk * kk, ROUTE_GROUP)
            cidx = (i0[:, None, :] * N_KEYS + i1[None, :, :]).reshape(kk * kk, ROUTE_GROUP)
            tops, ids = [], []
            for r in range(kk):
                m, e, cand = _take_top(cand, cidx, kk * kk)
                tops.append(m)
                ids.append(e)
            top_s = jnp.concatenate(tops, axis=0)
            ex = jnp.exp(top_s - tops[0])
            gate = ex / jnp.sum(ex, axis=0, keepdims=True)
            r0 = pl.multiple_of(h * kk, kk)
            idx_scr[pl.ds(r0, kk), :] = jnp.concatenate(ids, axis=0)
            g_scr[pl.ds(r0, kk), :] = gate
            return carry2
        lax.fori_loop(0, PEER_HEADS, head_body, 0)

        idx_ref[pl.ds(t0, ROUTE_GROUP), :] = idx_scr[...].T
        g_ref[pl.ds(t0, ROUTE_GROUP), :] = g_scr[...].T
        return carry
    lax.fori_loop(0, ROUTE_TM // ROUTE_GROUP, group_body, 0)


def peer_route(x2d, ycat, w_out_bf16, ffn_norm_w, w_q_bf16, keys_bf16):
    m, d = x2d.shape
    tm = ROUTE_TM
    tok = lambda i: (i, 0)
    full2 = lambda i: (0, 0)
    return pl.pallas_call(
        _route_kernel,
        grid=(m // tm,),
        in_specs=[
            pl.BlockSpec((tm, d), tok),
            pl.BlockSpec((tm, MIX_WIDTH), tok),
            pl.BlockSpec((MIX_WIDTH, d), full2),
            pl.BlockSpec((1, d), full2),
            pl.BlockSpec((d, PEER_HEADS * PEER_KEY_DIM), full2),
            pl.BlockSpec((N_SUBSETS, N_KEYS, PEER_HALF), lambda i: (0, 0, 0)),
        ],
        out_specs=[
            pl.BlockSpec((tm, d), tok),
            pl.BlockSpec((tm, d), tok),
            pl.BlockSpec((tm, PEER_NSEL), tok),
            pl.BlockSpec((tm, PEER_NSEL), tok),
        ],
        out_shape=[
            jax.ShapeDtypeStruct((m, d), jnp.float32),
            jax.ShapeDtypeStruct((m, d), jnp.float32),
            jax.ShapeDtypeStruct((m, PEER_NSEL), jnp.int32),
            jax.ShapeDtypeStruct((m, PEER_NSEL), jnp.float32),
        ],
        scratch_shapes=[
            pltpu.VMEM((tm, PEER_HEADS * PEER_KEY_DIM), jnp.bfloat16),
            pltpu.VMEM((N_SUBSETS, PEER_TOPK, ROUTE_GROUP), jnp.float32),
            pltpu.VMEM((N_SUBSETS, PEER_TOPK, ROUTE_GROUP), jnp.int32),
            pltpu.VMEM((PEER_NSEL, ROUTE_GROUP), jnp.int32),
            pltpu.VMEM((PEER_NSEL, ROUTE_GROUP), jnp.float32),
        ],
        compiler_params=pltpu.CompilerParams(
            dimension_semantics=("arbitrary",),
            vmem_limit_bytes=48 * 1024 * 1024),
        name="peer_route",
    )(x2d, ycat, w_out_bf16, ffn_norm_w.reshape(1, d), w_q_bf16, keys_bf16)


def rmsnorm(x, w):
    xf = x.astype(jnp.float32)
    y = xf * lax.rsqrt(jnp.mean(xf * xf, axis=-1, keepdims=True) + EPS)
    return (y * w.astype(jnp.float32)).astype(x.dtype)


def rotary_tables(seq):
    inv = jnp.power(ROPE_THETA, -jnp.arange(0, ROPE_DIM, 2, dtype=jnp.float32) / ROPE_DIM)
    ang = jnp.arange(seq, dtype=jnp.float32)[:, None] * inv[None, :]
    return jnp.cos(ang), jnp.sin(ang)


def partial_rotary(x, cos, sin):
    half = ROPE_DIM // 2
    c = cos[:, None, None, :]
    s = sin[:, None, None, :]
    xf = x.astype(jnp.float32)
    x1, x2, xp = xf[..., :half], xf[..., half:ROPE_DIM], xf[..., ROPE_DIM:]
    out = jnp.concatenate([x1 * c - x2 * s, x1 * s + x2 * c, xp], axis=-1)
    return out.astype(x.dtype)


def causal_depthwise_conv(u, w, b):
    out = lax.conv_general_dilated(
        u, w[:, None, :].astype(u.dtype), window_strides=(1,),
        padding=[(CONV_WIDTH - 1, 0)],
        dimension_numbers=('NWC', 'WIO', 'NWC'),
        feature_group_count=u.shape[-1])
    return out + b.astype(u.dtype)


def ssd_branch(z, xbc, dt_raw, conv_w, conv_b, dt_bias, a_log, d_skip, norm_w):
    f32 = jnp.float32
    b, s, _ = z.shape
    nc = s // CHUNK
    G, HG = SSD_GROUPS, SSD_HEADS_PER_GROUP
    xbc = jax.nn.silu(causal_depthwise_conv(xbc, conv_w, conv_b))
    xs, bm, cm = jnp.split(xbc, [SSD_WIDTH, SSD_WIDTH + G * SSD_STATE], axis=-1)
    xs = xs.reshape(b, nc, CHUNK, G, HG, SSD_HEAD_DIM).astype(f32)
    bm = bm.reshape(b, nc, CHUNK, G, SSD_STATE).astype(f32)
    cm = cm.reshape(b, nc, CHUNK, G, SSD_STATE).astype(f32)
    dt = jax.nn.softplus(dt_raw.astype(f32) + dt_bias.astype(f32)).reshape(b, nc, CHUNK, G, HG)
    a = -jnp.exp(a_log.astype(f32)).reshape(G, HG)
    a_cs = jnp.cumsum(dt * a, axis=2)
    xdt = xs * dt[..., None]
    seg = a_cs[:, :, :, None] - a_cs[:, :, None, :]
    causal = jnp.tril(jnp.ones((CHUNK, CHUNK), dtype=bool))[:, :, None, None]
    decay = jnp.exp(jnp.where(causal, seg, -jnp.inf))
    cb = jnp.einsum('bclgn,bcsgn->bclsg', cm, bm)
    y_diag = jnp.einsum('bclsg,bclsgh,bcsghp->bclghp', cb, decay, xdt)
    decay_to_end = jnp.exp(a_cs[:, :, -1:] - a_cs)
    states = jnp.einsum('bclgn,bclgh,bclghp->bcghpn', bm, decay_to_end, xdt)
    chunk_decay = jnp.exp(a_cs[:, :, -1])

    def step(carry, inp):
        st, dc = inp
        return carry * dc[..., None, None] + st, carry

    init = jnp.zeros_like(states[:, 0])
    _, prev = lax.scan(step, init, (jnp.moveaxis(states, 1, 0), jnp.moveaxis(chunk_decay, 1, 0)))
    prev = jnp.moveaxis(prev, 0, 1)
    y_off = jnp.einsum('bclgn,bcghpn,bclgh->bclghp', cm, prev, jnp.exp(a_cs))
    y = y_diag + y_off + xs * d_skip.astype(f32).reshape(G, HG)[:, :, None]
    y = y.reshape(b, s, SSD_WIDTH)
    y = (y * jax.nn.silu(z.astype(f32))).reshape(b, s, G, SSD_WIDTH // G)
    y = y * lax.rsqrt(jnp.mean(y * y, axis=-1, keepdims=True) + EPS)
    y = y.reshape(b, s, SSD_WIDTH) * norm_w.astype(f32)
    return y.astype(z.dtype)


def diff_attention(q, k, v, qn_w, kn_w, lq1, lk1, lq2, lk2, subln_w, lambda_init, cos, sin):
    f32 = jnp.float32
    b, s, _ = q.shape
    nb = s // Q_BLOCK
    q = q.reshape(b, s, DIFF_HEADS, 2, DIFF_QKDIM)
    k = k.reshape(b, s, DIFF_HEADS, 2, DIFF_QKDIM)
    vf = v.reshape(b, s, DIFF_HEADS, DIFF_VDIM).astype(f32)
    q = partial_rotary(rmsnorm(q, qn_w), cos, sin).astype(f32)
    kf = partial_rotary(rmsnorm(k, kn_w), cos, sin).astype(f32)
    lam = (jnp.exp(jnp.sum(lq1.astype(f32) * lk1.astype(f32)))
           - jnp.exp(jnp.sum(lq2.astype(f32) * lk2.astype(f32))) + lambda_init)
    qb = jnp.moveaxis(q.reshape(b, nb, Q_BLOCK, DIFF_HEADS, 2, DIFF_QKDIM), 1, 0)
    k_chunk = jnp.arange(s) // CHUNK
    scale = DIFF_QKDIM ** -0.5

    def block(args):
        qi, i = args
        q_chunk = (i * Q_BLOCK + jnp.arange(Q_BLOCK)) // CHUNK
        mask = k_chunk[None, :] <= q_chunk[:, None]
        sc = jnp.einsum('bqhjd,bkhjd->bhjqk', qi, kf) * scale
        p = jax.nn.softmax(jnp.where(mask, sc, -jnp.inf), axis=-1)
        attn = p[:, :, 0] - lam * p[:, :, 1]
        return jnp.einsum('bhqk,bkhd->bqhd', attn, vf)

    o = lax.map(block, (qb, jnp.arange(nb)))
    o = jnp.moveaxis(o, 0, 1).reshape(b, s, DIFF_HEADS, DIFF_VDIM)
    o = rmsnorm(o, subln_w) * (1.0 - lambda_init)
    return o.reshape(b, s, DIFF_WIDTH).astype(v.dtype)


def kernel(x, mix_norm_w, w_in, conv_w, conv_b, dt_bias, a_log, d_skip, ssd_norm_w, q_norm_w, k_norm_w, lambda_q1, lambda_k1, lambda_q2, lambda_k2, subln_w, w_out, ffn_norm_w, peer_w_q, peer_sub_keys, peer_u, peer_v):
    b, s, d = x.shape
    cos, sin = rotary_tables(s)
    layer = 0
    lambda_init = 0.8 - 0.6 * math.exp(-0.3 * layer)
    proj = norm_proj(x.reshape(b * s, d), mix_norm_w[layer],
                     w_in[layer].astype(jnp.bfloat16)).reshape(b, s, IN_COLS)
    z, xbc, dt_raw, q, k, v = jnp.split(proj, SPLITS, axis=-1)
    y_ssd = ssd_branch(z, xbc, dt_raw, conv_w[layer], conv_b[layer], dt_bias[layer],
                       a_log[layer], d_skip[layer], ssd_norm_w[layer])
    y_diff = diff_attention(q, k, v, q_norm_w[layer], k_norm_w[layer], lambda_q1[layer],
                            lambda_k1[layer], lambda_q2[layer], lambda_k2[layer],
                            subln_w[layer], lambda_init, cos, sin)
    ycat = jnp.concatenate([y_ssd, y_diff], axis=-1).reshape(b * s, MIX_WIDTH)
    keys = peer_sub_keys[layer].reshape(N_SUBSETS, N_KEYS, PEER_HALF).astype(jnp.bfloat16)
    x2, xn, idx, gates = peer_route(x.reshape(b * s, d), ycat, w_out[layer].astype(jnp.bfloat16),
                                    ffn_norm_w[layer], peer_w_q[layer].astype(jnp.bfloat16), keys)
    uv = jnp.concatenate([peer_u[layer], peer_v[layer]], axis=-1)
    out = peer_gather(xn, idx, gates, x2, uv)
    return out.reshape(b, s, d)
```

```python
import functools
import math

import jax
import jax.numpy as jnp
from jax import lax
from jax.experimental import pallas as pl
from jax.experimental.pallas import tpu as pltpu

D_MODEL = 1024
CHUNK = 64
EPS = 1e-6

SSD_WIDTH = D_MODEL // 2
SSD_HEAD_DIM = 64
SSD_HEADS = SSD_WIDTH // SSD_HEAD_DIM
SSD_GROUPS = 2
SSD_HEADS_PER_GROUP = SSD_HEADS // SSD_GROUPS
SSD_STATE = 128
CONV_WIDTH = 4
SSD_CONV_CH = SSD_WIDTH + 2 * SSD_GROUPS * SSD_STATE

DIFF_WIDTH = D_MODEL - SSD_WIDTH
DIFF_HEADS = 4
DIFF_VDIM = DIFF_WIDTH // DIFF_HEADS
DIFF_QKDIM = DIFF_VDIM // 2
ROPE_DIM = DIFF_QKDIM // 4
ROPE_THETA = 500000.0
Q_BLOCK = 128

QK_COLS = DIFF_HEADS * 2 * DIFF_QKDIM
SPLITS = (SSD_WIDTH,
          SSD_WIDTH + SSD_CONV_CH,
          SSD_WIDTH + SSD_CONV_CH + SSD_HEADS,
          SSD_WIDTH + SSD_CONV_CH + SSD_HEADS + QK_COLS,
          SSD_WIDTH + SSD_CONV_CH + SSD_HEADS + 2 * QK_COLS)
IN_COLS = SPLITS[-1] + DIFF_WIDTH
MIX_WIDTH = SSD_WIDTH + DIFF_WIDTH

PEER_HEADS = 8
N_KEYS = 128
N_EXPERTS = N_KEYS * N_KEYS
PEER_KEY_DIM = 256
PEER_HALF = PEER_KEY_DIM // 2
PEER_TOPK = 16
PEER_TOKEN_BLOCK = 128


def _norm_proj_kernel(x_ref, nw_ref, w_ref, o_ref):
    xf = x_ref[...]
    y = xf * lax.rsqrt(jnp.mean(xf * xf, axis=-1, keepdims=True) + EPS)
    y = (y * nw_ref[...]).astype(jnp.bfloat16)
    o_ref[...] = jnp.dot(y, w_ref[...], preferred_element_type=jnp.float32)


def norm_proj(x2d, norm_w, w_bf16, tm=512):
    m, d = x2d.shape
    n = w_bf16.shape[1]
    return pl.pallas_call(
        _norm_proj_kernel,
        grid=(m // tm,),
        in_specs=[
            pl.BlockSpec((tm, d), lambda i: (i, 0)),
            pl.BlockSpec((1, d), lambda i: (0, 0)),
            pl.BlockSpec((d, n), lambda i: (0, 0)),
        ],
        out_specs=pl.BlockSpec((tm, n), lambda i: (i, 0)),
        out_shape=jax.ShapeDtypeStruct((m, n), jnp.float32),
        compiler_params=pltpu.CompilerParams(
            dimension_semantics=("arbitrary",),
            vmem_limit_bytes=56 * 1024 * 1024),
        name="norm_proj",
    )(x2d, norm_w.reshape(1, d), w_bf16)


PEER_NSEL = PEER_HEADS * PEER_TOPK
PEER_TB = 8
LANES = 128
SUBLANES = 8
_SQRT_HALF = 0.7071067811865476


def _peer_gather_kernel(idx_cur, idx_nxt, xn_ref, g_ref, xres_ref, uv_hbm,
                        o_ref, buf, sem):
    i = pl.program_id(0)
    nb = pl.num_programs(0)
    slot = i % 2

    def issue(idx_ref, s):
        for tt in range(PEER_TB):
            def body(kk, carry, tt=tt):
                k0 = kk * SUBLANES
                for j in range(SUBLANES):
                    e = idx_ref[tt * PEER_NSEL + k0 + j]
                    pltpu.make_async_copy(uv_hbm.at[pl.ds(e, 1)],
                                          buf.at[s, tt, kk, pl.ds(j, 1)],
                                          sem.at[s]).start()
                return carry
            lax.fori_loop(0, PEER_NSEL // SUBLANES, body, 0)

    @pl.when(i == 0)
    def _():
        issue(idx_cur, 0)

    @pl.when(i + 1 < nb)
    def _():
        issue(idx_nxt, 1 - slot)

    pltpu.make_async_copy(buf.at[slot], buf.at[slot], sem.at[slot]).wait()

    d = xn_ref.shape[-1]
    lane = lax.broadcasted_iota(jnp.int32, (PEER_NSEL, PEER_TB), 1)
    hm = jnp.zeros((PEER_NSEL, PEER_TB), jnp.float32)
    for tt in range(PEER_TB):
        urows = buf[slot, tt, :, :, 0:d].reshape(PEER_NSEL, d)
        prod = urows * xn_ref[tt:tt + 1, :]
        part = prod[:, 0:LANES]
        for c in range(1, d // LANES):
            part = part + prod[:, c * LANES:(c + 1) * LANES]
        h = jnp.sum(part, axis=1, keepdims=True)
        hm = jnp.where(lane == tt, h, hm)
    eye = (lax.broadcasted_iota(jnp.int32, (PEER_NSEL, PEER_NSEL), 0)
           == lax.broadcasted_iota(jnp.int32, (PEER_NSEL, PEER_NSEL), 1)).astype(jnp.float32)
    g_t = lax.dot_general(eye, g_ref[...], (((1,), (1,)), ((), ())),
                          precision=lax.Precision.HIGHEST,
                          preferred_element_type=jnp.float32)
    w = g_t * (0.5 * hm * (1.0 + lax.erf(hm * _SQRT_HALF)))
    for tt in range(PEER_TB):
        vrows = buf[slot, tt, :, :, d:2 * d].reshape(PEER_NSEL, d)
        acc = jnp.sum(vrows * w[:, tt:tt + 1], axis=0, keepdims=True)
        o_ref[tt:tt + 1, :] = xres_ref[tt:tt + 1, :] + acc


def peer_gather(xn, idx, gates, xres, uv):
    m, d = xn.shape
    nb = m // PEER_TB
    tok = lambda i: (i, 0)
    idx_flat = idx.reshape(m * PEER_NSEL)
    return pl.pallas_call(
        _peer_gather_kernel,
        grid=(nb,),
        in_specs=[
            pl.BlockSpec((PEER_TB * PEER_NSEL,), lambda i: (i,), memory_space=pltpu.SMEM),
            pl.BlockSpec((PEER_TB * PEER_NSEL,), lambda i: (jnp.minimum(i + 1, nb - 1),),
                         memory_space=pltpu.SMEM),
            pl.BlockSpec((PEER_TB, d), tok),
            pl.BlockSpec((PEER_TB, PEER_NSEL), tok),
            pl.BlockSpec((PEER_TB, d), tok),
            pl.BlockSpec(memory_space=pl.ANY),
        ],
        out_specs=pl.BlockSpec((PEER_TB, d), tok),
        out_shape=jax.ShapeDtypeStruct((m, d), jnp.float32),
        scratch_shapes=[
            pltpu.VMEM((2, PEER_TB, PEER_NSEL // SUBLANES, SUBLANES, 2 * d), jnp.float32),
            pltpu.SemaphoreType.DMA((2,)),
        ],
        compiler_params=pltpu.CompilerParams(
            dimension_semantics=("arbitrary",),
            vmem_limit_bytes=40 * 1024 * 1024,
            disable_bounds_checks=True),
        name="peer_gather",
    )(idx_flat, idx_flat, xn, gates, xres, uv)


ROUTE_TM = 256
ROUTE_GROUP = LANES
N_SUBSETS = PEER_HEADS * 2


def _take_top(vals, payload, n_rows):
    row = lax.broadcasted_iota(jnp.int32, vals.shape, 0)
    m = jnp.max(vals, axis=0, keepdims=True)
    pos = jnp.min(jnp.where(vals == m, row, n_rows), axis=0, keepdims=True)
    hit = row == pos
    picked = jnp.max(jnp.where(hit, payload, -1), axis=0, keepdims=True)
    return m, picked, jnp.where(hit, -jnp.inf, vals)


def _route_kernel(x_ref, y_ref, wo_ref, nw_ref, wq_ref, keys_ref,
                  xo_ref, xn_ref, idx_ref, g_ref,
                  q_scr, sh_scr, ih_scr, idx_scr, g_scr):
    xa = x_ref[...] + jnp.dot(y_ref[...].astype(jnp.bfloat16), wo_ref[...],
                              preferred_element_type=jnp.float32)
    xo_ref[...] = xa
    xn = xa * lax.rsqrt(jnp.mean(xa * xa, axis=-1, keepdims=True) + EPS) * nw_ref[...]
    xn_ref[...] = xn
    q_scr[...] = jnp.dot(xn.astype(jnp.bfloat16), wq_ref[...],
                         preferred_element_type=jnp.float32).astype(jnp.bfloat16)

    key_row = lax.broadcasted_iota(jnp.int32, (N_KEYS, ROUTE_GROUP), 0)

    def group_body(gi, carry):
        t0 = pl.multiple_of(gi * ROUTE_GROUP, ROUTE_GROUP)

        def subset_body(c, carry2):
            c0 = pl.multiple_of(c * PEER_HALF, PEER_HALF)
            qc = q_scr[pl.ds(t0, ROUTE_GROUP), pl.ds(c0, PEER_HALF)]
            sc = lax.dot_general(keys_ref[c], qc, (((1,), (1,)), ((), ())),
                                 preferred_element_type=jnp.float32)
            for r in range(PEER_TOPK):
                m, ki, sc = _take_top(sc, key_row, N_KEYS)
                sh_scr[c, r:r + 1, :] = m
                ih_scr[c, r:r + 1, :] = ki
            return carry2
        lax.fori_loop(0, N_SUBSETS, subset_body, 0)

        def head_body(h, carry2):
            s0 = sh_scr[2 * h]
            s1 = sh_scr[2 * h + 1]
            i0 = ih_scr[2 * h]
            i1 = ih_scr[2 * h + 1]
            kk = PEER_TOPK
            cand = (s0[:, None, :] + s1[None, :, :]).reshape(kk * kk, ROUTE_GROUP)
            cidx = (i0[:, None, :] * N_KEYS + i1[None, :, :]).reshape(kk * kk, ROUTE_GROUP)
            tops, ids = [], []
            for r in range(kk):
                m, e, cand = _take_top(cand, cidx, kk * kk)
                tops.append(m)
                ids.append(e)
            top_s = jnp.concatenate(tops, axis=0)
            ex = jnp.exp(top_s - tops[0])
            gate = ex / jnp.sum(ex, axis=0, keepdims=True)
            r0 = pl.multiple_of(h * kk, kk)
            idx_scr[pl.ds(r0, kk), :] = jnp.concatenate(ids, axis=0)
            g_scr[pl.ds(r0, kk), :] = gate
            return carry2
        lax.fori_loop(0, PEER_HEADS, head_body, 0)

        idx_ref[pl.ds(t0, ROUTE_GROUP), :] = idx_scr[...].T
        g_ref[pl.ds(t0, ROUTE_GROUP), :] = g_scr[...].T
        return carry
    lax.fori_loop(0, ROUTE_TM // ROUTE_GROUP, group_body, 0)


def peer_route(x2d, ycat, w_out_bf16, ffn_norm_w, w_q_bf16, keys_bf16):
    m, d = x2d.shape
    tm = ROUTE_TM
    tok = lambda i: (i, 0)
    full2 = lambda i: (0, 0)
    return pl.pallas_call(
        _route_kernel,
        grid=(m // tm,),
        in_specs=[
            pl.BlockSpec((tm, d), tok),
            pl.BlockSpec((tm, MIX_WIDTH), tok),
            pl.BlockSpec((MIX_WIDTH, d), full2),
            pl.BlockSpec((1, d), full2),
            pl.BlockSpec((d, PEER_HEADS * PEER_KEY_DIM), full2),
            pl.BlockSpec((N_SUBSETS, N_KEYS, PEER_HALF), lambda i: (0, 0, 0)),
        ],
        out_specs=[
            pl.BlockSpec((tm, d), tok),
            pl.BlockSpec((tm, d), tok),
            pl.BlockSpec((tm, PEER_NSEL), tok),
            pl.BlockSpec((tm, PEER_NSEL), tok),
        ],
        out_shape=[
            jax.ShapeDtypeStruct((m, d), jnp.float32),
            jax.ShapeDtypeStruct((m, d), jnp.float32),
            jax.ShapeDtypeStruct((m, PEER_NSEL), jnp.int32),
            jax.ShapeDtypeStruct((m, PEER_NSEL), jnp.float32),
        ],
        scratch_shapes=[
            pltpu.VMEM((tm, PEER_HEADS * PEER_KEY_DIM), jnp.bfloat16),
            pltpu.VMEM((N_SUBSETS, PEER_TOPK, ROUTE_GROUP), jnp.float32),
            pltpu.VMEM((N_SUBSETS, PEER_TOPK, ROUTE_GROUP), jnp.int32),
            pltpu.VMEM((PEER_NSEL, ROUTE_GROUP), jnp.int32),
            pltpu.VMEM((PEER_NSEL, ROUTE_GROUP), jnp.float32),
        ],
        compiler_params=pltpu.CompilerParams(
            dimension_semantics=("arbitrary",),
            vmem_limit_bytes=48 * 1024 * 1024),
        name="peer_route",
    )(x2d, ycat, w_out_bf16, ffn_norm_w.reshape(1, d), w_q_bf16, keys_bf16)


ATT_T = 256


def _diff_attn_kernel(lam_ref, q0_ref, q1_ref, k_ref, v_ref, sw_ref, o_ref,
                      m_scr, l_scr, acc_scr, *, out_scale):
    qb = pl.program_id(2)
    m_scr[...] = jnp.full(m_scr.shape, -jnp.inf, jnp.float32)
    l_scr[...] = jnp.zeros(l_scr.shape, jnp.float32)
    acc_scr[...] = jnp.zeros(acc_scr.shape, jnp.float32)
    row_chunk = lax.broadcasted_iota(jnp.int32, (ATT_T, ATT_T), 0) // CHUNK
    col_chunk = lax.broadcasted_iota(jnp.int32, (ATT_T, ATT_T), 1) // CHUNK
    diag_mask = col_chunk <= row_chunk

    def update(kb, masked):
        k0 = pl.multiple_of(kb * ATT_T, ATT_T)
        ks = k_ref[0, pl.ds(k0, ATT_T), :]
        vs = v_ref[0, pl.ds(k0, ATT_T), :]
        for j, q_ref in enumerate((q0_ref, q1_ref)):
            s = lax.dot_general(q_ref[0], ks, (((1,), (1,)), ((), ())),
                                preferred_element_type=jnp.float32)
            if masked:
                s = jnp.where(diag_mask, s, -jnp.inf)
            m_old = m_scr[j]
            m_new = jnp.maximum(m_old, jnp.max(s, axis=1, keepdims=True))
            alpha = jnp.exp(m_old - m_new)
            p = jnp.exp(s - m_new)
            l_scr[j] = alpha * l_scr[j] + jnp.sum(p, axis=1, keepdims=True)
            acc_scr[j] = alpha * acc_scr[j] + jnp.dot(p.astype(jnp.bfloat16), vs,
                                                      preferred_element_type=jnp.float32)
            m_scr[j] = m_new

    def body(kb, carry):
        update(kb, False)
        return carry
    lax.fori_loop(0, qb, body, 0)
    update(qb, True)

    o = acc_scr[0] / l_scr[0] - lam_ref[0] * (acc_scr[1] / l_scr[1])
    o = o * lax.rsqrt(jnp.mean(o * o, axis=-1, keepdims=True) + EPS) * sw_ref[...]
    o_ref[0] = o * out_scale


def diff_attention_core(q0, q1, k, v, lam, subln_w, out_scale):
    b, s, _ = q0.shape
    qspec = pl.BlockSpec((1, ATT_T, 2 * DIFF_QKDIM), lambda bi, h, i: (bi, i, h))
    kspec = pl.BlockSpec((1, s, 2 * DIFF_QKDIM), lambda bi, h, i: (bi, 0, h))
    vspec = pl.BlockSpec((1, s, DIFF_VDIM), lambda bi, h, i: (bi, 0, h))
    return pl.pallas_call(
        functools.partial(_diff_attn_kernel, out_scale=out_scale),
        grid=(b, DIFF_HEADS, s // ATT_T),
        in_specs=[
            pl.BlockSpec(memory_space=pltpu.SMEM),
            qspec, qspec, kspec, vspec,
            pl.BlockSpec((1, DIFF_VDIM), lambda bi, h, i: (0, 0)),
        ],
        out_specs=pl.BlockSpec((1, ATT_T, DIFF_VDIM), lambda bi, h, i: (bi, i, h)),
        out_shape=jax.ShapeDtypeStruct((b, s, DIFF_WIDTH), jnp.float32),
        scratch_shapes=[
            pltpu.VMEM((2, ATT_T, 1), jnp.float32),
            pltpu.VMEM((2, ATT_T, 1), jnp.float32),
            pltpu.VMEM((2, ATT_T, DIFF_VDIM), jnp.float32),
        ],
        compiler_params=pltpu.CompilerParams(
            dimension_semantics=("arbitrary", "arbitrary", "arbitrary"),
            vmem_limit_bytes=32 * 1024 * 1024),
        name="diff_attention",
    )(lam, q0, q1, k, v, subln_w.reshape(1, DIFF_VDIM))


def rmsnorm(x, w):
    xf = x.astype(jnp.float32)
    y = xf * lax.rsqrt(jnp.mean(xf * xf, axis=-1, keepdims=True) + EPS)
    return (y * w.astype(jnp.float32)).astype(x.dtype)


def rotary_tables(seq):
    inv = jnp.power(ROPE_THETA, -jnp.arange(0, ROPE_DIM, 2, dtype=jnp.float32) / ROPE_DIM)
    ang = jnp.arange(seq, dtype=jnp.float32)[:, None] * inv[None, :]
    return jnp.cos(ang), jnp.sin(ang)


def partial_rotary(x, cos, sin):
    half = ROPE_DIM // 2
    c = cos[:, None, None, :]
    s = sin[:, None, None, :]
    xf = x.astype(jnp.float32)
    x1, x2, xp = xf[..., :half], xf[..., half:ROPE_DIM], xf[..., ROPE_DIM:]
    out = jnp.concatenate([x1 * c - x2 * s, x1 * s + x2 * c, xp], axis=-1)
    return out.astype(x.dtype)


def causal_depthwise_conv(u, w, b):
    out = lax.conv_general_dilated(
        u, w[:, None, :].astype(u.dtype), window_strides=(1,),
        padding=[(CONV_WIDTH - 1, 0)],
        dimension_numbers=('NWC', 'WIO', 'NWC'),
        feature_group_count=u.shape[-1])
    return out + b.astype(u.dtype)


def ssd_branch(z, xbc, dt_raw, conv_w, conv_b, dt_bias, a_log, d_skip, norm_w):
    f32 = jnp.float32
    b, s, _ = z.shape
    nc = s // CHUNK
    G, HG = SSD_GROUPS, SSD_HEADS_PER_GROUP
    xbc = jax.nn.silu(causal_depthwise_conv(xbc, conv_w, conv_b))
    xs, bm, cm = jnp.split(xbc, [SSD_WIDTH, SSD_WIDTH + G * SSD_STATE], axis=-1)
    xs = xs.reshape(b, nc, CHUNK, G, HG, SSD_HEAD_DIM).astype(f32)
    bm = bm.reshape(b, nc, CHUNK, G, SSD_STATE).astype(f32)
    cm = cm.reshape(b, nc, CHUNK, G, SSD_STATE).astype(f32)
    dt = jax.nn.softplus(dt_raw.astype(f32) + dt_bias.astype(f32)).reshape(b, nc, CHUNK, G, HG)
    a = -jnp.exp(a_log.astype(f32)).reshape(G, HG)
    a_cs = jnp.cumsum(dt * a, axis=2)
    xdt = xs * dt[..., None]
    seg = a_cs[:, :, :, None] - a_cs[:, :, None, :]
    causal = jnp.tril(jnp.ones((CHUNK, CHUNK), dtype=bool))[:, :, None, None]
    decay = jnp.exp(jnp.where(causal, seg, -jnp.inf))
    cb = jnp.einsum('bclgn,bcsgn->bclsg', cm, bm)
    y_diag = jnp.einsum('bclsg,bclsgh,bcsghp->bclghp', cb, decay, xdt)
    decay_to_end = jnp.exp(a_cs[:, :, -1:] - a_cs)
    states = jnp.einsum('bclgn,bclgh,bclghp->bcghpn', bm, decay_to_end, xdt)
    chunk_decay = jnp.exp(a_cs[:, :, -1])

    def step(carry, inp):
        st, dc = inp
        return carry * dc[..., None, None] + st, carry

    init = jnp.zeros_like(states[:, 0])
    _, prev = lax.scan(step, init, (jnp.moveaxis(states, 1, 0), jnp.moveaxis(chunk_decay, 1, 0)))
    prev = jnp.moveaxis(prev, 0, 1)
    y_off = jnp.einsum('bclgn,bcghpn,bclgh->bclghp', cm, prev, jnp.exp(a_cs))
    y = y_diag + y_off + xs * d_skip.astype(f32).reshape(G, HG)[:, :, None]
    y = y.reshape(b, s, SSD_WIDTH)
    y = (y * jax.nn.silu(z.astype(f32))).reshape(b, s, G, SSD_WIDTH // G)
    y = y * lax.rsqrt(jnp.mean(y * y, axis=-1, keepdims=True) + EPS)
    y = y.reshape(b, s, SSD_WIDTH) * norm_w.astype(f32)
    return y.astype(z.dtype)


def diff_attention(q, k, v, qn_w, kn_w, lq1, lk1, lq2, lk2, subln_w, lambda_init, cos, sin):
    f32 = jnp.float32
    b, s, _ = q.shape
    q = q.reshape(b, s, DIFF_HEADS, 2, DIFF_QKDIM)
    k = k.reshape(b, s, DIFF_HEADS, 2, DIFF_QKDIM)
    q = partial_rotary(rmsnorm(q, qn_w), cos, sin).astype(f32)
    kf = partial_rotary(rmsnorm(k, kn_w), cos, sin).astype(f32)
    lam = (jnp.exp(jnp.sum(lq1.astype(f32) * lk1.astype(f32)))
           - jnp.exp(jnp.sum(lq2.astype(f32) * lk2.astype(f32))) + lambda_init)
    scale = DIFF_QKDIM ** -0.5
    branch = jnp.arange(2)
    q0 = (q * scale * (branch == 0)[:, None]).reshape(b, s, QK_COLS).astype(jnp.bfloat16)
    q1 = (q * scale * (branch == 1)[:, None]).reshape(b, s, QK_COLS).astype(jnp.bfloat16)
    kb = kf.reshape(b, s, QK_COLS).astype(jnp.bfloat16)
    vb = v.astype(jnp.bfloat16)
    return diff_attention_core(q0, q1, kb, vb, lam.reshape(1), subln_w, 1.0 - lambda_init)


def kernel(x, mix_norm_w, w_in, conv_w, conv_b, dt_bias, a_log, d_skip, ssd_norm_w, q_norm_w, k_norm_w, lambda_q1, lambda_k1, lambda_q2, lambda_k2, subln_w, w_out, ffn_norm_w, peer_w_q, peer_sub_keys, peer_u, peer_v):
    b, s, d = x.shape
    cos, sin = rotary_tables(s)
    layer = 0
    lambda_init = 0.8 - 0.6 * math.exp(-0.3 * layer)
    proj = norm_proj(x.reshape(b * s, d), mix_norm_w[layer],
                     w_in[layer].astype(jnp.bfloat16)).reshape(b, s, IN_COLS)
    z, xbc, dt_raw, q, k, v = jnp.split(proj, SPLITS, axis=-1)
    y_ssd = ssd_branch(z, xbc, dt_raw, conv_w[layer], conv_b[layer], dt_bias[layer],
                       a_log[layer], d_skip[layer], ssd_norm_w[layer])
    y_diff = diff_attention(q, k, v, q_norm_w[layer], k_norm_w[layer], lambda_q1[layer],
                            lambda_k1[layer], lambda_q2[layer], lambda_k2[layer],
                            subln_w[layer], lambda_init, cos, sin)
    ycat = jnp.concatenate([y_ssd, y_diff], axis=-1).reshape(b * s, MIX_WIDTH)
    keys = peer_sub_keys[layer].reshape(N_SUBSETS, N_KEYS, PEER_HALF).astype(jnp.bfloat16)
    x2, xn, idx, gates = peer_route(x.reshape(b * s, d), ycat, w_out[layer].astype(jnp.bfloat16),
                                    ffn_norm_w[layer], peer_w_q[layer].astype(jnp.bfloat16), keys)
    uv = jnp.concatenate([peer_u[layer], peer_v[layer]], axis=-1)
    out = peer_gather(xn, idx, gates, x2, uv)
    return out.reshape(b, s, d)
```

```python
import functools
import math

import jax
import jax.numpy as jnp
from jax import lax
from jax.experimental import pallas as pl
from jax.experimental.pallas import tpu as pltpu

D_MODEL = 1024
CHUNK = 64
EPS = 1e-6

SSD_WIDTH = D_MODEL // 2
SSD_HEAD_DIM = 64
SSD_HEADS = SSD_WIDTH // SSD_HEAD_DIM
SSD_GROUPS = 2
SSD_HEADS_PER_GROUP = SSD_HEADS // SSD_GROUPS
SSD_STATE = 128
CONV_WIDTH = 4
SSD_CONV_CH = SSD_WIDTH + 2 * SSD_GROUPS * SSD_STATE

DIFF_WIDTH = D_MODEL - SSD_WIDTH
DIFF_HEADS = 4
DIFF_VDIM = DIFF_WIDTH // DIFF_HEADS
DIFF_QKDIM = DIFF_VDIM // 2
ROPE_DIM = DIFF_QKDIM // 4
ROPE_THETA = 500000.0
Q_BLOCK = 128

QK_COLS = DIFF_HEADS * 2 * DIFF_QKDIM
SPLITS = (SSD_WIDTH,
          SSD_WIDTH + SSD_CONV_CH,
          SSD_WIDTH + SSD_CONV_CH + SSD_HEADS,
          SSD_WIDTH + SSD_CONV_CH + SSD_HEADS + QK_COLS,
          SSD_WIDTH + SSD_CONV_CH + SSD_HEADS + 2 * QK_COLS)
IN_COLS = SPLITS[-1] + DIFF_WIDTH
MIX_WIDTH = SSD_WIDTH + DIFF_WIDTH

PEER_HEADS = 8
N_KEYS = 128
N_EXPERTS = N_KEYS * N_KEYS
PEER_KEY_DIM = 256
PEER_HALF = PEER_KEY_DIM // 2
PEER_TOPK = 16
PEER_TOKEN_BLOCK = 128


def _norm_proj_kernel(x_ref, nw_ref, w_ref, o_ref):
    xf = x_ref[...]
    y = xf * lax.rsqrt(jnp.mean(xf * xf, axis=-1, keepdims=True) + EPS)
    y = (y * nw_ref[...]).astype(jnp.bfloat16)
    o_ref[...] = jnp.dot(y, w_ref[...], preferred_element_type=jnp.float32)


def norm_proj(x2d, norm_w, w_bf16, tm=512):
    m, d = x2d.shape
    n = w_bf16.shape[1]
    return pl.pallas_call(
        _norm_proj_kernel,
        grid=(m // tm,),
        in_specs=[
            pl.BlockSpec((tm, d), lambda i: (i, 0)),
            pl.BlockSpec((1, d), lambda i: (0, 0)),
            pl.BlockSpec((d, n), lambda i: (0, 0)),
        ],
        out_specs=pl.BlockSpec((tm, n), lambda i: (i, 0)),
        out_shape=jax.ShapeDtypeStruct((m, n), jnp.float32),
        compiler_params=pltpu.CompilerParams(
            dimension_semantics=("arbitrary",),
            vmem_limit_bytes=56 * 1024 * 1024),
        name="norm_proj",
    )(x2d, norm_w.reshape(1, d), w_bf16)


PEER_NSEL = PEER_HEADS * PEER_TOPK
PEER_TB = 8
LANES = 128
SUBLANES = 8
_SQRT_HALF = 0.7071067811865476


def _peer_gather_kernel(idx_cur, idx_nxt, xn_ref, g_ref, xres_ref, uv_hbm,
                        o_ref, buf, part_scr, wb_scr, sem):
    i = pl.program_id(0)
    nb = pl.num_programs(0)
    slot = i % 2
    nxt = 1 - slot
    d = xn_ref.shape[-1]
    n_groups = PEER_NSEL // SUBLANES
    n_chunks = d // LANES
    half = PEER_TB // 2

    def start_row(idx_ref, s, tt, kk, j):
        e = idx_ref[tt * PEER_NSEL + kk * SUBLANES + j]
        pltpu.make_async_copy(uv_hbm.at[pl.ds(e, 1)], buf.at[s, tt, kk, pl.ds(j, 1)],
                              sem.at[s]).start()

    @pl.when(i == 0)
    def _():
        for tt in range(PEER_TB):
            def body(kk, carry, tt=tt):
                for j in range(SUBLANES):
                    start_row(idx_cur, 0, tt, kk, j)
                return carry
            lax.fori_loop(0, n_groups, body, 0)

    pltpu.make_async_copy(buf.at[slot], buf.at[slot], sem.at[slot]).wait()

    for tt in range(half):
        def body(kk, carry, tt=tt):
            r0 = pl.multiple_of(kk * SUBLANES, SUBLANES)
            for n, tok in enumerate((2 * tt, 2 * tt + 1)):
                prod = buf[slot, tok, kk, :, 0:d] * xn_ref[tok:tok + 1, :]
                part = prod[:, 0:LANES]
                for c in range(1, n_chunks):
                    part = part + prod[:, c * LANES:(c + 1) * LANES]
                part_scr[tok, pl.ds(r0, SUBLANES), :] = part
                for j in range(n * half, (n + 1) * half):
                    start_row(idx_nxt, nxt, tt, kk, j)
            return carry
        lax.fori_loop(0, n_groups, body, 0)

    lane = lax.broadcasted_iota(jnp.int32, (PEER_NSEL, PEER_TB), 1)
    hm = jnp.zeros((PEER_NSEL, PEER_TB), jnp.float32)
    for tok in range(PEER_TB):
        h = jnp.sum(part_scr[tok], axis=1, keepdims=True)
        hm = jnp.where(lane == tok, h, hm)
    eye = (lax.broadcasted_iota(jnp.int32, (PEER_NSEL, PEER_NSEL), 0)
           == lax.broadcasted_iota(jnp.int32, (PEER_NSEL, PEER_NSEL), 1)).astype(jnp.float32)
    g_t = lax.dot_general(eye, g_ref[...], (((1,), (1,)), ((), ())),
                          precision=lax.Precision.HIGHEST,
                          preferred_element_type=jnp.float32)
    w = g_t * (0.5 * hm * (1.0 + lax.erf(hm * _SQRT_HALF)))
    for tok in range(PEER_TB):
        wb_scr[tok] = jnp.broadcast_to(w[:, tok:tok + 1], (PEER_NSEL, LANES))

    for tt in range(half, PEER_TB):
        toks = (2 * (tt - half), 2 * (tt - half) + 1)

        def body(kk, accs, tt=tt, toks=toks):
            r0 = pl.multiple_of(kk * SUBLANES, SUBLANES)
            out = []
            for n, tok in enumerate(toks):
                wg = wb_scr[tok, pl.ds(r0, SUBLANES), :]
                rows = buf[slot, tok, kk, :, d:2 * d]
                out.append(accs[n] + rows * jnp.concatenate([wg] * n_chunks, axis=1))
                for j in range(n * half, (n + 1) * half):
                    start_row(idx_nxt, nxt, tt, kk, j)
            return tuple(out)
        zero = jnp.zeros((SUBLANES, d), jnp.float32)
        accs = lax.fori_loop(0, n_groups, body, (zero, zero))
        for n, tok in enumerate(toks):
            o_ref[tok:tok + 1, :] = (xres_ref[tok:tok + 1, :]
                                     + jnp.sum(accs[n], axis=0, keepdims=True))

    @pl.when(i == nb - 1)
    def _():
        pltpu.make_async_copy(buf.at[nxt], buf.at[nxt], sem.at[nxt]).wait()


def peer_gather(xn, idx, gates, xres, uv):
    m, d = xn.shape
    nb = m // PEER_TB
    tok = lambda i: (i, 0)
    idx_flat = idx.reshape(m * PEER_NSEL)
    return pl.pallas_call(
        _peer_gather_kernel,
        grid=(nb,),
        in_specs=[
            pl.BlockSpec((PEER_TB * PEER_NSEL,), lambda i: (i,), memory_space=pltpu.SMEM),
            pl.BlockSpec((PEER_TB * PEER_NSEL,), lambda i: (jnp.minimum(i + 1, nb - 1),),
                         memory_space=pltpu.SMEM),
            pl.BlockSpec((PEER_TB, d), tok),
            pl.BlockSpec((PEER_TB, PEER_NSEL), tok),
            pl.BlockSpec((PEER_TB, d), tok),
            pl.BlockSpec(memory_space=pl.ANY),
        ],
        out_specs=pl.BlockSpec((PEER_TB, d), tok),
        out_shape=jax.ShapeDtypeStruct((m, d), jnp.float32),
        scratch_shapes=[
            pltpu.VMEM((2, PEER_TB, PEER_NSEL // SUBLANES, SUBLANES, 2 * d), jnp.float32),
            pltpu.VMEM((PEER_TB, PEER_NSEL, LANES), jnp.float32),
            pltpu.VMEM((PEER_TB, PEER_NSEL, LANES), jnp.float32),
            pltpu.SemaphoreType.DMA((2,)),
        ],
        compiler_params=pltpu.CompilerParams(
            dimension_semantics=("arbitrary",),
            vmem_limit_bytes=40 * 1024 * 1024,
            disable_bounds_checks=True),
        name="peer_gather",
    )(idx_flat, idx_flat, xn, gates, xres, uv)


ROUTE_TM = 256
ROUTE_GROUP = LANES
N_SUBSETS = PEER_HEADS * 2


_POS_SENTINEL = 1 << 20


def _take_top(vals, pos, payload=None):
    m = jnp.max(vals, axis=0, keepdims=True)
    p = jnp.min(jnp.where(vals == m, pos, _POS_SENTINEL), axis=0, keepdims=True)
    hit = pos == p
    picked = p if payload is None else jnp.max(jnp.where(hit, payload, -1), axis=0, keepdims=True)
    return m, picked, jnp.where(hit, -jnp.inf, vals)


def _route_kernel(x_ref, y_ref, wo_ref, nw_ref, wq_ref, keys_ref,
                  xo_ref, xn_ref, idx_ref, g_ref,
                  q_scr, sh_scr, ih_scr, idx_scr, g_scr):
    xa = x_ref[...] + jnp.dot(y_ref[...].astype(jnp.bfloat16), wo_ref[...],
                              preferred_element_type=jnp.float32)
    xo_ref[...] = xa
    xn = xa * lax.rsqrt(jnp.mean(xa * xa, axis=-1, keepdims=True) + EPS) * nw_ref[...]
    xn_ref[...] = xn
    q_scr[...] = jnp.dot(xn.astype(jnp.bfloat16), wq_ref[...],
                         preferred_element_type=jnp.float32).astype(jnp.bfloat16)

    key_row = lax.broadcasted_iota(jnp.int32, (N_KEYS, ROUTE_GROUP), 0)

    def group_body(gi, carry):
        t0 = pl.multiple_of(gi * ROUTE_GROUP, ROUTE_GROUP)

        def head_body(h, carry2):
            kk = PEER_TOPK
            sc = []
            for j in range(2):
                c0 = pl.multiple_of((2 * h + j) * PEER_HALF, PEER_HALF)
                qc = q_scr[pl.ds(t0, ROUTE_GROUP), pl.ds(c0, PEER_HALF)]
                sc.append(lax.dot_general(keys_ref[2 * h + j], qc, (((1,), (1,)), ((), ())),
                                          preferred_element_type=jnp.float32))
            for r in range(kk):
                for j in range(2):
                    m, ki, sc[j] = _take_top(sc[j], key_row)
                    sh_scr[j, r:r + 1, :] = m
                    ih_scr[j, r:r + 1, :] = ki

            row8 = lax.broadcasted_iota(jnp.int32, (SUBLANES, ROUTE_GROUP), 0)
            s1_lo, s1_hi = sh_scr[1, 0:SUBLANES, :], sh_scr[1, SUBLANES:kk, :]
            i1_lo, i1_hi = ih_scr[1, 0:SUBLANES, :], ih_scr[1, SUBLANES:kk, :]
            vals = [sh_scr[0, 0:1, :] + s1_lo, sh_scr[0, 0:1, :] + s1_hi]
            cids = [ih_scr[0, 0:1, :] * N_KEYS + i1_lo, ih_scr[0, 0:1, :] * N_KEYS + i1_hi]
            poss = [row8, row8 + SUBLANES]
            for a in range(1, SUBLANES):
                v = sh_scr[0, a:a + 1, :] + s1_lo
                if kk // (a + 1) < SUBLANES:
                    v = jnp.where(row8 < kk // (a + 1), v, -jnp.inf)
                vals.append(v)
                cids.append(ih_scr[0, a:a + 1, :] * N_KEYS + i1_lo)
                poss.append(row8 + a * kk)
            vals.append(sh_scr[0, SUBLANES:kk, :] + sh_scr[1, 0:1, :])
            cids.append(ih_scr[0, SUBLANES:kk, :] * N_KEYS + ih_scr[1, 0:1, :])
            poss.append((row8 + SUBLANES) * kk)
            cand = jnp.concatenate(vals, axis=0)
            cidx = jnp.concatenate(cids, axis=0)
            cpos = jnp.concatenate(poss, axis=0)
            tops, ids = [], []
            for r in range(kk):
                m, e, cand = _take_top(cand, cpos, cidx)
                tops.append(m)
                ids.append(e)
            top_s = jnp.concatenate(tops, axis=0)
            ex = jnp.exp(top_s - tops[0])
            gate = ex / jnp.sum(ex, axis=0, keepdims=True)
            r0 = pl.multiple_of(h * kk, kk)
            idx_scr[pl.ds(r0, kk), :] = jnp.concatenate(ids, axis=0)
            g_scr[pl.ds(r0, kk), :] = gate
            return carry2
        lax.fori_loop(0, PEER_HEADS, head_body, 0)

        idx_ref[pl.ds(t0, ROUTE_GROUP), :] = idx_scr[...].T
        g_ref[pl.ds(t0, ROUTE_GROUP), :] = g_scr[...].T
        return carry
    lax.fori_loop(0, ROUTE_TM // ROUTE_GROUP, group_body, 0)


def peer_route(x2d, ycat, w_out_bf16, ffn_norm_w, w_q_bf16, keys_bf16):
    m, d = x2d.shape
    tm = ROUTE_TM
    tok = lambda i: (i, 0)
    full2 = lambda i: (0, 0)
    return pl.pallas_call(
        _route_kernel,
        grid=(m // tm,),
        in_specs=[
            pl.BlockSpec((tm, d), tok),
            pl.BlockSpec((tm, MIX_WIDTH), tok),
            pl.BlockSpec((MIX_WIDTH, d), full2),
            pl.BlockSpec((1, d), full2),
            pl.BlockSpec((d, PEER_HEADS * PEER_KEY_DIM), full2),
            pl.BlockSpec((N_SUBSETS, N_KEYS, PEER_HALF), lambda i: (0, 0, 0)),
        ],
        out_specs=[
            pl.BlockSpec((tm, d), tok),
            pl.BlockSpec((tm, d), tok),
            pl.BlockSpec((tm, PEER_NSEL), tok),
            pl.BlockSpec((tm, PEER_NSEL), tok),
        ],
        out_shape=[
            jax.ShapeDtypeStruct((m, d), jnp.float32),
            jax.ShapeDtypeStruct((m, d), jnp.float32),
            jax.ShapeDtypeStruct((m, PEER_NSEL), jnp.int32),
            jax.ShapeDtypeStruct((m, PEER_NSEL), jnp.float32),
        ],
        scratch_shapes=[
            pltpu.VMEM((tm, PEER_HEADS * PEER_KEY_DIM), jnp.bfloat16),
            pltpu.VMEM((2, PEER_TOPK, ROUTE_GROUP), jnp.float32),
            pltpu.VMEM((2, PEER_TOPK, ROUTE_GROUP), jnp.int32),
            pltpu.VMEM((PEER_NSEL, ROUTE_GROUP), jnp.int32),
            pltpu.VMEM((PEER_NSEL, ROUTE_GROUP), jnp.float32),
        ],
        compiler_params=pltpu.CompilerParams(
            dimension_semantics=("arbitrary",),
            vmem_limit_bytes=48 * 1024 * 1024),
        name="peer_route",
    )(x2d, ycat, w_out_bf16, ffn_norm_w.reshape(1, d), w_q_bf16, keys_bf16)


ATT_T = 256


def _diff_attn_kernel(lam_ref, q0_ref, q1_ref, k_ref, v_ref, sw_ref, o_ref,
                      m_scr, l_scr, acc_scr, *, out_scale):
    qb = pl.program_id(2)
    m_scr[...] = jnp.full(m_scr.shape, -jnp.inf, jnp.float32)
    l_scr[...] = jnp.zeros(l_scr.shape, jnp.float32)
    acc_scr[...] = jnp.zeros(acc_scr.shape, jnp.float32)
    row_chunk = lax.broadcasted_iota(jnp.int32, (ATT_T, ATT_T), 0) // CHUNK
    col_chunk = lax.broadcasted_iota(jnp.int32, (ATT_T, ATT_T), 1) // CHUNK
    diag_mask = col_chunk <= row_chunk

    def update(kb, masked):
        k0 = pl.multiple_of(kb * ATT_T, ATT_T)
        ks = k_ref[0, pl.ds(k0, ATT_T), :]
        vs = v_ref[0, pl.ds(k0, ATT_T), :]
        for j, q_ref in enumerate((q0_ref, q1_ref)):
            s = lax.dot_general(q_ref[0], ks, (((1,), (1,)), ((), ())),
                                preferred_element_type=jnp.float32)
            if masked:
                s = jnp.where(diag_mask, s, -jnp.inf)
            m_old = m_scr[j]
            m_new = jnp.maximum(m_old, jnp.max(s, axis=1, keepdims=True))
            alpha = jnp.exp(m_old - m_new)
            p = jnp.exp(s - m_new)
            l_scr[j] = alpha * l_scr[j] + jnp.sum(p, axis=1, keepdims=True)
            acc_scr[j] = alpha * acc_scr[j] + jnp.dot(p.astype(jnp.bfloat16), vs,
                                                      preferred_element_type=jnp.float32)
            m_scr[j] = m_new

    def body(kb, carry):
        update(kb, False)
        return carry
    lax.fori_loop(0, qb, body, 0)
    update(qb, True)

    o = acc_scr[0] / l_scr[0] - lam_ref[0] * (acc_scr[1] / l_scr[1])
    o = o * lax.rsqrt(jnp.mean(o * o, axis=-1, keepdims=True) + EPS) * sw_ref[...]
    o_ref[0] = o * out_scale


def diff_attention_core(q0, q1, k, v, lam, subln_w, out_scale):
    b, s, _ = q0.shape
    qspec = pl.BlockSpec((1, ATT_T, 2 * DIFF_QKDIM), lambda bi, h, i: (bi, i, h))
    kspec = pl.BlockSpec((1, s, 2 * DIFF_QKDIM), lambda bi, h, i: (bi, 0, h))
    vspec = pl.BlockSpec((1, s, DIFF_VDIM), lambda bi, h, i: (bi, 0, h))
    return pl.pallas_call(
        functools.partial(_diff_attn_kernel, out_scale=out_scale),
        grid=(b, DIFF_HEADS, s // ATT_T),
        in_specs=[
            pl.BlockSpec(memory_space=pltpu.SMEM),
            qspec, qspec, kspec, vspec,
            pl.BlockSpec((1, DIFF_VDIM), lambda bi, h, i: (0, 0)),
        ],
        out_specs=pl.BlockSpec((1, ATT_T, DIFF_VDIM), lambda bi, h, i: (bi, i, h)),
        out_shape=jax.ShapeDtypeStruct((b, s, DIFF_WIDTH), jnp.float32),
        scratch_shapes=[
            pltpu.VMEM((2, ATT_T, 1), jnp.float32),
            pltpu.VMEM((2, ATT_T, 1), jnp.float32),
            pltpu.VMEM((2, ATT_T, DIFF_VDIM), jnp.float32),
        ],
        compiler_params=pltpu.CompilerParams(
            dimension_semantics=("arbitrary", "arbitrary", "arbitrary"),
            vmem_limit_bytes=32 * 1024 * 1024),
        name="diff_attention",
    )(lam, q0, q1, k, v, subln_w.reshape(1, DIFF_VDIM))


def rmsnorm(x, w):
    xf = x.astype(jnp.float32)
    y = xf * lax.rsqrt(jnp.mean(xf * xf, axis=-1, keepdims=True) + EPS)
    return (y * w.astype(jnp.float32)).astype(x.dtype)


def rotary_tables(seq):
    inv = jnp.power(ROPE_THETA, -jnp.arange(0, ROPE_DIM, 2, dtype=jnp.float32) / ROPE_DIM)
    ang = jnp.arange(seq, dtype=jnp.float32)[:, None] * inv[None, :]
    return jnp.cos(ang), jnp.sin(ang)


def partial_rotary(x, cos, sin):
    half = ROPE_DIM // 2
    c = cos[:, None, None, :]
    s = sin[:, None, None, :]
    xf = x.astype(jnp.float32)
    x1, x2, xp = xf[..., :half], xf[..., half:ROPE_DIM], xf[..., ROPE_DIM:]
    out = jnp.concatenate([x1 * c - x2 * s, x1 * s + x2 * c, xp], axis=-1)
    return out.astype(x.dtype)


def causal_depthwise_conv(u, w, b):
    out = lax.conv_general_dilated(
        u, w[:, None, :].astype(u.dtype), window_strides=(1,),
        padding=[(CONV_WIDTH - 1, 0)],
        dimension_numbers=('NWC', 'WIO', 'NWC'),
        feature_group_count=u.shape[-1])
    return out + b.astype(u.dtype)


def ssd_branch(z, xbc, dt_raw, conv_w, conv_b, dt_bias, a_log, d_skip, norm_w):
    f32 = jnp.float32
    b, s, _ = z.shape
    nc = s // CHUNK
    G, HG = SSD_GROUPS, SSD_HEADS_PER_GROUP
    xbc = jax.nn.silu(causal_depthwise_conv(xbc, conv_w, conv_b))
    xs, bm, cm = jnp.split(xbc, [SSD_WIDTH, SSD_WIDTH + G * SSD_STATE], axis=-1)
    xs = xs.reshape(b, nc, CHUNK, G, HG, SSD_HEAD_DIM).astype(f32)
    bm = bm.reshape(b, nc, CHUNK, G, SSD_STATE).astype(f32)
    cm = cm.reshape(b, nc, CHUNK, G, SSD_STATE).astype(f32)
    dt = jax.nn.softplus(dt_raw.astype(f32) + dt_bias.astype(f32)).reshape(b, nc, CHUNK, G, HG)
    a = -jnp.exp(a_log.astype(f32)).reshape(G, HG)
    a_cs = jnp.cumsum(dt * a, axis=2)
    xdt = xs * dt[..., None]
    seg = a_cs[:, :, :, None] - a_cs[:, :, None, :]
    causal = jnp.tril(jnp.ones((CHUNK, CHUNK), dtype=bool))[:, :, None, None]
    decay = jnp.exp(jnp.where(causal, seg, -jnp.inf))
    cb = jnp.einsum('bclgn,bcsgn->bclsg', cm, bm)
    y_diag = jnp.einsum('bclsg,bclsgh,bcsghp->bclghp', cb, decay, xdt)
    decay_to_end = jnp.exp(a_cs[:, :, -1:] - a_cs)
    states = jnp.einsum('bclgn,bclgh,bclghp->bcghpn', bm, decay_to_end, xdt)
    chunk_decay = jnp.exp(a_cs[:, :, -1])

    def step(carry, inp):
        st, dc = inp
        return carry * dc[..., None, None] + st, carry

    init = jnp.zeros_like(states[:, 0])
    _, prev = lax.scan(step, init, (jnp.moveaxis(states, 1, 0), jnp.moveaxis(chunk_decay, 1, 0)))
    prev = jnp.moveaxis(prev, 0, 1)
    y_off = jnp.einsum('bclgn,bcghpn,bclgh->bclghp', cm, prev, jnp.exp(a_cs))
    y = y_diag + y_off + xs * d_skip.astype(f32).reshape(G, HG)[:, :, None]
    y = y.reshape(b, s, SSD_WIDTH)
    y = (y * jax.nn.silu(z.astype(f32))).reshape(b, s, G, SSD_WIDTH // G)
    y = y * lax.rsqrt(jnp.mean(y * y, axis=-1, keepdims=True) + EPS)
    y = y.reshape(b, s, SSD_WIDTH) * norm_w.astype(f32)
    return y.astype(z.dtype)


def diff_attention(q, k, v, qn_w, kn_w, lq1, lk1, lq2, lk2, subln_w, lambda_init, cos, sin):
    f32 = jnp.float32
    b, s, _ = q.shape
    q = q.reshape(b, s, DIFF_HEADS, 2, DIFF_QKDIM)
    k = k.reshape(b, s, DIFF_HEADS, 2, DIFF_QKDIM)
    q = partial_rotary(rmsnorm(q, qn_w), cos, sin).astype(f32)
    kf = partial_rotary(rmsnorm(k, kn_w), cos, sin).astype(f32)
    lam = (jnp.exp(jnp.sum(lq1.astype(f32) * lk1.astype(f32)))
           - jnp.exp(jnp.sum(lq2.astype(f32) * lk2.astype(f32))) + lambda_init)
    scale = DIFF_QKDIM ** -0.5
    branch = jnp.arange(2)
    q0 = (q * scale * (branch == 0)[:, None]).reshape(b, s, QK_COLS).astype(jnp.bfloat16)
    q1 = (q * scale * (branch == 1)[:, None]).reshape(b, s, QK_COLS).astype(jnp.bfloat16)
    kb = kf.reshape(b, s, QK_COLS).astype(jnp.bfloat16)
    vb = v.astype(jnp.bfloat16)
    return diff_attention_core(q0, q1, kb, vb, lam.reshape(1), subln_w, 1.0 - lambda_init)


def kernel(x, mix_norm_w, w_in, conv_w, conv_b, dt_bias, a_log, d_skip, ssd_norm_w, q_norm_w, k_norm_w, lambda_q1, lambda_k1, lambda_q2, lambda_k2, subln_w, w_out, ffn_norm_w, peer_w_q, peer_sub_keys, peer_u, peer_v):
    b, s, d = x.shape
    cos, sin = rotary_tables(s)
    layer = 0
    lambda_init = 0.8 - 0.6 * math.exp(-0.3 * layer)
    proj = norm_proj(x.reshape(b * s, d), mix_norm_w[layer],
                     w_in[layer].astype(jnp.bfloat16)).reshape(b, s, IN_COLS)
    z, xbc, dt_raw, q, k, v = jnp.split(proj, SPLITS, axis=-1)
    y_ssd = ssd_branch(z, xbc, dt_raw, conv_w[layer], conv_b[layer], dt_bias[layer],
                       a_log[layer], d_skip[layer], ssd_norm_w[layer])
    y_diff = diff_attention(q, k, v, q_norm_w[layer], k_norm_w[layer], lambda_q1[layer],
                            lambda_k1[layer], lambda_q2[layer], lambda_k2[layer],
                            subln_w[layer], lambda_init, cos, sin)
    ycat = jnp.concatenate([y_ssd, y_diff], axis=-1).reshape(b * s, MIX_WIDTH)
    keys = peer_sub_keys[layer].reshape(N_SUBSETS, N_KEYS, PEER_HALF).astype(jnp.bfloat16)
    x2, xn, idx, gates = peer_route(x.reshape(b * s, d), ycat, w_out[layer].astype(jnp.bfloat16),
                                    ffn_norm_w[layer], peer_w_q[layer].astype(jnp.bfloat16), keys)
    uv = jnp.concatenate([peer_u[layer], peer_v[layer]], axis=-1)
    out = peer_gather(xn, idx, gates, x2, uv)
    return out.reshape(b, s, d)
```

```python
import functools
import math

import jax
import jax.numpy as jnp
from jax import lax
from jax.experimental import pallas as pl
from jax.experimental.pallas import tpu as pltpu

D_MODEL = 1024
CHUNK = 64
EPS = 1e-6

SSD_WIDTH = D_MODEL // 2
SSD_HEAD_DIM = 64
SSD_HEADS = SSD_WIDTH // SSD_HEAD_DIM
SSD_GROUPS = 2
SSD_HEADS_PER_GROUP = SSD_HEADS // SSD_GROUPS
SSD_STATE = 128
CONV_WIDTH = 4
SSD_CONV_CH = SSD_WIDTH + 2 * SSD_GROUPS * SSD_STATE

DIFF_WIDTH = D_MODEL - SSD_WIDTH
DIFF_HEADS = 4
DIFF_VDIM = DIFF_WIDTH // DIFF_HEADS
DIFF_QKDIM = DIFF_VDIM // 2
ROPE_DIM = DIFF_QKDIM // 4
ROPE_THETA = 500000.0
Q_BLOCK = 128

QK_COLS = DIFF_HEADS * 2 * DIFF_QKDIM
SPLITS = (SSD_WIDTH,
          SSD_WIDTH + SSD_CONV_CH,
          SSD_WIDTH + SSD_CONV_CH + SSD_HEADS,
          SSD_WIDTH + SSD_CONV_CH + SSD_HEADS + QK_COLS,
          SSD_WIDTH + SSD_CONV_CH + SSD_HEADS + 2 * QK_COLS)
IN_COLS = SPLITS[-1] + DIFF_WIDTH
MIX_WIDTH = SSD_WIDTH + DIFF_WIDTH

PEER_HEADS = 8
N_KEYS = 128
N_EXPERTS = N_KEYS * N_KEYS
PEER_KEY_DIM = 256
PEER_HALF = PEER_KEY_DIM // 2
PEER_TOPK = 16
PEER_TOKEN_BLOCK = 128


def _norm_proj_kernel(x_ref, nw_ref, w_ref, o_ref):
    xf = x_ref[...]
    y = xf * lax.rsqrt(jnp.mean(xf * xf, axis=-1, keepdims=True) + EPS)
    y = (y * nw_ref[...]).astype(jnp.bfloat16)
    o_ref[...] = jnp.dot(y, w_ref[...], preferred_element_type=jnp.float32)


def norm_proj(x2d, norm_w, w_bf16, tm=512):
    m, d = x2d.shape
    n = w_bf16.shape[1]
    return pl.pallas_call(
        _norm_proj_kernel,
        grid=(m // tm,),
        in_specs=[
            pl.BlockSpec((tm, d), lambda i: (i, 0)),
            pl.BlockSpec((1, d), lambda i: (0, 0)),
            pl.BlockSpec((d, n), lambda i: (0, 0)),
        ],
        out_specs=pl.BlockSpec((tm, n), lambda i: (i, 0)),
        out_shape=jax.ShapeDtypeStruct((m, n), jnp.float32),
        compiler_params=pltpu.CompilerParams(
            dimension_semantics=("arbitrary",),
            vmem_limit_bytes=56 * 1024 * 1024),
        name="norm_proj",
    )(x2d, norm_w.reshape(1, d), w_bf16)


PEER_NSEL = PEER_HEADS * PEER_TOPK
PEER_TB = 8
PEER_SLOTS = 3
LANES = 128
SUBLANES = 8
_SQRT_HALF = 0.7071067811865476


def _peer_gather_kernel(idx_cur, idx_mid, idx_nxt, xn_ref, g_ref, xres_ref, uv_hbm,
                        o_ref, buf, part_scr, wb_scr, sem):
    i = pl.program_id(0)
    nb = pl.num_programs(0)
    slot = i % PEER_SLOTS
    nxt = (i + PEER_SLOTS - 1) % PEER_SLOTS
    d = xn_ref.shape[-1]
    n_groups = PEER_NSEL // SUBLANES
    n_chunks = d // LANES
    half = PEER_TB // 2

    def start_row(idx_ref, s, tt, kk, j):
        e = idx_ref[tt * PEER_NSEL + kk * SUBLANES + j]
        pltpu.make_async_copy(uv_hbm.at[pl.ds(e, 1)], buf.at[s, tt, kk, pl.ds(j, 1)],
                              sem.at[s]).start()

    @pl.when(i == 0)
    def _():
        for s, idx_ref in ((0, idx_cur), (1, idx_mid)):
            for tt in range(PEER_TB):
                def body(kk, carry, tt=tt, s=s, idx_ref=idx_ref):
                    for j in range(SUBLANES):
                        start_row(idx_ref, s, tt, kk, j)
                    return carry
                lax.fori_loop(0, n_groups, body, 0)

    pltpu.make_async_copy(buf.at[slot], buf.at[slot], sem.at[slot]).wait()

    for tt in range(half):
        def body(kk, carry, tt=tt):
            r0 = pl.multiple_of(kk * SUBLANES, SUBLANES)
            for n, tok in enumerate((2 * tt, 2 * tt + 1)):
                prod = buf[slot, tok, kk, :, 0:d] * xn_ref[tok:tok + 1, :]
                part = prod[:, 0:LANES]
                for c in range(1, n_chunks):
                    part = part + prod[:, c * LANES:(c + 1) * LANES]
                part_scr[tok, pl.ds(r0, SUBLANES), :] = part
                for j in range(n * half, (n + 1) * half):
                    start_row(idx_nxt, nxt, tt, kk, j)
            return carry
        lax.fori_loop(0, n_groups, body, 0)

    lane = lax.broadcasted_iota(jnp.int32, (PEER_NSEL, PEER_TB), 1)
    hm = jnp.zeros((PEER_NSEL, PEER_TB), jnp.float32)
    for tok in range(PEER_TB):
        h = jnp.sum(part_scr[tok], axis=1, keepdims=True)
        hm = jnp.where(lane == tok, h, hm)
    eye = (lax.broadcasted_iota(jnp.int32, (PEER_NSEL, PEER_NSEL), 0)
           == lax.broadcasted_iota(jnp.int32, (PEER_NSEL, PEER_NSEL), 1)).astype(jnp.float32)
    g_t = lax.dot_general(eye, g_ref[...], (((1,), (1,)), ((), ())),
                          precision=lax.Precision.HIGHEST,
                          preferred_element_type=jnp.float32)
    w = g_t * (0.5 * hm * (1.0 + lax.erf(hm * _SQRT_HALF)))
    for tok in range(PEER_TB):
        wb_scr[tok] = jnp.broadcast_to(w[:, tok:tok + 1], (PEER_NSEL, LANES))

    for tt in range(half, PEER_TB):
        toks = (2 * (tt - half), 2 * (tt - half) + 1)

        def body(kk, accs, tt=tt, toks=toks):
            r0 = pl.multiple_of(kk * SUBLANES, SUBLANES)
            out = []
            for n, tok in enumerate(toks):
                wg = wb_scr[tok, pl.ds(r0, SUBLANES), :]
                rows = buf[slot, tok, kk, :, d:2 * d]
                out.append(accs[n] + rows * jnp.concatenate([wg] * n_chunks, axis=1))
                for j in range(n * half, (n + 1) * half):
                    start_row(idx_nxt, nxt, tt, kk, j)
            return tuple(out)
        zero = jnp.zeros((SUBLANES, d), jnp.float32)
        accs = lax.fori_loop(0, n_groups, body, (zero, zero))
        for n, tok in enumerate(toks):
            o_ref[tok:tok + 1, :] = (xres_ref[tok:tok + 1, :]
                                     + jnp.sum(accs[n], axis=0, keepdims=True))

    @pl.when(i == nb - 1)
    def _():
        for s in (nxt, (i + 1) % PEER_SLOTS):
            pltpu.make_async_copy(buf.at[s], buf.at[s], sem.at[s]).wait()


def peer_gather(xn, idx, gates, xres, uv):
    m, d = xn.shape
    nb = m // PEER_TB
    tok = lambda i: (i, 0)
    idx_flat = idx.reshape(m * PEER_NSEL)
    return pl.pallas_call(
        _peer_gather_kernel,
        grid=(nb,),
        in_specs=[
            pl.BlockSpec((PEER_TB * PEER_NSEL,), lambda i: (i,), memory_space=pltpu.SMEM),
            pl.BlockSpec((PEER_TB * PEER_NSEL,), lambda i: (jnp.minimum(i + 1, nb - 1),),
                         memory_space=pltpu.SMEM),
            pl.BlockSpec((PEER_TB * PEER_NSEL,), lambda i: (jnp.minimum(i + 2, nb - 1),),
                         memory_space=pltpu.SMEM),
            pl.BlockSpec((PEER_TB, d), tok),
            pl.BlockSpec((PEER_TB, PEER_NSEL), tok),
            pl.BlockSpec((PEER_TB, d), tok),
            pl.BlockSpec(memory_space=pl.ANY),
        ],
        out_specs=pl.BlockSpec((PEER_TB, d), tok),
        out_shape=jax.ShapeDtypeStruct((m, d), jnp.float32),
        scratch_shapes=[
            pltpu.VMEM((PEER_SLOTS, PEER_TB, PEER_NSEL // SUBLANES, SUBLANES, 2 * d), jnp.float32),
            pltpu.VMEM((PEER_TB, PEER_NSEL, LANES), jnp.float32),
            pltpu.VMEM((PEER_TB, PEER_NSEL, LANES), jnp.float32),
            pltpu.SemaphoreType.DMA((PEER_SLOTS,)),
        ],
        compiler_params=pltpu.CompilerParams(
            dimension_semantics=("arbitrary",),
            vmem_limit_bytes=48 * 1024 * 1024,
            disable_bounds_checks=True),
        name="peer_gather",
    )(idx_flat, idx_flat, idx_flat, xn, gates, xres, uv)


ROUTE_TM = 256
ROUTE_GROUP = LANES
N_SUBSETS = PEER_HEADS * 2


_POS_SENTINEL = 1 << 20


def _take_top(vals, pos, payload=None):
    m = jnp.max(vals, axis=0, keepdims=True)
    p = jnp.min(jnp.where(vals == m, pos, _POS_SENTINEL), axis=0, keepdims=True)
    hit = pos == p
    picked = p if payload is None else jnp.max(jnp.where(hit, payload, -1), axis=0, keepdims=True)
    return m, picked, jnp.where(hit, -jnp.inf, vals)


def _route_kernel(x_ref, y_ref, wo_ref, nw_ref, wq_ref, keys_ref,
                  xo_ref, xn_ref, idx_ref, g_ref,
                  q_scr, sh_scr, ih_scr, idx_scr, g_scr):
    xa = x_ref[...] + jnp.dot(y_ref[...].astype(jnp.bfloat16), wo_ref[...],
                              preferred_element_type=jnp.float32)
    xo_ref[...] = xa
    xn = xa * lax.rsqrt(jnp.mean(xa * xa, axis=-1, keepdims=True) + EPS) * nw_ref[...]
    xn_ref[...] = xn
    q_scr[...] = jnp.dot(xn.astype(jnp.bfloat16), wq_ref[...],
                         preferred_element_type=jnp.float32).astype(jnp.bfloat16)

    key_row = lax.broadcasted_iota(jnp.int32, (N_KEYS, ROUTE_GROUP), 0)

    def group_body(gi, carry):
        t0 = pl.multiple_of(gi * ROUTE_GROUP, ROUTE_GROUP)

        def head_body(h, carry2):
            kk = PEER_TOPK
            sc = []
            for j in range(2):
                c0 = pl.multiple_of((2 * h + j) * PEER_HALF, PEER_HALF)
                qc = q_scr[pl.ds(t0, ROUTE_GROUP), pl.ds(c0, PEER_HALF)]
                sc.append(lax.dot_general(keys_ref[2 * h + j], qc, (((1,), (1,)), ((), ())),
                                          preferred_element_type=jnp.float32))
            for r in range(kk):
                for j in range(2):
                    m, ki, sc[j] = _take_top(sc[j], key_row)
                    sh_scr[j, r:r + 1, :] = m
                    ih_scr[j, r:r + 1, :] = ki

            row8 = lax.broadcasted_iota(jnp.int32, (SUBLANES, ROUTE_GROUP), 0)
            s1_lo, s1_hi = sh_scr[1, 0:SUBLANES, :], sh_scr[1, SUBLANES:kk, :]
            i1_lo, i1_hi = ih_scr[1, 0:SUBLANES, :], ih_scr[1, SUBLANES:kk, :]
            vals = [sh_scr[0, 0:1, :] + s1_lo, sh_scr[0, 0:1, :] + s1_hi]
            cids = [ih_scr[0, 0:1, :] * N_KEYS + i1_lo, ih_scr[0, 0:1, :] * N_KEYS + i1_hi]
            poss = [row8, row8 + SUBLANES]
            for a in range(1, SUBLANES):
                v = sh_scr[0, a:a + 1, :] + s1_lo
                if kk // (a + 1) < SUBLANES:
                    v = jnp.where(row8 < kk // (a + 1), v, -jnp.inf)
                vals.append(v)
                cids.append(ih_scr[0, a:a + 1, :] * N_KEYS + i1_lo)
                poss.append(row8 + a * kk)
            vals.append(sh_scr[0, SUBLANES:kk, :] + sh_scr[1, 0:1, :])
            cids.append(ih_scr[0, SUBLANES:kk, :] * N_KEYS + ih_scr[1, 0:1, :])
            poss.append((row8 + SUBLANES) * kk)
            cand = jnp.concatenate(vals, axis=0)
            cidx = jnp.concatenate(cids, axis=0)
            cpos = jnp.concatenate(poss, axis=0)
            tops, ids = [], []
            for r in range(kk):
                m, e, cand = _take_top(cand, cpos, cidx)
                tops.append(m)
                ids.append(e)
            top_s = jnp.concatenate(tops, axis=0)
            ex = jnp.exp(top_s - tops[0])
            gate = ex / jnp.sum(ex, axis=0, keepdims=True)
            r0 = pl.multiple_of(h * kk, kk)
            idx_scr[pl.ds(r0, kk), :] = jnp.concatenate(ids, axis=0)
            g_scr[pl.ds(r0, kk), :] = gate
            return carry2
        lax.fori_loop(0, PEER_HEADS, head_body, 0)

        idx_ref[pl.ds(t0, ROUTE_GROUP), :] = idx_scr[...].T
        g_ref[pl.ds(t0, ROUTE_GROUP), :] = g_scr[...].T
        return carry
    lax.fori_loop(0, ROUTE_TM // ROUTE_GROUP, group_body, 0)


def peer_route(x2d, ycat, w_out_bf16, ffn_norm_w, w_q_bf16, keys_bf16):
    m, d = x2d.shape
    tm = ROUTE_TM
    tok = lambda i: (i, 0)
    full2 = lambda i: (0, 0)
    return pl.pallas_call(
        _route_kernel,
        grid=(m // tm,),
        in_specs=[
            pl.BlockSpec((tm, d), tok),
            pl.BlockSpec((tm, MIX_WIDTH), tok),
            pl.BlockSpec((MIX_WIDTH, d), full2),
            pl.BlockSpec((1, d), full2),
            pl.BlockSpec((d, PEER_HEADS * PEER_KEY_DIM), full2),
            pl.BlockSpec((N_SUBSETS, N_KEYS, PEER_HALF), lambda i: (0, 0, 0)),
        ],
        out_specs=[
            pl.BlockSpec((tm, d), tok),
            pl.BlockSpec((tm, d), tok),
            pl.BlockSpec((tm, PEER_NSEL), tok),
            pl.BlockSpec((tm, PEER_NSEL), tok),
        ],
        out_shape=[
            jax.ShapeDtypeStruct((m, d), jnp.float32),
            jax.ShapeDtypeStruct((m, d), jnp.float32),
            jax.ShapeDtypeStruct((m, PEER_NSEL), jnp.int32),
            jax.ShapeDtypeStruct((m, PEER_NSEL), jnp.float32),
        ],
        scratch_shapes=[
            pltpu.VMEM((tm, PEER_HEADS * PEER_KEY_DIM), jnp.bfloat16),
            pltpu.VMEM((2, PEER_TOPK, ROUTE_GROUP), jnp.float32),
            pltpu.VMEM((2, PEER_TOPK, ROUTE_GROUP), jnp.int32),
            pltpu.VMEM((PEER_NSEL, ROUTE_GROUP), jnp.int32),
            pltpu.VMEM((PEER_NSEL, ROUTE_GROUP), jnp.float32),
        ],
        compiler_params=pltpu.CompilerParams(
            dimension_semantics=("arbitrary",),
            vmem_limit_bytes=48 * 1024 * 1024),
        name="peer_route",
    )(x2d, ycat, w_out_bf16, ffn_norm_w.reshape(1, d), w_q_bf16, keys_bf16)


ATT_T = 256


def _diff_attn_kernel(lam_ref, q0_ref, q1_ref, k_ref, v_ref, sw_ref, o_ref,
                      m_scr, l_scr, acc_scr, *, out_scale):
    qb = pl.program_id(2)
    m_scr[...] = jnp.full(m_scr.shape, -jnp.inf, jnp.float32)
    l_scr[...] = jnp.zeros(l_scr.shape, jnp.float32)
    acc_scr[...] = jnp.zeros(acc_scr.shape, jnp.float32)
    row_chunk = lax.broadcasted_iota(jnp.int32, (ATT_T, ATT_T), 0) // CHUNK
    col_chunk = lax.broadcasted_iota(jnp.int32, (ATT_T, ATT_T), 1) // CHUNK
    diag_mask = col_chunk <= row_chunk

    def update(kb, masked):
        k0 = pl.multiple_of(kb * ATT_T, ATT_T)
        ks = k_ref[0, pl.ds(k0, ATT_T), :]
        vs = v_ref[0, pl.ds(k0, ATT_T), :]
        for j, q_ref in enumerate((q0_ref, q1_ref)):
            s = lax.dot_general(q_ref[0], ks, (((1,), (1,)), ((), ())),
                                preferred_element_type=jnp.float32)
            if masked:
                s = jnp.where(diag_mask, s, -jnp.inf)
            m_old = m_scr[j]
            m_new = jnp.maximum(m_old, jnp.max(s, axis=1, keepdims=True))
            alpha = jnp.exp(m_old - m_new)
            p = jnp.exp(s - m_new)
            l_scr[j] = alpha * l_scr[j] + jnp.sum(p, axis=1, keepdims=True)
            acc_scr[j] = alpha * acc_scr[j] + jnp.dot(p.astype(jnp.bfloat16), vs,
                                                      preferred_element_type=jnp.float32)
            m_scr[j] = m_new

    def body(kb, carry):
        update(kb, False)
        return carry
    lax.fori_loop(0, qb, body, 0)
    update(qb, True)

    o = acc_scr[0] / l_scr[0] - lam_ref[0] * (acc_scr[1] / l_scr[1])
    o = o * lax.rsqrt(jnp.mean(o * o, axis=-1, keepdims=True) + EPS) * sw_ref[...]
    o_ref[0] = o * out_scale


def diff_attention_core(q0, q1, k, v, lam, subln_w, out_scale):
    b, s, _ = q0.shape
    qspec = pl.BlockSpec((1, ATT_T, 2 * DIFF_QKDIM), lambda bi, h, i: (bi, i, h))
    kspec = pl.BlockSpec((1, s, 2 * DIFF_QKDIM), lambda bi, h, i: (bi, 0, h))
    vspec = pl.BlockSpec((1, s, DIFF_VDIM), lambda bi, h, i: (bi, 0, h))
    return pl.pallas_call(
        functools.partial(_diff_attn_kernel, out_scale=out_scale),
        grid=(b, DIFF_HEADS, s // ATT_T),
        in_specs=[
            pl.BlockSpec(memory_space=pltpu.SMEM),
            qspec, qspec, kspec, vspec,
            pl.BlockSpec((1, DIFF_VDIM), lambda bi, h, i: (0, 0)),
        ],
        out_specs=pl.BlockSpec((1, ATT_T, DIFF_VDIM), lambda bi, h, i: (bi, i, h)),
        out_shape=jax.ShapeDtypeStruct((b, s, DIFF_WIDTH), jnp.float32),
        scratch_shapes=[
            pltpu.VMEM((2, ATT_T, 1), jnp.float32),
            pltpu.VMEM((2, ATT_T, 1), jnp.float32),
            pltpu.VMEM((2, ATT_T, DIFF_VDIM), jnp.float32),
        ],
        compiler_params=pltpu.CompilerParams(
            dimension_semantics=("arbitrary", "arbitrary", "arbitrary"),
            vmem_limit_bytes=32 * 1024 * 1024),
        name="diff_attention",
    )(lam, q0, q1, k, v, subln_w.reshape(1, DIFF_VDIM))


def rmsnorm(x, w):
    xf = x.astype(jnp.float32)
    y = xf * lax.rsqrt(jnp.mean(xf * xf, axis=-1, keepdims=True) + EPS)
    return (y * w.astype(jnp.float32)).astype(x.dtype)


def rotary_tables(seq):
    inv = jnp.power(ROPE_THETA, -jnp.arange(0, ROPE_DIM, 2, dtype=jnp.float32) / ROPE_DIM)
    ang = jnp.arange(seq, dtype=jnp.float32)[:, None] * inv[None, :]
    return jnp.cos(ang), jnp.sin(ang)


def partial_rotary(x, cos, sin):
    half = ROPE_DIM // 2
    c = cos[:, None, None, :]
    s = sin[:, None, None, :]
    xf = x.astype(jnp.float32)
    x1, x2, xp = xf[..., :half], xf[..., half:ROPE_DIM], xf[..., ROPE_DIM:]
    out = jnp.concatenate([x1 * c - x2 * s, x1 * s + x2 * c, xp], axis=-1)
    return out.astype(x.dtype)


def causal_depthwise_conv(u, w, b):
    out = lax.conv_general_dilated(
        u, w[:, None, :].astype(u.dtype), window_strides=(1,),
        padding=[(CONV_WIDTH - 1, 0)],
        dimension_numbers=('NWC', 'WIO', 'NWC'),
        feature_group_count=u.shape[-1])
    return out + b.astype(u.dtype)


def ssd_branch(z, xbc, dt_raw, conv_w, conv_b, dt_bias, a_log, d_skip, norm_w):
    f32 = jnp.float32
    b, s, _ = z.shape
    nc = s // CHUNK
    G, HG = SSD_GROUPS, SSD_HEADS_PER_GROUP
    xbc = jax.nn.silu(causal_depthwise_conv(xbc, conv_w, conv_b))
    xs, bm, cm = jnp.split(xbc, [SSD_WIDTH, SSD_WIDTH + G * SSD_STATE], axis=-1)
    xs = xs.reshape(b, nc, CHUNK, G, HG, SSD_HEAD_DIM).astype(f32)
    bm = bm.reshape(b, nc, CHUNK, G, SSD_STATE).astype(f32)
    cm = cm.reshape(b, nc, CHUNK, G, SSD_STATE).astype(f32)
    dt = jax.nn.softplus(dt_raw.astype(f32) + dt_bias.astype(f32)).reshape(b, nc, CHUNK, G, HG)
    a = -jnp.exp(a_log.astype(f32)).reshape(G, HG)
    a_cs = jnp.cumsum(dt * a, axis=2)
    xdt = xs * dt[..., None]
    seg = a_cs[:, :, :, None] - a_cs[:, :, None, :]
    causal = jnp.tril(jnp.ones((CHUNK, CHUNK), dtype=bool))[:, :, None, None]
    decay = jnp.exp(jnp.where(causal, seg, -jnp.inf))
    cb = jnp.einsum('bclgn,bcsgn->bclsg', cm, bm)
    y_diag = jnp.einsum('bclsg,bclsgh,bcsghp->bclghp', cb, decay, xdt)
    decay_to_end = jnp.exp(a_cs[:, :, -1:] - a_cs)
    states = jnp.einsum('bclgn,bclgh,bclghp->bcghpn', bm, decay_to_end, xdt)
    chunk_decay = jnp.exp(a_cs[:, :, -1])

    def step(carry, inp):
        st, dc = inp
        return carry * dc[..., None, None] + st, carry

    init = jnp.zeros_like(states[:, 0])
    _, prev = lax.scan(step, init, (jnp.moveaxis(states, 1, 0), jnp.moveaxis(chunk_decay, 1, 0)))
    prev = jnp.moveaxis(prev, 0, 1)
    y_off = jnp.einsum('bclgn,bcghpn,bclgh->bclghp', cm, prev, jnp.exp(a_cs))
    y = y_diag + y_off + xs * d_skip.astype(f32).reshape(G, HG)[:, :, None]
    y = y.reshape(b, s, SSD_WIDTH)
    y = (y * jax.nn.silu(z.astype(f32))).reshape(b, s, G, SSD_WIDTH // G)
    y = y * lax.rsqrt(jnp.mean(y * y, axis=-1, keepdims=True) + EPS)
    y = y.reshape(b, s, SSD_WIDTH) * norm_w.astype(f32)
    return y.astype(z.dtype)


def diff_attention(q, k, v, qn_w, kn_w, lq1, lk1, lq2, lk2, subln_w, lambda_init, cos, sin):
    f32 = jnp.float32
    b, s, _ = q.shape
    q = q.reshape(b, s, DIFF_HEADS, 2, DIFF_QKDIM)
    k = k.reshape(b, s, DIFF_HEADS, 2, DIFF_QKDIM)
    q = partial_rotary(rmsnorm(q, qn_w), cos, sin).astype(f32)
    kf = partial_rotary(rmsnorm(k, kn_w), cos, sin).astype(f32)
    lam = (jnp.exp(jnp.sum(lq1.astype(f32) * lk1.astype(f32)))
           - jnp.exp(jnp.sum(lq2.astype(f32) * lk2.astype(f32))) + lambda_init)
    scale = DIFF_QKDIM ** -0.5
    branch = jnp.arange(2)
    q0 = (q * scale * (branch == 0)[:, None]).reshape(b, s, QK_COLS).astype(jnp.bfloat16)
    q1 = (q * scale * (branch == 1)[:, None]).reshape(b, s, QK_COLS).astype(jnp.bfloat16)
    kb = kf.reshape(b, s, QK_COLS).astype(jnp.bfloat16)
    vb = v.astype(jnp.bfloat16)
    return diff_attention_core(q0, q1, kb, vb, lam.reshape(1), subln_w, 1.0 - lambda_init)


def kernel(x, mix_norm_w, w_in, conv_w, conv_b, dt_bias, a_log, d_skip, ssd_norm_w, q_norm_w, k_norm_w, lambda_q1, lambda_k1, lambda_q2, lambda_k2, subln_w, w_out, ffn_norm_w, peer_w_q, peer_sub_keys, peer_u, peer_v):
    b, s, d = x.shape
    cos, sin = rotary_tables(s)
    layer = 0
    lambda_init = 0.8 - 0.6 * math.exp(-0.3 * layer)
    proj = norm_proj(x.reshape(b * s, d), mix_norm_w[layer],
                     w_in[layer].astype(jnp.bfloat16)).reshape(b, s, IN_COLS)
    z, xbc, dt_raw, q, k, v = jnp.split(proj, SPLITS, axis=-1)
    y_ssd = ssd_branch(z, xbc, dt_raw, conv_w[layer], conv_b[layer], dt_bias[layer],
                       a_log[layer], d_skip[layer], ssd_norm_w[layer])
    y_diff = diff_attention(q, k, v, q_norm_w[layer], k_norm_w[layer], lambda_q1[layer],
                            lambda_k1[layer], lambda_q2[layer], lambda_k2[layer],
                            subln_w[layer], lambda_init, cos, sin)
    ycat = jnp.concatenate([y_ssd, y_diff], axis=-1).reshape(b * s, MIX_WIDTH)
    keys = peer_sub_keys[layer].reshape(N_SUBSETS, N_KEYS, PEER_HALF).astype(jnp.bfloat16)
    x2, xn, idx, gates = peer_route(x.reshape(b * s, d), ycat, w_out[layer].astype(jnp.bfloat16),
                                    ffn_norm_w[layer], peer_w_q[layer].astype(jnp.bfloat16), keys)
    uv = jnp.concatenate([peer_u[layer], peer_v[layer]], axis=-1)
    out = peer_gather(xn, idx, gates, x2, uv)
    return out.reshape(b, s, d)
```

```python
import functools
import math

import jax
import jax.numpy as jnp
from jax import lax
from jax.experimental import pallas as pl
from jax.experimental.pallas import tpu as pltpu

D_MODEL = 1024
CHUNK = 64
EPS = 1e-6

SSD_WIDTH = D_MODEL // 2
SSD_HEAD_DIM = 64
SSD_HEADS = SSD_WIDTH // SSD_HEAD_DIM
SSD_GROUPS = 2
SSD_HEADS_PER_GROUP = SSD_HEADS // SSD_GROUPS
SSD_STATE = 128
CONV_WIDTH = 4
SSD_CONV_CH = SSD_WIDTH + 2 * SSD_GROUPS * SSD_STATE

DIFF_WIDTH = D_MODEL - SSD_WIDTH
DIFF_HEADS = 4
DIFF_VDIM = DIFF_WIDTH // DIFF_HEADS
DIFF_QKDIM = DIFF_VDIM // 2
ROPE_DIM = DIFF_QKDIM // 4
ROPE_THETA = 500000.0
Q_BLOCK = 128

QK_COLS = DIFF_HEADS * 2 * DIFF_QKDIM
SPLITS = (SSD_WIDTH,
          SSD_WIDTH + SSD_CONV_CH,
          SSD_WIDTH + SSD_CONV_CH + SSD_HEADS,
          SSD_WIDTH + SSD_CONV_CH + SSD_HEADS + QK_COLS,
          SSD_WIDTH + SSD_CONV_CH + SSD_HEADS + 2 * QK_COLS)
IN_COLS = SPLITS[-1] + DIFF_WIDTH
MIX_WIDTH = SSD_WIDTH + DIFF_WIDTH

PEER_HEADS = 8
N_KEYS = 128
N_EXPERTS = N_KEYS * N_KEYS
PEER_KEY_DIM = 256
PEER_HALF = PEER_KEY_DIM // 2
PEER_TOPK = 16
PEER_TOKEN_BLOCK = 128


def _norm_proj_kernel(x_ref, nw_ref, w_ref, o_ref):
    xf = x_ref[...]
    y = xf * lax.rsqrt(jnp.mean(xf * xf, axis=-1, keepdims=True) + EPS)
    y = (y * nw_ref[...]).astype(jnp.bfloat16)
    o_ref[...] = jnp.dot(y, w_ref[...], preferred_element_type=jnp.float32)


def norm_proj(x2d, norm_w, w_bf16, tm=512):
    m, d = x2d.shape
    n = w_bf16.shape[1]
    return pl.pallas_call(
        _norm_proj_kernel,
        grid=(m // tm,),
        in_specs=[
            pl.BlockSpec((tm, d), lambda i: (i, 0)),
            pl.BlockSpec((1, d), lambda i: (0, 0)),
            pl.BlockSpec((d, n), lambda i: (0, 0)),
        ],
        out_specs=pl.BlockSpec((tm, n), lambda i: (i, 0)),
        out_shape=jax.ShapeDtypeStruct((m, n), jnp.float32),
        compiler_params=pltpu.CompilerParams(
            dimension_semantics=("arbitrary",),
            vmem_limit_bytes=56 * 1024 * 1024),
        name="norm_proj",
    )(x2d, norm_w.reshape(1, d), w_bf16)


PEER_NSEL = PEER_HEADS * PEER_TOPK
PEER_TB = 8
PEER_SLOTS = 3
LANES = 128
SUBLANES = 8
_SQRT_HALF = 0.7071067811865476
_U_MASK = 0xFFFF0000


def _peer_gather_kernel(idx_cur, idx_mid, idx_nxt, xn_ref, g_ref, xres_ref, uv_hbm,
                        o_ref, buf, part_scr, wb_scr, sem):
    i = pl.program_id(0)
    nb = pl.num_programs(0)
    slot = i % PEER_SLOTS
    nxt = (i + PEER_SLOTS - 1) % PEER_SLOTS
    d = xn_ref.shape[-1]
    n_groups = PEER_NSEL // SUBLANES
    n_chunks = d // LANES
    half = PEER_TB // 2

    def start_row(idx_ref, s, tt, kk, j):
        e = idx_ref[tt * PEER_NSEL + kk * SUBLANES + j]
        pltpu.make_async_copy(uv_hbm.at[pl.ds(e, 1)], buf.at[s, tt, kk, pl.ds(j, 1)],
                              sem.at[s]).start()

    @pl.when(i == 0)
    def _():
        for s, idx_ref in ((0, idx_cur), (1, idx_mid)):
            for tt in range(PEER_TB):
                def body(kk, carry, tt=tt, s=s, idx_ref=idx_ref):
                    for j in range(SUBLANES):
                        start_row(idx_ref, s, tt, kk, j)
                    return carry
                lax.fori_loop(0, n_groups, body, 0)

    pltpu.make_async_copy(buf.at[slot], buf.at[slot], sem.at[slot]).wait()

    for tt in range(half):
        def body(kk, carry, tt=tt):
            r0 = pl.multiple_of(kk * SUBLANES, SUBLANES)
            for n, tok in enumerate((2 * tt, 2 * tt + 1)):
                urows = lax.bitcast_convert_type(buf[slot, tok, kk] & jnp.uint32(_U_MASK),
                                                 jnp.float32)
                prod = urows * xn_ref[tok:tok + 1, :]
                part = prod[:, 0:LANES]
                for c in range(1, n_chunks):
                    part = part + prod[:, c * LANES:(c + 1) * LANES]
                part_scr[tok, pl.ds(r0, SUBLANES), :] = part
                for j in range(n * half, (n + 1) * half):
                    start_row(idx_nxt, nxt, tt, kk, j)
            return carry
        lax.fori_loop(0, n_groups, body, 0)

    lane = lax.broadcasted_iota(jnp.int32, (PEER_NSEL, PEER_TB), 1)
    hm = jnp.zeros((PEER_NSEL, PEER_TB), jnp.float32)
    for tok in range(PEER_TB):
        h = jnp.sum(part_scr[tok], axis=1, keepdims=True)
        hm = jnp.where(lane == tok, h, hm)
    eye = (lax.broadcasted_iota(jnp.int32, (PEER_NSEL, PEER_NSEL), 0)
           == lax.broadcasted_iota(jnp.int32, (PEER_NSEL, PEER_NSEL), 1)).astype(jnp.float32)
    g_t = lax.dot_general(eye, g_ref[...], (((1,), (1,)), ((), ())),
                          precision=lax.Precision.HIGHEST,
                          preferred_element_type=jnp.float32)
    w = g_t * (0.5 * hm * (1.0 + lax.erf(hm * _SQRT_HALF)))
    for tok in range(PEER_TB):
        wb_scr[tok] = jnp.broadcast_to(w[:, tok:tok + 1], (PEER_NSEL, LANES))

    for tt in range(half, PEER_TB):
        toks = (2 * (tt - half), 2 * (tt - half) + 1)

        def body(kk, accs, tt=tt, toks=toks):
            r0 = pl.multiple_of(kk * SUBLANES, SUBLANES)
            out = []
            for n, tok in enumerate(toks):
                wg = wb_scr[tok, pl.ds(r0, SUBLANES), :]
                rows = lax.bitcast_convert_type(buf[slot, tok, kk] << 16, jnp.float32)
                out.append(accs[n] + rows * jnp.concatenate([wg] * n_chunks, axis=1))
                for j in range(n * half, (n + 1) * half):
                    start_row(idx_nxt, nxt, tt, kk, j)
            return tuple(out)
        zero = jnp.zeros((SUBLANES, d), jnp.float32)
        accs = lax.fori_loop(0, n_groups, body, (zero, zero))
        for n, tok in enumerate(toks):
            o_ref[tok:tok + 1, :] = (xres_ref[tok:tok + 1, :]
                                     + jnp.sum(accs[n], axis=0, keepdims=True))

    @pl.when(i == nb - 1)
    def _():
        for s in (nxt, (i + 1) % PEER_SLOTS):
            pltpu.make_async_copy(buf.at[s], buf.at[s], sem.at[s]).wait()


def pack_uv(u, v):
    to_bits = lambda a: lax.bitcast_convert_type(a.astype(jnp.bfloat16), jnp.uint16).astype(jnp.uint32)
    return (to_bits(u) << 16) | to_bits(v)


def peer_gather(xn, idx, gates, xres, uv):
    m, d = xn.shape
    nb = m // PEER_TB
    tok = lambda i: (i, 0)
    idx_flat = idx.reshape(m * PEER_NSEL)
    return pl.pallas_call(
        _peer_gather_kernel,
        grid=(nb,),
        in_specs=[
            pl.BlockSpec((PEER_TB * PEER_NSEL,), lambda i: (i,), memory_space=pltpu.SMEM),
            pl.BlockSpec((PEER_TB * PEER_NSEL,), lambda i: (jnp.minimum(i + 1, nb - 1),),
                         memory_space=pltpu.SMEM),
            pl.BlockSpec((PEER_TB * PEER_NSEL,), lambda i: (jnp.minimum(i + 2, nb - 1),),
                         memory_space=pltpu.SMEM),
            pl.BlockSpec((PEER_TB, d), tok),
            pl.BlockSpec((PEER_TB, PEER_NSEL), tok),
            pl.BlockSpec((PEER_TB, d), tok),
            pl.BlockSpec(memory_space=pl.ANY),
        ],
        out_specs=pl.BlockSpec((PEER_TB, d), tok),
        out_shape=jax.ShapeDtypeStruct((m, d), jnp.float32),
        scratch_shapes=[
            pltpu.VMEM((PEER_SLOTS, PEER_TB, PEER_NSEL // SUBLANES, SUBLANES, d), jnp.uint32),
            pltpu.VMEM((PEER_TB, PEER_NSEL, LANES), jnp.float32),
            pltpu.VMEM((PEER_TB, PEER_NSEL, LANES), jnp.float32),
            pltpu.SemaphoreType.DMA((PEER_SLOTS,)),
        ],
        compiler_params=pltpu.CompilerParams(
            dimension_semantics=("arbitrary",),
            vmem_limit_bytes=48 * 1024 * 1024,
            disable_bounds_checks=True),
        name="peer_gather",
    )(idx_flat, idx_flat, idx_flat, xn, gates, xres, uv)


ROUTE_TM = 256
ROUTE_GROUP = LANES
N_SUBSETS = PEER_HEADS * 2


_POS_SENTINEL = 1 << 20


def _take_top(vals, pos, payload=None):
    m = jnp.max(vals, axis=0, keepdims=True)
    p = jnp.min(jnp.where(vals == m, pos, _POS_SENTINEL), axis=0, keepdims=True)
    hit = pos == p
    picked = p if payload is None else jnp.max(jnp.where(hit, payload, -1), axis=0, keepdims=True)
    return m, picked, jnp.where(hit, -jnp.inf, vals)


def _route_kernel(x_ref, y_ref, wo_ref, nw_ref, wq_ref, keys_ref,
                  xo_ref, xn_ref, idx_ref, g_ref,
                  q_scr, sh_scr, ih_scr, idx_scr, g_scr):
    xa = x_ref[...] + jnp.dot(y_ref[...].astype(jnp.bfloat16), wo_ref[...],
                              preferred_element_type=jnp.float32)
    xo_ref[...] = xa
    xn = xa * lax.rsqrt(jnp.mean(xa * xa, axis=-1, keepdims=True) + EPS) * nw_ref[...]
    xn_ref[...] = xn
    q_scr[...] = jnp.dot(xn.astype(jnp.bfloat16), wq_ref[...],
                         preferred_element_type=jnp.float32).astype(jnp.bfloat16)

    key_row = lax.broadcasted_iota(jnp.int32, (N_KEYS, ROUTE_GROUP), 0)

    def group_body(gi, carry):
        t0 = pl.multiple_of(gi * ROUTE_GROUP, ROUTE_GROUP)

        def head_body(h, carry2):
            kk = PEER_TOPK
            sc = []
            for j in range(2):
                c0 = pl.multiple_of((2 * h + j) * PEER_HALF, PEER_HALF)
                qc = q_scr[pl.ds(t0, ROUTE_GROUP), pl.ds(c0, PEER_HALF)]
                sc.append(lax.dot_general(keys_ref[2 * h + j], qc, (((1,), (1,)), ((), ())),
                                          preferred_element_type=jnp.float32))
            for r in range(kk):
                for j in range(2):
                    m, ki, sc[j] = _take_top(sc[j], key_row)
                    sh_scr[j, r:r + 1, :] = m
                    ih_scr[j, r:r + 1, :] = ki

            row8 = lax.broadcasted_iota(jnp.int32, (SUBLANES, ROUTE_GROUP), 0)
            s1_lo, s1_hi = sh_scr[1, 0:SUBLANES, :], sh_scr[1, SUBLANES:kk, :]
            i1_lo, i1_hi = ih_scr[1, 0:SUBLANES, :], ih_scr[1, SUBLANES:kk, :]
            vals = [sh_scr[0, 0:1, :] + s1_lo, sh_scr[0, 0:1, :] + s1_hi]
            cids = [ih_scr[0, 0:1, :] * N_KEYS + i1_lo, ih_scr[0, 0:1, :] * N_KEYS + i1_hi]
            poss = [row8, row8 + SUBLANES]
            for a in range(1, SUBLANES):
                v = sh_scr[0, a:a + 1, :] + s1_lo
                if kk // (a + 1) < SUBLANES:
                    v = jnp.where(row8 < kk // (a + 1), v, -jnp.inf)
                vals.append(v)
                cids.append(ih_scr[0, a:a + 1, :] * N_KEYS + i1_lo)
                poss.append(row8 + a * kk)
            vals.append(sh_scr[0, SUBLANES:kk, :] + sh_scr[1, 0:1, :])
            cids.append(ih_scr[0, SUBLANES:kk, :] * N_KEYS + ih_scr[1, 0:1, :])
            poss.append((row8 + SUBLANES) * kk)
            cand = jnp.concatenate(vals, axis=0)
            cidx = jnp.concatenate(cids, axis=0)
            cpos = jnp.concatenate(poss, axis=0)
            tops, ids = [], []
            for r in range(kk):
                m, e, cand = _take_top(cand, cpos, cidx)
                tops.append(m)
                ids.append(e)
            top_s = jnp.concatenate(tops, axis=0)
            ex = jnp.exp(top_s - tops[0])
            gate = ex / jnp.sum(ex, axis=0, keepdims=True)
            r0 = pl.multiple_of(h * kk, kk)
            idx_scr[pl.ds(r0, kk), :] = jnp.concatenate(ids, axis=0)
            g_scr[pl.ds(r0, kk), :] = gate
            return carry2
        lax.fori_loop(0, PEER_HEADS, head_body, 0)

        idx_ref[pl.ds(t0, ROUTE_GROUP), :] = idx_scr[...].T
        g_ref[pl.ds(t0, ROUTE_GROUP), :] = g_scr[...].T
        return carry
    lax.fori_loop(0, ROUTE_TM // ROUTE_GROUP, group_body, 0)


def peer_route(x2d, ycat, w_out_bf16, ffn_norm_w, w_q_bf16, keys_bf16):
    m, d = x2d.shape
    tm = ROUTE_TM
    tok = lambda i: (i, 0)
    full2 = lambda i: (0, 0)
    return pl.pallas_call(
        _route_kernel,
        grid=(m // tm,),
        in_specs=[
            pl.BlockSpec((tm, d), tok),
            pl.BlockSpec((tm, MIX_WIDTH), tok),
            pl.BlockSpec((MIX_WIDTH, d), full2),
            pl.BlockSpec((1, d), full2),
            pl.BlockSpec((d, PEER_HEADS * PEER_KEY_DIM), full2),
            pl.BlockSpec((N_SUBSETS, N_KEYS, PEER_HALF), lambda i: (0, 0, 0)),
        ],
        out_specs=[
            pl.BlockSpec((tm, d), tok),
            pl.BlockSpec((tm, d), tok),
            pl.BlockSpec((tm, PEER_NSEL), tok),
            pl.BlockSpec((tm, PEER_NSEL), tok),
        ],
        out_shape=[
            jax.ShapeDtypeStruct((m, d), jnp.float32),
            jax.ShapeDtypeStruct((m, d), jnp.float32),
            jax.ShapeDtypeStruct((m, PEER_NSEL), jnp.int32),
            jax.ShapeDtypeStruct((m, PEER_NSEL), jnp.float32),
        ],
        scratch_shapes=[
            pltpu.VMEM((tm, PEER_HEADS * PEER_KEY_DIM), jnp.bfloat16),
            pltpu.VMEM((2, PEER_TOPK, ROUTE_GROUP), jnp.float32),
            pltpu.VMEM((2, PEER_TOPK, ROUTE_GROUP), jnp.int32),
            pltpu.VMEM((PEER_NSEL, ROUTE_GROUP), jnp.int32),
            pltpu.VMEM((PEER_NSEL, ROUTE_GROUP), jnp.float32),
        ],
        compiler_params=pltpu.CompilerParams(
            dimension_semantics=("arbitrary",),
            vmem_limit_bytes=48 * 1024 * 1024),
        name="peer_route",
    )(x2d, ycat, w_out_bf16, ffn_norm_w.reshape(1, d), w_q_bf16, keys_bf16)


ATT_T = 256


def _diff_attn_kernel(lam_ref, q0_ref, q1_ref, k_ref, v_ref, sw_ref, o_ref,
                      m_scr, l_scr, acc_scr, *, out_scale):
    qb = pl.program_id(2)
    m_scr[...] = jnp.full(m_scr.shape, -jnp.inf, jnp.float32)
    l_scr[...] = jnp.zeros(l_scr.shape, jnp.float32)
    acc_scr[...] = jnp.zeros(acc_scr.shape, jnp.float32)
    row_chunk = lax.broadcasted_iota(jnp.int32, (ATT_T, ATT_T), 0) // CHUNK
    col_chunk = lax.broadcasted_iota(jnp.int32, (ATT_T, ATT_T), 1) // CHUNK
    diag_mask = col_chunk <= row_chunk

    def update(kb, masked):
        k0 = pl.multiple_of(kb * ATT_T, ATT_T)
        ks = k_ref[0, pl.ds(k0, ATT_T), :]
        vs = v_ref[0, pl.ds(k0, ATT_T), :]
        for j, q_ref in enumerate((q0_ref, q1_ref)):
            s = lax.dot_general(q_ref[0], ks, (((1,), (1,)), ((), ())),
                                preferred_element_type=jnp.float32)
            if masked:
                s = jnp.where(diag_mask, s, -jnp.inf)
            m_old = m_scr[j]
            m_new = jnp.maximum(m_old, jnp.max(s, axis=1, keepdims=True))
            alpha = jnp.exp(m_old - m_new)
            p = jnp.exp(s - m_new)
            l_scr[j] = alpha * l_scr[j] + jnp.sum(p, axis=1, keepdims=True)
            acc_scr[j] = alpha * acc_scr[j] + jnp.dot(p.astype(jnp.bfloat16), vs,
                                                      preferred_element_type=jnp.float32)
            m_scr[j] = m_new

    def body(kb, carry):
        update(kb, False)
        return carry
    lax.fori_loop(0, qb, body, 0)
    update(qb, True)

    o = acc_scr[0] / l_scr[0] - lam_ref[0] * (acc_scr[1] / l_scr[1])
    o = o * lax.rsqrt(jnp.mean(o * o, axis=-1, keepdims=True) + EPS) * sw_ref[...]
    o_ref[0] = o * out_scale


def diff_attention_core(q0, q1, k, v, lam, subln_w, out_scale):
    b, s, _ = q0.shape
    qspec = pl.BlockSpec((1, ATT_T, 2 * DIFF_QKDIM), lambda bi, h, i: (bi, i, h))
    kspec = pl.BlockSpec((1, s, 2 * DIFF_QKDIM), lambda bi, h, i: (bi, 0, h))
    vspec = pl.BlockSpec((1, s, DIFF_VDIM), lambda bi, h, i: (bi, 0, h))
    return pl.pallas_call(
        functools.partial(_diff_attn_kernel, out_scale=out_scale),
        grid=(b, DIFF_HEADS, s // ATT_T),
        in_specs=[
            pl.BlockSpec(memory_space=pltpu.SMEM),
            qspec, qspec, kspec, vspec,
            pl.BlockSpec((1, DIFF_VDIM), lambda bi, h, i: (0, 0)),
        ],
        out_specs=pl.BlockSpec((1, ATT_T, DIFF_VDIM), lambda bi, h, i: (bi, i, h)),
        out_shape=jax.ShapeDtypeStruct((b, s, DIFF_WIDTH), jnp.float32),
        scratch_shapes=[
            pltpu.VMEM((2, ATT_T, 1), jnp.float32),
            pltpu.VMEM((2, ATT_T, 1), jnp.float32),
            pltpu.VMEM((2, ATT_T, DIFF_VDIM), jnp.float32),
        ],
        compiler_params=pltpu.CompilerParams(
            dimension_semantics=("arbitrary", "arbitrary", "arbitrary"),
            vmem_limit_bytes=32 * 1024 * 1024),
        name="diff_attention",
    )(lam, q0, q1, k, v, subln_w.reshape(1, DIFF_VDIM))


def rmsnorm(x, w):
    xf = x.astype(jnp.float32)
    y = xf * lax.rsqrt(jnp.mean(xf * xf, axis=-1, keepdims=True) + EPS)
    return (y * w.astype(jnp.float32)).astype(x.dtype)


def rotary_tables(seq):
    inv = jnp.power(ROPE_THETA, -jnp.arange(0, ROPE_DIM, 2, dtype=jnp.float32) / ROPE_DIM)
    ang = jnp.arange(seq, dtype=jnp.float32)[:, None] * inv[None, :]
    return jnp.cos(ang), jnp.sin(ang)


def partial_rotary(x, cos, sin):
    half = ROPE_DIM // 2
    c = cos[:, None, None, :]
    s = sin[:, None, None, :]
    xf = x.astype(jnp.float32)
    x1, x2, xp = xf[..., :half], xf[..., half:ROPE_DIM], xf[..., ROPE_DIM:]
    out = jnp.concatenate([x1 * c - x2 * s, x1 * s + x2 * c, xp], axis=-1)
    return out.astype(x.dtype)


def causal_depthwise_conv(u, w, b):
    out = lax.conv_general_dilated(
        u, w[:, None, :].astype(u.dtype), window_strides=(1,),
        padding=[(CONV_WIDTH - 1, 0)],
        dimension_numbers=('NWC', 'WIO', 'NWC'),
        feature_group_count=u.shape[-1])
    return out + b.astype(u.dtype)


def ssd_branch(z, xbc, dt_raw, conv_w, conv_b, dt_bias, a_log, d_skip, norm_w):
    f32 = jnp.float32
    b, s, _ = z.shape
    nc = s // CHUNK
    G, HG = SSD_GROUPS, SSD_HEADS_PER_GROUP
    xbc = jax.nn.silu(causal_depthwise_conv(xbc, conv_w, conv_b))
    xs, bm, cm = jnp.split(xbc, [SSD_WIDTH, SSD_WIDTH + G * SSD_STATE], axis=-1)
    xs = xs.reshape(b, nc, CHUNK, G, HG, SSD_HEAD_DIM).astype(f32)
    bm = bm.reshape(b, nc, CHUNK, G, SSD_STATE).astype(f32)
    cm = cm.reshape(b, nc, CHUNK, G, SSD_STATE).astype(f32)
    dt = jax.nn.softplus(dt_raw.astype(f32) + dt_bias.astype(f32)).reshape(b, nc, CHUNK, G, HG)
    a = -jnp.exp(a_log.astype(f32)).reshape(G, HG)
    a_cs = jnp.cumsum(dt * a, axis=2)
    xdt = xs * dt[..., None]
    seg = a_cs[:, :, :, None] - a_cs[:, :, None, :]
    causal = jnp.tril(jnp.ones((CHUNK, CHUNK), dtype=bool))[:, :, None, None]
    decay = jnp.exp(jnp.where(causal, seg, -jnp.inf))
    cb = jnp.einsum('bclgn,bcsgn->bclsg', cm, bm)
    y_diag = jnp.einsum('bclsg,bclsgh,bcsghp->bclghp', cb, decay, xdt)
    decay_to_end = jnp.exp(a_cs[:, :, -1:] - a_cs)
    states = jnp.einsum('bclgn,bclgh,bclghp->bcghpn', bm, decay_to_end, xdt)
    chunk_decay = jnp.exp(a_cs[:, :, -1])

    def step(carry, inp):
        st, dc = inp
        return carry * dc[..., None, None] + st, carry

    init = jnp.zeros_like(states[:, 0])
    _, prev = lax.scan(step, init, (jnp.moveaxis(states, 1, 0), jnp.moveaxis(chunk_decay, 1, 0)))
    prev = jnp.moveaxis(prev, 0, 1)
    y_off = jnp.einsum('bclgn,bcghpn,bclgh->bclghp', cm, prev, jnp.exp(a_cs))
    y = y_diag + y_off + xs * d_skip.astype(f32).reshape(G, HG)[:, :, None]
    y = y.reshape(b, s, SSD_WIDTH)
    y = (y * jax.nn.silu(z.astype(f32))).reshape(b, s, G, SSD_WIDTH // G)
    y = y * lax.rsqrt(jnp.mean(y * y, axis=-1, keepdims=True) + EPS)
    y = y.reshape(b, s, SSD_WIDTH) * norm_w.astype(f32)
    return y.astype(z.dtype)


def diff_attention(q, k, v, qn_w, kn_w, lq1, lk1, lq2, lk2, subln_w, lambda_init, cos, sin):
    f32 = jnp.float32
    b, s, _ = q.shape
    q = q.reshape(b, s, DIFF_HEADS, 2, DIFF_QKDIM)
    k = k.reshape(b, s, DIFF_HEADS, 2, DIFF_QKDIM)
    q = partial_rotary(rmsnorm(q, qn_w), cos, sin).astype(f32)
    kf = partial_rotary(rmsnorm(k, kn_w), cos, sin).astype(f32)
    lam = (jnp.exp(jnp.sum(lq1.astype(f32) * lk1.astype(f32)))
           - jnp.exp(jnp.sum(lq2.astype(f32) * lk2.astype(f32))) + lambda_init)
    scale = DIFF_QKDIM ** -0.5
    branch = jnp.arange(2)
    q0 = (q * scale * (branch == 0)[:, None]).reshape(b, s, QK_COLS).astype(jnp.bfloat16)
    q1 = (q * scale * (branch == 1)[:, None]).reshape(b, s, QK_COLS).astype(jnp.bfloat16)
    kb = kf.reshape(b, s, QK_COLS).astype(jnp.bfloat16)
    vb = v.astype(jnp.bfloat16)
    return diff_attention_core(q0, q1, kb, vb, lam.reshape(1), subln_w, 1.0 - lambda_init)


def kernel(x, mix_norm_w, w_in, conv_w, conv_b, dt_bias, a_log, d_skip, ssd_norm_w, q_norm_w, k_norm_w, lambda_q1, lambda_k1, lambda_q2, lambda_k2, subln_w, w_out, ffn_norm_w, peer_w_q, peer_sub_keys, peer_u, peer_v):
    b, s, d = x.shape
    cos, sin = rotary_tables(s)
    layer = 0
    lambda_init = 0.8 - 0.6 * math.exp(-0.3 * layer)
    proj = norm_proj(x.reshape(b * s, d), mix_norm_w[layer],
                     w_in[layer].astype(jnp.bfloat16)).reshape(b, s, IN_COLS)
    z, xbc, dt_raw, q, k, v = jnp.split(proj, SPLITS, axis=-1)
    y_ssd = ssd_branch(z, xbc, dt_raw, conv_w[layer], conv_b[layer], dt_bias[layer],
                       a_log[layer], d_skip[layer], ssd_norm_w[layer])
    y_diff = diff_attention(q, k, v, q_norm_w[layer], k_norm_w[layer], lambda_q1[layer],
                            lambda_k1[layer], lambda_q2[layer], lambda_k2[layer],
                            subln_w[layer], lambda_init, cos, sin)
    ycat = jnp.concatenate([y_ssd, y_diff], axis=-1).reshape(b * s, MIX_WIDTH)
    keys = peer_sub_keys[layer].reshape(N_SUBSETS, N_KEYS, PEER_HALF).astype(jnp.bfloat16)
    x2, xn, idx, gates = peer_route(x.reshape(b * s, d), ycat, w_out[layer].astype(jnp.bfloat16),
                                    ffn_norm_w[layer], peer_w_q[layer].astype(jnp.bfloat16), keys)
    out = peer_gather(xn, idx, gates, x2, pack_uv(peer_u[layer], peer_v[layer]))
    return out.reshape(b, s, d)
```

```python
import functools
import math

import jax
import jax.numpy as jnp
from jax import lax
from jax.experimental import pallas as pl
from jax.experimental.pallas import tpu as pltpu

D_MODEL = 1024
CHUNK = 64
EPS = 1e-6

SSD_WIDTH = D_MODEL // 2
SSD_HEAD_DIM = 64
SSD_HEADS = SSD_WIDTH // SSD_HEAD_DIM
SSD_GROUPS = 2
SSD_HEADS_PER_GROUP = SSD_HEADS // SSD_GROUPS
SSD_STATE = 128
CONV_WIDTH = 4
SSD_CONV_CH = SSD_WIDTH + 2 * SSD_GROUPS * SSD_STATE

DIFF_WIDTH = D_MODEL - SSD_WIDTH
DIFF_HEADS = 4
DIFF_VDIM = DIFF_WIDTH // DIFF_HEADS
DIFF_QKDIM = DIFF_VDIM // 2
ROPE_DIM = DIFF_QKDIM // 4
ROPE_THETA = 500000.0
Q_BLOCK = 128

QK_COLS = DIFF_HEADS * 2 * DIFF_QKDIM
SPLITS = (SSD_WIDTH,
          SSD_WIDTH + SSD_CONV_CH,
          SSD_WIDTH + SSD_CONV_CH + SSD_HEADS,
          SSD_WIDTH + SSD_CONV_CH + SSD_HEADS + QK_COLS,
          SSD_WIDTH + SSD_CONV_CH + SSD_HEADS + 2 * QK_COLS)
IN_COLS = SPLITS[-1] + DIFF_WIDTH
MIX_WIDTH = SSD_WIDTH + DIFF_WIDTH

PEER_HEADS = 8
N_KEYS = 128
N_EXPERTS = N_KEYS * N_KEYS
PEER_KEY_DIM = 256
PEER_HALF = PEER_KEY_DIM // 2
PEER_TOPK = 16
PEER_TOKEN_BLOCK = 128


def _norm_proj_kernel(x_ref, nw_ref, w_ref, o_ref):
    xf = x_ref[...]
    y = xf * lax.rsqrt(jnp.mean(xf * xf, axis=-1, keepdims=True) + EPS)
    y = (y * nw_ref[...]).astype(jnp.bfloat16)
    o_ref[...] = jnp.dot(y, w_ref[...], preferred_element_type=jnp.float32)


def norm_proj(x2d, norm_w, w_bf16, tm=512):
    m, d = x2d.shape
    n = w_bf16.shape[1]
    return pl.pallas_call(
        _norm_proj_kernel,
        grid=(m // tm,),
        in_specs=[
            pl.BlockSpec((tm, d), lambda i: (i, 0)),
            pl.BlockSpec((1, d), lambda i: (0, 0)),
            pl.BlockSpec((d, n), lambda i: (0, 0)),
        ],
        out_specs=pl.BlockSpec((tm, n), lambda i: (i, 0)),
        out_shape=jax.ShapeDtypeStruct((m, n), jnp.float32),
        compiler_params=pltpu.CompilerParams(
            dimension_semantics=("arbitrary",),
            vmem_limit_bytes=56 * 1024 * 1024),
        name="norm_proj",
    )(x2d, norm_w.reshape(1, d), w_bf16)


PEER_NSEL = PEER_HEADS * PEER_TOPK
PEER_TB = 8
PEER_SLOTS = 3
LANES = 128
SUBLANES = 8
_SQRT_HALF = 0.7071067811865476
_U_MASK = 0xFFFF0000


def _peer_gather_kernel(idx_cur, idx_mid, idx_nxt, xn_ref, g_ref, xres_ref, uv_hbm,
                        o_ref, buf, part_scr, wb_scr, sem):
    i = pl.program_id(0)
    nb = pl.num_programs(0)
    slot = i % PEER_SLOTS
    nxt = (i + PEER_SLOTS - 1) % PEER_SLOTS
    d = xn_ref.shape[-1]
    n_groups = PEER_NSEL // SUBLANES
    n_chunks = d // LANES
    half = PEER_TB // 2

    def start_row(idx_ref, s, tt, kk, j):
        e = idx_ref[tt * PEER_NSEL + kk * SUBLANES + j]
        pltpu.make_async_copy(uv_hbm.at[e, pl.ds(0, 1)], buf.at[s, tt, kk, pl.ds(j, 1)],
                              sem.at[s]).start()

    @pl.when(i == 0)
    def _():
        for s, idx_ref in ((0, idx_cur), (1, idx_mid)):
            for tt in range(PEER_TB):
                def body(kk, carry, tt=tt, s=s, idx_ref=idx_ref):
                    for j in range(SUBLANES):
                        start_row(idx_ref, s, tt, kk, j)
                    return carry
                lax.fori_loop(0, n_groups, body, 0)

    pltpu.make_async_copy(buf.at[slot], buf.at[slot], sem.at[slot]).wait()

    for tt in range(half):
        def body(kk, carry, tt=tt):
            r0 = pl.multiple_of(kk * SUBLANES, SUBLANES)
            for n, tok in enumerate((2 * tt, 2 * tt + 1)):
                urows = lax.bitcast_convert_type(buf[slot, tok, kk] & jnp.uint32(_U_MASK),
                                                 jnp.float32)
                prod = urows * xn_ref[tok:tok + 1, :]
                part = prod[:, 0:LANES]
                for c in range(1, n_chunks):
                    part = part + prod[:, c * LANES:(c + 1) * LANES]
                part_scr[tok, pl.ds(r0, SUBLANES), :] = part
                for j in range(n * half, (n + 1) * half):
                    start_row(idx_nxt, nxt, tt, kk, j)
            return carry
        lax.fori_loop(0, n_groups, body, 0)

    lane = lax.broadcasted_iota(jnp.int32, (PEER_NSEL, PEER_TB), 1)
    hm = jnp.zeros((PEER_NSEL, PEER_TB), jnp.float32)
    for tok in range(PEER_TB):
        h = jnp.sum(part_scr[tok], axis=1, keepdims=True)
        hm = jnp.where(lane == tok, h, hm)
    eye = (lax.broadcasted_iota(jnp.int32, (PEER_NSEL, PEER_NSEL), 0)
           == lax.broadcasted_iota(jnp.int32, (PEER_NSEL, PEER_NSEL), 1)).astype(jnp.float32)
    g_t = lax.dot_general(eye, g_ref[...], (((1,), (1,)), ((), ())),
                          precision=lax.Precision.HIGHEST,
                          preferred_element_type=jnp.float32)
    w = g_t * (0.5 * hm * (1.0 + lax.erf(hm * _SQRT_HALF)))
    for tok in range(PEER_TB):
        wb_scr[tok] = jnp.broadcast_to(w[:, tok:tok + 1], (PEER_NSEL, LANES))

    for tt in range(half, PEER_TB):
        toks = (2 * (tt - half), 2 * (tt - half) + 1)

        def body(kk, accs, tt=tt, toks=toks):
            r0 = pl.multiple_of(kk * SUBLANES, SUBLANES)
            out = []
            for n, tok in enumerate(toks):
                wg = wb_scr[tok, pl.ds(r0, SUBLANES), :]
                rows = lax.bitcast_convert_type(buf[slot, tok, kk] << 16, jnp.float32)
                out.append(accs[n] + rows * jnp.concatenate([wg] * n_chunks, axis=1))
                for j in range(n * half, (n + 1) * half):
                    start_row(idx_nxt, nxt, tt, kk, j)
            return tuple(out)
        zero = jnp.zeros((SUBLANES, d), jnp.float32)
        accs = lax.fori_loop(0, n_groups, body, (zero, zero))
        for n, tok in enumerate(toks):
            o_ref[tok:tok + 1, :] = (xres_ref[tok:tok + 1, :]
                                     + jnp.sum(accs[n], axis=0, keepdims=True))

    @pl.when(i == nb - 1)
    def _():
        for s in (nxt, (i + 1) % PEER_SLOTS):
            pltpu.make_async_copy(buf.at[s], buf.at[s], sem.at[s]).wait()


def pack_uv(u, v):
    to_bits = lambda a: lax.bitcast_convert_type(a.astype(jnp.bfloat16), jnp.uint16).astype(jnp.uint32)
    packed = (to_bits(u) << 16) | to_bits(v)
    return jnp.pad(packed[:, None, :], ((0, 0), (0, SUBLANES - 1), (0, 0)))


def peer_gather(xn, idx, gates, xres, uv):
    m, d = xn.shape
    nb = m // PEER_TB
    tok = lambda i: (i, 0)
    idx_flat = idx.reshape(m * PEER_NSEL)
    return pl.pallas_call(
        _peer_gather_kernel,
        grid=(nb,),
        in_specs=[
            pl.BlockSpec((PEER_TB * PEER_NSEL,), lambda i: (i,), memory_space=pltpu.SMEM),
            pl.BlockSpec((PEER_TB * PEER_NSEL,), lambda i: (jnp.minimum(i + 1, nb - 1),),
                         memory_space=pltpu.SMEM),
            pl.BlockSpec((PEER_TB * PEER_NSEL,), lambda i: (jnp.minimum(i + 2, nb - 1),),
                         memory_space=pltpu.SMEM),
            pl.BlockSpec((PEER_TB, d), tok),
            pl.BlockSpec((PEER_TB, PEER_NSEL), tok),
            pl.BlockSpec((PEER_TB, d), tok),
            pl.BlockSpec(memory_space=pl.ANY),
        ],
        out_specs=pl.BlockSpec((PEER_TB, d), tok),
        out_shape=jax.ShapeDtypeStruct((m, d), jnp.float32),
        scratch_shapes=[
            pltpu.VMEM((PEER_SLOTS, PEER_TB, PEER_NSEL // SUBLANES, SUBLANES, d), jnp.uint32),
            pltpu.VMEM((PEER_TB, PEER_NSEL, LANES), jnp.float32),
            pltpu.VMEM((PEER_TB, PEER_NSEL, LANES), jnp.float32),
            pltpu.SemaphoreType.DMA((PEER_SLOTS,)),
        ],
        compiler_params=pltpu.CompilerParams(
            dimension_semantics=("arbitrary",),
            vmem_limit_bytes=48 * 1024 * 1024,
            disable_bounds_checks=True),
        name="peer_gather",
    )(idx_flat, idx_flat, idx_flat, xn, gates, xres, uv)


ROUTE_TM = 256
ROUTE_GROUP = LANES
N_SUBSETS = PEER_HEADS * 2


_POS_SENTINEL = 1 << 20


def _take_top(vals, pos, payload=None):
    m = jnp.max(vals, axis=0, keepdims=True)
    p = jnp.min(jnp.where(vals == m, pos, _POS_SENTINEL), axis=0, keepdims=True)
    hit = pos == p
    picked = p if payload is None else jnp.max(jnp.where(hit, payload, -1), axis=0, keepdims=True)
    return m, picked, jnp.where(hit, -jnp.inf, vals)


def _route_kernel(x_ref, y_ref, wo_ref, nw_ref, wq_ref, keys_ref,
                  xo_ref, xn_ref, idx_ref, g_ref,
                  q_scr, sh_scr, ih_scr, idx_scr, g_scr):
    xa = x_ref[...] + jnp.dot(y_ref[...].astype(jnp.bfloat16), wo_ref[...],
                              preferred_element_type=jnp.float32)
    xo_ref[...] = xa
    xn = xa * lax.rsqrt(jnp.mean(xa * xa, axis=-1, keepdims=True) + EPS) * nw_ref[...]
    xn_ref[...] = xn
    q_scr[...] = jnp.dot(xn.astype(jnp.bfloat16), wq_ref[...],
                         preferred_element_type=jnp.float32).astype(jnp.bfloat16)

    key_row = lax.broadcasted_iota(jnp.int32, (N_KEYS, ROUTE_GROUP), 0)

    def group_body(gi, carry):
        t0 = pl.multiple_of(gi * ROUTE_GROUP, ROUTE_GROUP)

        def head_body(h, carry2):
            kk = PEER_TOPK
            sc = []
            for j in range(2):
                c0 = pl.multiple_of((2 * h + j) * PEER_HALF, PEER_HALF)
                qc = q_scr[pl.ds(t0, ROUTE_GROUP), pl.ds(c0, PEER_HALF)]
                sc.append(lax.dot_general(keys_ref[2 * h + j], qc, (((1,), (1,)), ((), ())),
                                          preferred_element_type=jnp.float32))
            for r in range(kk):
                for j in range(2):
                    m, ki, sc[j] = _take_top(sc[j], key_row)
                    sh_scr[j, r:r + 1, :] = m
                    ih_scr[j, r:r + 1, :] = ki

            row8 = lax.broadcasted_iota(jnp.int32, (SUBLANES, ROUTE_GROUP), 0)
            s1_lo, s1_hi = sh_scr[1, 0:SUBLANES, :], sh_scr[1, SUBLANES:kk, :]
            i1_lo, i1_hi = ih_scr[1, 0:SUBLANES, :], ih_scr[1, SUBLANES:kk, :]
            vals = [sh_scr[0, 0:1, :] + s1_lo, sh_scr[0, 0:1, :] + s1_hi]
            cids = [ih_scr[0, 0:1, :] * N_KEYS + i1_lo, ih_scr[0, 0:1, :] * N_KEYS + i1_hi]
            poss = [row8, row8 + SUBLANES]
            for a in range(1, SUBLANES):
                v = sh_scr[0, a:a + 1, :] + s1_lo
                if kk // (a + 1) < SUBLANES:
                    v = jnp.where(row8 < kk // (a + 1), v, -jnp.inf)
                vals.append(v)
                cids.append(ih_scr[0, a:a + 1, :] * N_KEYS + i1_lo)
                poss.append(row8 + a * kk)
            vals.append(sh_scr[0, SUBLANES:kk, :] + sh_scr[1, 0:1, :])
            cids.append(ih_scr[0, SUBLANES:kk, :] * N_KEYS + ih_scr[1, 0:1, :])
            poss.append((row8 + SUBLANES) * kk)
            cand = jnp.concatenate(vals, axis=0)
            cidx = jnp.concatenate(cids, axis=0)
            cpos = jnp.concatenate(poss, axis=0)
            tops, ids = [], []
            for r in range(kk):
                m, e, cand = _take_top(cand, cpos, cidx)
                tops.append(m)
                ids.append(e)
            top_s = jnp.concatenate(tops, axis=0)
            ex = jnp.exp(top_s - tops[0])
            gate = ex / jnp.sum(ex, axis=0, keepdims=True)
            r0 = pl.multiple_of(h * kk, kk)
            idx_scr[pl.ds(r0, kk), :] = jnp.concatenate(ids, axis=0)
            g_scr[pl.ds(r0, kk), :] = gate
            return carry2
        lax.fori_loop(0, PEER_HEADS, head_body, 0)

        idx_ref[pl.ds(t0, ROUTE_GROUP), :] = idx_scr[...].T
        g_ref[pl.ds(t0, ROUTE_GROUP), :] = g_scr[...].T
        return carry
    lax.fori_loop(0, ROUTE_TM // ROUTE_GROUP, group_body, 0)


def peer_route(x2d, ycat, w_out_bf16, ffn_norm_w, w_q_bf16, keys_bf16):
    m, d = x2d.shape
    tm = ROUTE_TM
    tok = lambda i: (i, 0)
    full2 = lambda i: (0, 0)
    return pl.pallas_call(
        _route_kernel,
        grid=(m // tm,),
        in_specs=[
            pl.BlockSpec((tm, d), tok),
            pl.BlockSpec((tm, MIX_WIDTH), tok),
            pl.BlockSpec((MIX_WIDTH, d), full2),
            pl.BlockSpec((1, d), full2),
            pl.BlockSpec((d, PEER_HEADS * PEER_KEY_DIM), full2),
            pl.BlockSpec((N_SUBSETS, N_KEYS, PEER_HALF), lambda i: (0, 0, 0)),
        ],
        out_specs=[
            pl.BlockSpec((tm, d), tok),
            pl.BlockSpec((tm, d), tok),
            pl.BlockSpec((tm, PEER_NSEL), tok),
            pl.BlockSpec((tm, PEER_NSEL), tok),
        ],
        out_shape=[
            jax.ShapeDtypeStruct((m, d), jnp.float32),
            jax.ShapeDtypeStruct((m, d), jnp.float32),
            jax.ShapeDtypeStruct((m, PEER_NSEL), jnp.int32),
            jax.ShapeDtypeStruct((m, PEER_NSEL), jnp.float32),
        ],
        scratch_shapes=[
            pltpu.VMEM((tm, PEER_HEADS * PEER_KEY_DIM), jnp.bfloat16),
            pltpu.VMEM((2, PEER_TOPK, ROUTE_GROUP), jnp.float32),
            pltpu.VMEM((2, PEER_TOPK, ROUTE_GROUP), jnp.int32),
            pltpu.VMEM((PEER_NSEL, ROUTE_GROUP), jnp.int32),
            pltpu.VMEM((PEER_NSEL, ROUTE_GROUP), jnp.float32),
        ],
        compiler_params=pltpu.CompilerParams(
            dimension_semantics=("arbitrary",),
            vmem_limit_bytes=48 * 1024 * 1024),
        name="peer_route",
    )(x2d, ycat, w_out_bf16, ffn_norm_w.reshape(1, d), w_q_bf16, keys_bf16)


ATT_T = 256


def _diff_attn_kernel(lam_ref, q0_ref, q1_ref, k_ref, v_ref, sw_ref, o_ref,
                      m_scr, l_scr, acc_scr, *, out_scale):
    qb = pl.program_id(2)
    m_scr[...] = jnp.full(m_scr.shape, -jnp.inf, jnp.float32)
    l_scr[...] = jnp.zeros(l_scr.shape, jnp.float32)
    acc_scr[...] = jnp.zeros(acc_scr.shape, jnp.float32)
    row_chunk = lax.broadcasted_iota(jnp.int32, (ATT_T, ATT_T), 0) // CHUNK
    col_chunk = lax.broadcasted_iota(jnp.int32, (ATT_T, ATT_T), 1) // CHUNK
    diag_mask = col_chunk <= row_chunk

    def update(kb, masked):
        k0 = pl.multiple_of(kb * ATT_T, ATT_T)
        ks = k_ref[0, pl.ds(k0, ATT_T), :]
        vs = v_ref[0, pl.ds(k0, ATT_T), :]
        for j, q_ref in enumerate((q0_ref, q1_ref)):
            s = lax.dot_general(q_ref[0], ks, (((1,), (1,)), ((), ())),
                                preferred_element_type=jnp.float32)
            if masked:
                s = jnp.where(diag_mask, s, -jnp.inf)
            m_old = m_scr[j]
            m_new = jnp.maximum(m_old, jnp.max(s, axis=1, keepdims=True))
            alpha = jnp.exp(m_old - m_new)
            p = jnp.exp(s - m_new)
            l_scr[j] = alpha * l_scr[j] + jnp.sum(p, axis=1, keepdims=True)
            acc_scr[j] = alpha * acc_scr[j] + jnp.dot(p.astype(jnp.bfloat16), vs,
                                                      preferred_element_type=jnp.float32)
            m_scr[j] = m_new

    def body(kb, carry):
        update(kb, False)
        return carry
    lax.fori_loop(0, qb, body, 0)
    update(qb, True)

    o = acc_scr[0] / l_scr[0] - lam_ref[0] * (acc_scr[1] / l_scr[1])
    o = o * lax.rsqrt(jnp.mean(o * o, axis=-1, keepdims=True) + EPS) * sw_ref[...]
    o_ref[0] = o * out_scale


def diff_attention_core(q0, q1, k, v, lam, subln_w, out_scale):
    b, s, _ = q0.shape
    qspec = pl.BlockSpec((1, ATT_T, 2 * DIFF_QKDIM), lambda bi, h, i: (bi, i, h))
    kspec = pl.BlockSpec((1, s, 2 * DIFF_QKDIM), lambda bi, h, i: (bi, 0, h))
    vspec = pl.BlockSpec((1, s, DIFF_VDIM), lambda bi, h, i: (bi, 0, h))
    return pl.pallas_call(
        functools.partial(_diff_attn_kernel, out_scale=out_scale),
        grid=(b, DIFF_HEADS, s // ATT_T),
        in_specs=[
            pl.BlockSpec(memory_space=pltpu.SMEM),
            qspec, qspec, kspec, vspec,
            pl.BlockSpec((1, DIFF_VDIM), lambda bi, h, i: (0, 0)),
        ],
        out_specs=pl.BlockSpec((1, ATT_T, DIFF_VDIM), lambda bi, h, i: (bi, i, h)),
        out_shape=jax.ShapeDtypeStruct((b, s, DIFF_WIDTH), jnp.float32),
        scratch_shapes=[
            pltpu.VMEM((2, ATT_T, 1), jnp.float32),
            pltpu.VMEM((2, ATT_T, 1), jnp.float32),
            pltpu.VMEM((2, ATT_T, DIFF_VDIM), jnp.float32),
        ],
        compiler_params=pltpu.CompilerParams(
            dimension_semantics=("arbitrary", "arbitrary", "arbitrary"),
            vmem_limit_bytes=32 * 1024 * 1024),
        name="diff_attention",
    )(lam, q0, q1, k, v, subln_w.reshape(1, DIFF_VDIM))


def rmsnorm(x, w):
    xf = x.astype(jnp.float32)
    y = xf * lax.rsqrt(jnp.mean(xf * xf, axis=-1, keepdims=True) + EPS)
    return (y * w.astype(jnp.float32)).astype(x.dtype)


def rotary_tables(seq):
    inv = jnp.power(ROPE_THETA, -jnp.arange(0, ROPE_DIM, 2, dtype=jnp.float32) / ROPE_DIM)
    ang = jnp.arange(seq, dtype=jnp.float32)[:, None] * inv[None, :]
    return jnp.cos(ang), jnp.sin(ang)


def partial_rotary(x, cos, sin):
    half = ROPE_DIM // 2
    c = cos[:, None, None, :]
    s = sin[:, None, None, :]
    xf = x.astype(jnp.float32)
    x1, x2, xp = xf[..., :half], xf[..., half:ROPE_DIM], xf[..., ROPE_DIM:]
    out = jnp.concatenate([x1 * c - x2 * s, x1 * s + x2 * c, xp], axis=-1)
    return out.astype(x.dtype)


def causal_depthwise_conv(u, w, b):
    out = lax.conv_general_dilated(
        u, w[:, None, :].astype(u.dtype), window_strides=(1,),
        padding=[(CONV_WIDTH - 1, 0)],
        dimension_numbers=('NWC', 'WIO', 'NWC'),
        feature_group_count=u.shape[-1])
    return out + b.astype(u.dtype)


def ssd_branch(z, xbc, dt_raw, conv_w, conv_b, dt_bias, a_log, d_skip, norm_w):
    f32 = jnp.float32
    b, s, _ = z.shape
    nc = s // CHUNK
    G, HG = SSD_GROUPS, SSD_HEADS_PER_GROUP
    xbc = jax.nn.silu(causal_depthwise_conv(xbc, conv_w, conv_b))
    xs, bm, cm = jnp.split(xbc, [SSD_WIDTH, SSD_WIDTH + G * SSD_STATE], axis=-1)
    xs = xs.reshape(b, nc, CHUNK, G, HG, SSD_HEAD_DIM).astype(f32)
    bm = bm.reshape(b, nc, CHUNK, G, SSD_STATE).astype(f32)
    cm = cm.reshape(b, nc, CHUNK, G, SSD_STATE).astype(f32)
    dt = jax.nn.softplus(dt_raw.astype(f32) + dt_bias.astype(f32)).reshape(b, nc, CHUNK, G, HG)
    a = -jnp.exp(a_log.astype(f32)).reshape(G, HG)
    a_cs = jnp.cumsum(dt * a, axis=2)
    xdt = xs * dt[..., None]
    seg = a_cs[:, :, :, None] - a_cs[:, :, None, :]
    causal = jnp.tril(jnp.ones((CHUNK, CHUNK), dtype=bool))[:, :, None, None]
    decay = jnp.exp(jnp.where(causal, seg, -jnp.inf))
    cb = jnp.einsum('bclgn,bcsgn->bclsg', cm, bm)
    y_diag = jnp.einsum('bclsg,bclsgh,bcsghp->bclghp', cb, decay, xdt)
    decay_to_end = jnp.exp(a_cs[:, :, -1:] - a_cs)
    states = jnp.einsum('bclgn,bclgh,bclghp->bcghpn', bm, decay_to_end, xdt)
    chunk_decay = jnp.exp(a_cs[:, :, -1])

    def step(carry, inp):
        st, dc = inp
        return carry * dc[..., None, None] + st, carry

    init = jnp.zeros_like(states[:, 0])
    _, prev = lax.scan(step, init, (jnp.moveaxis(states, 1, 0), jnp.moveaxis(chunk_decay, 1, 0)))
    prev = jnp.moveaxis(prev, 0, 1)
    y_off = jnp.einsum('bclgn,bcghpn,bclgh->bclghp', cm, prev, jnp.exp(a_cs))
    y = y_diag + y_off + xs * d_skip.astype(f32).reshape(G, HG)[:, :, None]
    y = y.reshape(b, s, SSD_WIDTH)
    y = (y * jax.nn.silu(z.astype(f32))).reshape(b, s, G, SSD_WIDTH // G)
    y = y * lax.rsqrt(jnp.mean(y * y, axis=-1, keepdims=True) + EPS)
    y = y.reshape(b, s, SSD_WIDTH) * norm_w.astype(f32)
    return y.astype(z.dtype)


def diff_attention(q, k, v, qn_w, kn_w, lq1, lk1, lq2, lk2, subln_w, lambda_init, cos, sin):
    f32 = jnp.float32
    b, s, _ = q.shape
    q = q.reshape(b, s, DIFF_HEADS, 2, DIFF_QKDIM)
    k = k.reshape(b, s, DIFF_HEADS, 2, DIFF_QKDIM)
    q = partial_rotary(rmsnorm(q, qn_w), cos, sin).astype(f32)
    kf = partial_rotary(rmsnorm(k, kn_w), cos, sin).astype(f32)
    lam = (jnp.exp(jnp.sum(lq1.astype(f32) * lk1.astype(f32)))
           - jnp.exp(jnp.sum(lq2.astype(f32) * lk2.astype(f32))) + lambda_init)
    scale = DIFF_QKDIM ** -0.5
    branch = jnp.arange(2)
    q0 = (q * scale * (branch == 0)[:, None]).reshape(b, s, QK_COLS).astype(jnp.bfloat16)
    q1 = (q * scale * (branch == 1)[:, None]).reshape(b, s, QK_COLS).astype(jnp.bfloat16)
    kb = kf.reshape(b, s, QK_COLS).astype(jnp.bfloat16)
    vb = v.astype(jnp.bfloat16)
    return diff_attention_core(q0, q1, kb, vb, lam.reshape(1), subln_w, 1.0 - lambda_init)


def kernel(x, mix_norm_w, w_in, conv_w, conv_b, dt_bias, a_log, d_skip, ssd_norm_w, q_norm_w, k_norm_w, lambda_q1, lambda_k1, lambda_q2, lambda_k2, subln_w, w_out, ffn_norm_w, peer_w_q, peer_sub_keys, peer_u, peer_v):
    b, s, d = x.shape
    cos, sin = rotary_tables(s)
    layer = 0
    lambda_init = 0.8 - 0.6 * math.exp(-0.3 * layer)
    proj = norm_proj(x.reshape(b * s, d), mix_norm_w[layer],
                     w_in[layer].astype(jnp.bfloat16)).reshape(b, s, IN_COLS)
    z, xbc, dt_raw, q, k, v = jnp.split(proj, SPLITS, axis=-1)
    y_ssd = ssd_branch(z, xbc, dt_raw, conv_w[layer], conv_b[layer], dt_bias[layer],
                       a_log[layer], d_skip[layer], ssd_norm_w[layer])
    y_diff = diff_attention(q, k, v, q_norm_w[layer], k_norm_w[layer], lambda_q1[layer],
                            lambda_k1[layer], lambda_q2[layer], lambda_k2[layer],
                            subln_w[layer], lambda_init, cos, sin)
    ycat = jnp.concatenate([y_ssd, y_diff], axis=-1).reshape(b * s, MIX_WIDTH)
    keys = peer_sub_keys[layer].reshape(N_SUBSETS, N_KEYS, PEER_HALF).astype(jnp.bfloat16)
    x2, xn, idx, gates = peer_route(x.reshape(b * s, d), ycat, w_out[layer].astype(jnp.bfloat16),
                                    ffn_norm_w[layer], peer_w_q[layer].astype(jnp.bfloat16), keys)
    out = peer_gather(xn, idx, gates, x2, pack_uv(peer_u[layer], peer_v[layer]))
    return out.reshape(b, s, d)
```

```python
import functools
import math

import jax
import jax.numpy as jnp
from jax import lax
from jax.experimental import pallas as pl
from jax.experimental.pallas import tpu as pltpu

D_MODEL = 1024
CHUNK = 64
EPS = 1e-6

SSD_WIDTH = D_MODEL // 2
SSD_HEAD_DIM = 64
SSD_HEADS = SSD_WIDTH // SSD_HEAD_DIM
SSD_GROUPS = 2
SSD_HEADS_PER_GROUP = SSD_HEADS // SSD_GROUPS
SSD_STATE = 128
CONV_WIDTH = 4
SSD_CONV_CH = SSD_WIDTH + 2 * SSD_GROUPS * SSD_STATE

DIFF_WIDTH = D_MODEL - SSD_WIDTH
DIFF_HEADS = 4
DIFF_VDIM = DIFF_WIDTH // DIFF_HEADS
DIFF_QKDIM = DIFF_VDIM // 2
ROPE_DIM = DIFF_QKDIM // 4
ROPE_THETA = 500000.0
Q_BLOCK = 128

QK_COLS = DIFF_HEADS * 2 * DIFF_QKDIM
SPLITS = (SSD_WIDTH,
          SSD_WIDTH + SSD_CONV_CH,
          SSD_WIDTH + SSD_CONV_CH + SSD_HEADS,
          SSD_WIDTH + SSD_CONV_CH + SSD_HEADS + QK_COLS,
          SSD_WIDTH + SSD_CONV_CH + SSD_HEADS + 2 * QK_COLS)
IN_COLS = SPLITS[-1] + DIFF_WIDTH
MIX_WIDTH = SSD_WIDTH + DIFF_WIDTH

PEER_HEADS = 8
N_KEYS = 128
N_EXPERTS = N_KEYS * N_KEYS
PEER_KEY_DIM = 256
PEER_HALF = PEER_KEY_DIM // 2
PEER_TOPK = 16
PEER_TOKEN_BLOCK = 128


def _norm_proj_kernel(x_ref, nw_ref, w_ref, o_ref):
    xf = x_ref[...]
    y = xf * lax.rsqrt(jnp.mean(xf * xf, axis=-1, keepdims=True) + EPS)
    y = (y * nw_ref[...]).astype(jnp.bfloat16)
    o_ref[...] = jnp.dot(y, w_ref[...], preferred_element_type=jnp.float32)


def norm_proj(x2d, norm_w, w_bf16, tm=512):
    m, d = x2d.shape
    n = w_bf16.shape[1]
    return pl.pallas_call(
        _norm_proj_kernel,
        grid=(m // tm,),
        in_specs=[
            pl.BlockSpec((tm, d), lambda i: (i, 0)),
            pl.BlockSpec((1, d), lambda i: (0, 0)),
            pl.BlockSpec((d, n), lambda i: (0, 0)),
        ],
        out_specs=pl.BlockSpec((tm, n), lambda i: (i, 0)),
        out_shape=jax.ShapeDtypeStruct((m, n), jnp.float32),
        compiler_params=pltpu.CompilerParams(
            dimension_semantics=("arbitrary",),
            vmem_limit_bytes=56 * 1024 * 1024),
        name="norm_proj",
    )(x2d, norm_w.reshape(1, d), w_bf16)


PEER_NSEL = PEER_HEADS * PEER_TOPK
PEER_TB = 8
PEER_SLOTS = 3
LANES = 128
SUBLANES = 8
_SQRT_HALF = 0.7071067811865476
_U_MASK = 0xFFFF0000


def _peer_gather_kernel(idx_cur, idx_mid, idx_nxt, xn_ref, g_ref, xres_ref, uv_hbm,
                        o_ref, buf, part_scr, wb_scr, sem):
    i = pl.program_id(0)
    nb = pl.num_programs(0)
    slot = i % PEER_SLOTS
    nxt = (i + PEER_SLOTS - 1) % PEER_SLOTS
    d = xn_ref.shape[-1]
    n_groups = PEER_NSEL // SUBLANES
    n_chunks = d // LANES
    half = PEER_TB // 2

    def start_row(idx_ref, s, tt, kk, j):
        e = idx_ref[tt * PEER_NSEL + kk * SUBLANES + j]
        pltpu.make_async_copy(uv_hbm.at[e, pl.ds(0, 1)], buf.at[s, tt, kk, pl.ds(j, 1)],
                              sem.at[s]).start()

    @pl.when(i == 0)
    def _():
        for s, idx_ref in ((0, idx_cur), (1, idx_mid)):
            for tt in range(PEER_TB):
                def body(kk, carry, tt=tt, s=s, idx_ref=idx_ref):
                    for j in range(SUBLANES):
                        start_row(idx_ref, s, tt, kk, j)
                    return carry
                lax.fori_loop(0, n_groups, body, 0)

    pltpu.make_async_copy(buf.at[slot], buf.at[slot], sem.at[slot]).wait()

    for tt in range(half):
        def body(kk, carry, tt=tt):
            r0 = pl.multiple_of(kk * SUBLANES, SUBLANES)
            for n, tok in enumerate((2 * tt, 2 * tt + 1)):
                urows = lax.bitcast_convert_type(buf[slot, tok, kk] & jnp.uint32(_U_MASK),
                                                 jnp.float32)
                prod = urows * xn_ref[tok:tok + 1, :]
                part = prod[:, 0:LANES]
                for c in range(1, n_chunks):
                    part = part + prod[:, c * LANES:(c + 1) * LANES]
                part_scr[tok, pl.ds(r0, SUBLANES), :] = part
                for j in range(n * half, (n + 1) * half):
                    start_row(idx_nxt, nxt, tt, kk, j)
            return carry
        lax.fori_loop(0, n_groups, body, 0)

    lane = lax.broadcasted_iota(jnp.int32, (PEER_NSEL, PEER_TB), 1)
    hm = jnp.zeros((PEER_NSEL, PEER_TB), jnp.float32)
    for tok in range(PEER_TB):
        h = jnp.sum(part_scr[tok], axis=1, keepdims=True)
        hm = jnp.where(lane == tok, h, hm)
    eye = (lax.broadcasted_iota(jnp.int32, (PEER_NSEL, PEER_NSEL), 0)
           == lax.broadcasted_iota(jnp.int32, (PEER_NSEL, PEER_NSEL), 1)).astype(jnp.float32)
    g_t = lax.dot_general(eye, g_ref[...], (((1,), (1,)), ((), ())),
                          precision=lax.Precision.HIGHEST,
                          preferred_element_type=jnp.float32)
    w = g_t * (0.5 * hm * (1.0 + lax.erf(hm * _SQRT_HALF)))
    for tok in range(PEER_TB):
        wb_scr[tok] = jnp.broadcast_to(w[:, tok:tok + 1], (PEER_NSEL, LANES))

    for tt in range(half, PEER_TB):
        toks = (2 * (tt - half), 2 * (tt - half) + 1)

        def body(kk, accs, tt=tt, toks=toks):
            r0 = pl.multiple_of(kk * SUBLANES, SUBLANES)
            out = []
            for n, tok in enumerate(toks):
                wg = wb_scr[tok, pl.ds(r0, SUBLANES), :]
                rows = lax.bitcast_convert_type(buf[slot, tok, kk] << 16, jnp.float32)
                out.append(accs[n] + rows * jnp.concatenate([wg] * n_chunks, axis=1))
                for j in range(n * half, (n + 1) * half):
                    start_row(idx_nxt, nxt, tt, kk, j)
            return tuple(out)
        zero = jnp.zeros((SUBLANES, d), jnp.float32)
        accs = lax.fori_loop(0, n_groups, body, (zero, zero))
        for n, tok in enumerate(toks):
            o_ref[tok:tok + 1, :] = (xres_ref[tok:tok + 1, :]
                                     + jnp.sum(accs[n], axis=0, keepdims=True))

    @pl.when(i == nb - 1)
    def _():
        for s in (nxt, (i + 1) % PEER_SLOTS):
            pltpu.make_async_copy(buf.at[s], buf.at[s], sem.at[s]).wait()


def pack_uv(u, v):
    to_bits = lambda a: lax.bitcast_convert_type(a.astype(jnp.bfloat16), jnp.uint16).astype(jnp.uint32)
    packed = (to_bits(u) << 16) | to_bits(v)
    return jnp.pad(packed[:, None, :], ((0, 0), (0, SUBLANES - 1), (0, 0)))


def peer_gather(xn, idx, gates, xres, uv):
    m, d = xn.shape
    nb = m // PEER_TB
    tok = lambda i: (i, 0)
    idx_flat = idx.reshape(m * PEER_NSEL)
    return pl.pallas_call(
        _peer_gather_kernel,
        grid=(nb,),
        in_specs=[
            pl.BlockSpec((PEER_TB * PEER_NSEL,), lambda i: (i,), memory_space=pltpu.SMEM),
            pl.BlockSpec((PEER_TB * PEER_NSEL,), lambda i: (jnp.minimum(i + 1, nb - 1),),
                         memory_space=pltpu.SMEM),
            pl.BlockSpec((PEER_TB * PEER_NSEL,), lambda i: (jnp.minimum(i + 2, nb - 1),),
                         memory_space=pltpu.SMEM),
            pl.BlockSpec((PEER_TB, d), tok),
            pl.BlockSpec((PEER_TB, PEER_NSEL), tok),
            pl.BlockSpec((PEER_TB, d), tok),
            pl.BlockSpec(memory_space=pl.ANY),
        ],
        out_specs=pl.BlockSpec((PEER_TB, d), tok),
        out_shape=jax.ShapeDtypeStruct((m, d), jnp.float32),
        scratch_shapes=[
            pltpu.VMEM((PEER_SLOTS, PEER_TB, PEER_NSEL // SUBLANES, SUBLANES, d), jnp.uint32),
            pltpu.VMEM((PEER_TB, PEER_NSEL, LANES), jnp.float32),
            pltpu.VMEM((PEER_TB, PEER_NSEL, LANES), jnp.float32),
            pltpu.SemaphoreType.DMA((PEER_SLOTS,)),
        ],
        compiler_params=pltpu.CompilerParams(
            dimension_semantics=("arbitrary",),
            vmem_limit_bytes=48 * 1024 * 1024,
            disable_bounds_checks=True),
        name="peer_gather",
    )(idx_flat, idx_flat, idx_flat, xn, gates, xres, uv)


ROUTE_TM = 256
ROUTE_GROUP = LANES
N_SUBSETS = PEER_HEADS * 2


_POS_SENTINEL = 1 << 20


def _take_top(vals, pos, payload=None):
    m = jnp.max(vals, axis=0, keepdims=True)
    p = jnp.min(jnp.where(vals == m, pos, _POS_SENTINEL), axis=0, keepdims=True)
    hit = pos == p
    picked = p if payload is None else jnp.max(jnp.where(hit, payload, -1), axis=0, keepdims=True)
    return m, picked, jnp.where(hit, -jnp.inf, vals)


def _route_kernel(x_ref, y_ref, wo_ref, nw_ref, wq_ref, keys_ref,
                  xo_ref, xn_ref, idx_ref, g_ref,
                  q_scr, sh_scr, ih_scr, idx_scr, g_scr):
    xa = x_ref[...] + jnp.dot(y_ref[...].astype(jnp.bfloat16), wo_ref[...],
                              preferred_element_type=jnp.float32)
    xo_ref[...] = xa
    xn = xa * lax.rsqrt(jnp.mean(xa * xa, axis=-1, keepdims=True) + EPS) * nw_ref[...]
    xn_ref[...] = xn
    q_scr[...] = jnp.dot(xn.astype(jnp.bfloat16), wq_ref[...],
                         preferred_element_type=jnp.float32).astype(jnp.bfloat16)

    key_row = lax.broadcasted_iota(jnp.int32, (N_KEYS, ROUTE_GROUP), 0)

    def group_body(gi, carry):
        t0 = pl.multiple_of(gi * ROUTE_GROUP, ROUTE_GROUP)

        def head_body(h, carry2):
            kk = PEER_TOPK
            sc = []
            for j in range(2):
                c0 = pl.multiple_of((2 * h + j) * PEER_HALF, PEER_HALF)
                qc = q_scr[pl.ds(t0, ROUTE_GROUP), pl.ds(c0, PEER_HALF)]
                sc.append(lax.dot_general(keys_ref[2 * h + j], qc, (((1,), (1,)), ((), ())),
                                          preferred_element_type=jnp.float32))
            for r in range(kk):
                for j in range(2):
                    m, ki, sc[j] = _take_top(sc[j], key_row)
                    sh_scr[j, r:r + 1, :] = m
                    ih_scr[j, r:r + 1, :] = ki

            row8 = lax.broadcasted_iota(jnp.int32, (SUBLANES, ROUTE_GROUP), 0)
            s1_lo, s1_hi = sh_scr[1, 0:SUBLANES, :], sh_scr[1, SUBLANES:kk, :]
            i1_lo, i1_hi = ih_scr[1, 0:SUBLANES, :], ih_scr[1, SUBLANES:kk, :]
            vals = [sh_scr[0, 0:1, :] + s1_lo, sh_scr[0, 0:1, :] + s1_hi]
            cids = [ih_scr[0, 0:1, :] * N_KEYS + i1_lo, ih_scr[0, 0:1, :] * N_KEYS + i1_hi]
            poss = [row8, row8 + SUBLANES]
            for a in range(1, SUBLANES):
                v = sh_scr[0, a:a + 1, :] + s1_lo
                if kk // (a + 1) < SUBLANES:
                    v = jnp.where(row8 < kk // (a + 1), v, -jnp.inf)
                vals.append(v)
                cids.append(ih_scr[0, a:a + 1, :] * N_KEYS + i1_lo)
                poss.append(row8 + a * kk)
            vals.append(sh_scr[0, SUBLANES:kk, :] + sh_scr[1, 0:1, :])
            cids.append(ih_scr[0, SUBLANES:kk, :] * N_KEYS + ih_scr[1, 0:1, :])
            poss.append((row8 + SUBLANES) * kk)
            cand = jnp.concatenate(vals, axis=0)
            cidx = jnp.concatenate(cids, axis=0)
            cpos = jnp.concatenate(poss, axis=0)
            tops, ids = [], []
            for r in range(kk):
                m, e, cand = _take_top(cand, cpos, cidx)
                tops.append(m)
                ids.append(e)
            top_s = jnp.concatenate(tops, axis=0)
            ex = jnp.exp(top_s - tops[0])
            gate = ex / jnp.sum(ex, axis=0, keepdims=True)
            r0 = pl.multiple_of(h * kk, kk)
            idx_scr[pl.ds(r0, kk), :] = jnp.concatenate(ids, axis=0)
            g_scr[pl.ds(r0, kk), :] = gate
            return carry2
        lax.fori_loop(0, PEER_HEADS, head_body, 0)

        idx_ref[pl.ds(t0, ROUTE_GROUP), :] = idx_scr[...].T
        g_ref[pl.ds(t0, ROUTE_GROUP), :] = g_scr[...].T
        return carry
    lax.fori_loop(0, ROUTE_TM // ROUTE_GROUP, group_body, 0)


def peer_route(x2d, ycat, w_out_bf16, ffn_norm_w, w_q_bf16, keys_bf16):
    m, d = x2d.shape
    tm = ROUTE_TM
    tok = lambda i: (i, 0)
    full2 = lambda i: (0, 0)
    return pl.pallas_call(
        _route_kernel,
        grid=(m // tm,),
        in_specs=[
            pl.BlockSpec((tm, d), tok),
            pl.BlockSpec((tm, MIX_WIDTH), tok),
            pl.BlockSpec((MIX_WIDTH, d), full2),
            pl.BlockSpec((1, d), full2),
            pl.BlockSpec((d, PEER_HEADS * PEER_KEY_DIM), full2),
            pl.BlockSpec((N_SUBSETS, N_KEYS, PEER_HALF), lambda i: (0, 0, 0)),
        ],
        out_specs=[
            pl.BlockSpec((tm, d), tok),
            pl.BlockSpec((tm, d), tok),
            pl.BlockSpec((tm, PEER_NSEL), tok),
            pl.BlockSpec((tm, PEER_NSEL), tok),
        ],
        out_shape=[
            jax.ShapeDtypeStruct((m, d), jnp.float32),
            jax.ShapeDtypeStruct((m, d), jnp.float32),
            jax.ShapeDtypeStruct((m, PEER_NSEL), jnp.int32),
            jax.ShapeDtypeStruct((m, PEER_NSEL), jnp.float32),
        ],
        scratch_shapes=[
            pltpu.VMEM((tm, PEER_HEADS * PEER_KEY_DIM), jnp.bfloat16),
            pltpu.VMEM((2, PEER_TOPK, ROUTE_GROUP), jnp.float32),
            pltpu.VMEM((2, PEER_TOPK, ROUTE_GROUP), jnp.int32),
            pltpu.VMEM((PEER_NSEL, ROUTE_GROUP), jnp.int32),
            pltpu.VMEM((PEER_NSEL, ROUTE_GROUP), jnp.float32),
        ],
        compiler_params=pltpu.CompilerParams(
            dimension_semantics=("arbitrary",),
            vmem_limit_bytes=48 * 1024 * 1024),
        name="peer_route",
    )(x2d, ycat, w_out_bf16, ffn_norm_w.reshape(1, d), w_q_bf16, keys_bf16)


ATT_T = 256


def _diff_attn_kernel(lam_ref, q0_ref, q1_ref, k_ref, v_ref, sw_ref, o_ref,
                      m_scr, l_scr, acc_scr, *, out_scale):
    qb = pl.program_id(2)
    m_scr[...] = jnp.full(m_scr.shape, -jnp.inf, jnp.float32)
    l_scr[...] = jnp.zeros(l_scr.shape, jnp.float32)
    acc_scr[...] = jnp.zeros(acc_scr.shape, jnp.float32)
    row_chunk = lax.broadcasted_iota(jnp.int32, (ATT_T, ATT_T), 0) // CHUNK
    col_chunk = lax.broadcasted_iota(jnp.int32, (ATT_T, ATT_T), 1) // CHUNK
    diag_mask = col_chunk <= row_chunk

    def update(kb, masked):
        k0 = pl.multiple_of(kb * ATT_T, ATT_T)
        ks = k_ref[0, pl.ds(k0, ATT_T), :]
        vs = v_ref[0, pl.ds(k0, ATT_T), :]
        for j, q_ref in enumerate((q0_ref, q1_ref)):
            s = lax.dot_general(q_ref[0], ks, (((1,), (1,)), ((), ())),
                                preferred_element_type=jnp.float32)
            if masked:
                s = jnp.where(diag_mask, s, -jnp.inf)
            m_old = m_scr[j]
            m_new = jnp.maximum(m_old, jnp.max(s, axis=1, keepdims=True))
            alpha = jnp.exp(m_old - m_new)
            p = jnp.exp(s - m_new)
            l_scr[j] = alpha * l_scr[j] + jnp.sum(p, axis=1, keepdims=True)
            acc_scr[j] = alpha * acc_scr[j] + jnp.dot(p.astype(jnp.bfloat16), vs,
                                                      preferred_element_type=jnp.float32)
            m_scr[j] = m_new

    def body(kb, carry):
        update(kb, False)
        return carry
    lax.fori_loop(0, qb, body, 0)
    update(qb, True)

    o = acc_scr[0] / l_scr[0] - lam_ref[0] * (acc_scr[1] / l_scr[1])
    o = o * lax.rsqrt(jnp.mean(o * o, axis=-1, keepdims=True) + EPS) * sw_ref[...]
    o_ref[0] = o * out_scale


def diff_attention_core(q0, q1, k, v, lam, subln_w, out_scale):
    b, s, _ = q0.shape
    qspec = pl.BlockSpec((1, ATT_T, 2 * DIFF_QKDIM), lambda bi, h, i: (bi, i, h))
    kspec = pl.BlockSpec((1, s, 2 * DIFF_QKDIM), lambda bi, h, i: (bi, 0, h))
    vspec = pl.BlockSpec((1, s, DIFF_VDIM), lambda bi, h, i: (bi, 0, h))
    return pl.pallas_call(
        functools.partial(_diff_attn_kernel, out_scale=out_scale),
        grid=(b, DIFF_HEADS, s // ATT_T),
        in_specs=[
            pl.BlockSpec(memory_space=pltpu.SMEM),
            qspec, qspec, kspec, vspec,
            pl.BlockSpec((1, DIFF_VDIM), lambda bi, h, i: (0, 0)),
        ],
        out_specs=pl.BlockSpec((1, ATT_T, DIFF_VDIM), lambda bi, h, i: (bi, i, h)),
        out_shape=jax.ShapeDtypeStruct((b, s, DIFF_WIDTH), jnp.float32),
        scratch_shapes=[
            pltpu.VMEM((2, ATT_T, 1), jnp.float32),
            pltpu.VMEM((2, ATT_T, 1), jnp.float32),
            pltpu.VMEM((2, ATT_T, DIFF_VDIM), jnp.float32),
        ],
        compiler_params=pltpu.CompilerParams(
            dimension_semantics=("arbitrary", "arbitrary", "arbitrary"),
            vmem_limit_bytes=32 * 1024 * 1024),
        name="diff_attention",
    )(lam, q0, q1, k, v, subln_w.reshape(1, DIFF_VDIM))


SSD_T = 256
SSD_BC = 2 * SSD_GROUPS * SSD_STATE
SSD_GW = SSD_WIDTH // SSD_GROUPS
_HI = lax.Precision.HIGHEST


def _ssd_kernel(z_ref, xs_ref, bc_ref, dtc_ref, dtr_ref, cw_ref, cb_ref, dtb_c_ref, dtb_r_ref,
                al_c_ref, al_r_ref, dskip_ref, nw_ref, o_ref, xe_scr, st_scr, y_scr):
    t = SSD_T
    f32, bf16 = jnp.float32, jnp.bfloat16

    @pl.when(pl.program_id(1) == 0)
    def _():
        xe_scr[0:SUBLANES, :] = jnp.zeros((SUBLANES, SSD_CONV_CH), f32)
        st_scr[...] = jnp.zeros(st_scr.shape, f32)

    @pl.when(pl.program_id(1) > 0)
    def _():
        xe_scr[0:SUBLANES, :] = xe_scr[t:t + SUBLANES, :]

    xe_scr[SUBLANES:SUBLANES + t, 0:SSD_WIDTH] = xs_ref[0]
    xe_scr[SUBLANES:SUBLANES + t, SSD_WIDTH:SSD_CONV_CH] = bc_ref[0]
    conv = cb_ref[...]
    for w in range(CONV_WIDTH):
        r0 = SUBLANES - (CONV_WIDTH - 1) + w
        conv = conv + cw_ref[w:w + 1, :] * xe_scr[r0:r0 + t, :]
    xc = jax.nn.silu(conv)
    xs = xc[:, 0:SSD_WIDTH]

    dt_c = jax.nn.softplus(dtc_ref[0] + dtb_c_ref[...])
    dta_c = dt_c * (-jnp.exp(al_c_ref[...]))
    dta_r = jax.nn.softplus(dtr_ref[0] + dtb_r_ref[...]) * (-jnp.exp(al_r_ref[...]))
    row_i = lax.broadcasted_iota(jnp.int32, (t, t), 0)
    col_i = lax.broadcasted_iota(jnp.int32, (t, t), 1)
    same = (row_i // CHUNK) == (col_i // CHUNK)
    lower = (same & (col_i <= row_i)).astype(f32)
    upper = (same & (row_i <= col_i)).astype(f32)
    acs_c = jnp.dot(lower, dta_c, precision=_HI, preferred_element_type=f32)
    acs_r = jnp.dot(dta_r, upper, precision=_HI, preferred_element_type=f32)
    end_c = jnp.dot(same.astype(f32), dta_c, precision=_HI, preferred_element_type=f32)

    expand = (lax.broadcasted_iota(jnp.int32, (SSD_HEADS, SSD_WIDTH), 1) // SSD_HEAD_DIM
              == lax.broadcasted_iota(jnp.int32, (SSD_HEADS, SSD_WIDTH), 0)).astype(f32)
    widen = lambda a: jnp.dot(a, expand, precision=_HI, preferred_element_type=f32)
    xdt = xs * widen(dt_c)
    e_acs = widen(jnp.exp(acs_c))
    x_end = xdt * widen(jnp.exp(end_c - acs_c))
    e_end = widen(jnp.exp(end_c))

    causal = (lax.broadcasted_iota(jnp.int32, (CHUNK, CHUNK), 1)
              <= lax.broadcasted_iota(jnp.int32, (CHUNK, CHUNK), 0))
    head_of_lane = lax.broadcasted_iota(jnp.int32, (CHUNK, SSD_GW), 1) // SSD_HEAD_DIM
    nt = (((1,), (1,)), ((), ()))
    for c in range(t // CHUNK):
        r0 = c * CHUNK
        rows = slice(r0, r0 + CHUNK)
        new_state = []
        for g in range(SSD_GROUPS):
            cols = slice(g * SSD_GW, (g + 1) * SSD_GW)
            bm = xc[rows, SSD_WIDTH + g * SSD_STATE:SSD_WIDTH + (g + 1) * SSD_STATE].astype(bf16)
            cm = xc[rows, SSD_WIDTH + (SSD_GROUPS + g) * SSD_STATE:
                    SSD_WIDTH + (SSD_GROUPS + g + 1) * SSD_STATE].astype(bf16)
            cb = lax.dot_general(cm, bm, nt, preferred_element_type=f32)
            xdt_g = xdt[rows, cols]
            y = jnp.zeros((CHUNK, SSD_GW), f32)
            for hh in range(SSD_HEADS_PER_GROUP):
                h = g * SSD_HEADS_PER_GROUP + hh
                seg = acs_c[rows, h:h + 1] - acs_r[h:h + 1, r0:r0 + CHUNK]
                decay = jnp.exp(jnp.where(causal, seg, -jnp.inf))
                x_h = jnp.where(head_of_lane == hh, xdt_g, 0.0).astype(bf16)
                y = y + jnp.dot((cb * decay).astype(bf16), x_h, preferred_element_type=f32)
            y = y + e_acs[rows, cols] * jnp.dot(cm, st_scr[:, cols].astype(bf16),
                                                preferred_element_type=f32)
            y_scr[rows, cols] = y
            new_state.append(lax.dot_general(bm, x_end[rows, cols].astype(bf16),
                                             (((0,), (0,)), ((), ())),
                                             preferred_element_type=f32))
        st_scr[...] = st_scr[...] * e_end[r0:r0 + 1, :] + jnp.concatenate(new_state, axis=1)

    y = y_scr[...] + xs * dskip_ref[...]
    y = y * jax.nn.silu(z_ref[0])
    for g in range(SSD_GROUPS):
        cols = slice(g * SSD_GW, (g + 1) * SSD_GW)
        yg = y[:, cols]
        yg = yg * lax.rsqrt(jnp.mean(yg * yg, axis=-1, keepdims=True) + EPS)
        o_ref[0, :, cols] = yg * nw_ref[:, cols]


def ssd_branch(proj, dt_raw, conv_w, conv_b, dt_bias, a_log, d_skip, norm_w):
    b, s, _ = proj.shape
    h = SSD_HEADS
    w = SSD_WIDTH
    blk = lambda col: pl.BlockSpec((1, SSD_T, w), lambda bi, j, col=col: (bi, j, col))
    const = lambda shape: pl.BlockSpec(shape, lambda bi, j: (0,) * len(shape))
    return pl.pallas_call(
        _ssd_kernel,
        grid=(b, s // SSD_T),
        in_specs=[
            blk(0), blk(1), blk(2),
            pl.BlockSpec((1, SSD_T, h), lambda bi, j: (bi, j, 0)),
            pl.BlockSpec((1, h, SSD_T), lambda bi, j: (bi, 0, j)),
            const((CONV_WIDTH, SSD_CONV_CH)), const((1, SSD_CONV_CH)),
            const((1, h)), const((h, 1)), const((1, h)), const((h, 1)),
            const((1, w)), const((1, w)),
        ],
        out_specs=pl.BlockSpec((1, SSD_T, w), lambda bi, j: (bi, j, 0)),
        out_shape=jax.ShapeDtypeStruct((b, s, w), jnp.float32),
        scratch_shapes=[
            pltpu.VMEM((SUBLANES + SSD_T, SSD_CONV_CH), jnp.float32),
            pltpu.VMEM((SSD_STATE, w), jnp.float32),
            pltpu.VMEM((SSD_T, w), jnp.float32),
        ],
        compiler_params=pltpu.CompilerParams(
            dimension_semantics=("arbitrary", "arbitrary"),
            vmem_limit_bytes=32 * 1024 * 1024),
        name="ssd_branch",
    )(proj, proj, proj, dt_raw, jnp.swapaxes(dt_raw, 1, 2), conv_w, conv_b.reshape(1, -1),
      dt_bias.reshape(1, h), dt_bias.reshape(h, 1), a_log.reshape(1, h), a_log.reshape(h, 1),
      jnp.repeat(d_skip, SSD_HEAD_DIM).reshape(1, w), norm_w.reshape(1, w))


def rmsnorm(x, w):
    xf = x.astype(jnp.float32)
    y = xf * lax.rsqrt(jnp.mean(xf * xf, axis=-1, keepdims=True) + EPS)
    return (y * w.astype(jnp.float32)).astype(x.dtype)


def rotary_tables(seq):
    inv = jnp.power(ROPE_THETA, -jnp.arange(0, ROPE_DIM, 2, dtype=jnp.float32) / ROPE_DIM)
    ang = jnp.arange(seq, dtype=jnp.float32)[:, None] * inv[None, :]
    return jnp.cos(ang), jnp.sin(ang)


def partial_rotary(x, cos, sin):
    half = ROPE_DIM // 2
    c = cos[:, None, None, :]
    s = sin[:, None, None, :]
    xf = x.astype(jnp.float32)
    x1, x2, xp = xf[..., :half], xf[..., half:ROPE_DIM], xf[..., ROPE_DIM:]
    out = jnp.concatenate([x1 * c - x2 * s, x1 * s + x2 * c, xp], axis=-1)
    return out.astype(x.dtype)


def diff_attention(q, k, v, qn_w, kn_w, lq1, lk1, lq2, lk2, subln_w, lambda_init, cos, sin):
    f32 = jnp.float32
    b, s, _ = q.shape
    q = q.reshape(b, s, DIFF_HEADS, 2, DIFF_QKDIM)
    k = k.reshape(b, s, DIFF_HEADS, 2, DIFF_QKDIM)
    q = partial_rotary(rmsnorm(q, qn_w), cos, sin).astype(f32)
    kf = partial_rotary(rmsnorm(k, kn_w), cos, sin).astype(f32)
    lam = (jnp.exp(jnp.sum(lq1.astype(f32) * lk1.astype(f32)))
           - jnp.exp(jnp.sum(lq2.astype(f32) * lk2.astype(f32))) + lambda_init)
    scale = DIFF_QKDIM ** -0.5
    branch = jnp.arange(2)
    q0 = (q * scale * (branch == 0)[:, None]).reshape(b, s, QK_COLS).astype(jnp.bfloat16)
    q1 = (q * scale * (branch == 1)[:, None]).reshape(b, s, QK_COLS).astype(jnp.bfloat16)
    kb = kf.reshape(b, s, QK_COLS).astype(jnp.bfloat16)
    vb = v.astype(jnp.bfloat16)
    return diff_attention_core(q0, q1, kb, vb, lam.reshape(1), subln_w, 1.0 - lambda_init)


def kernel(x, mix_norm_w, w_in, conv_w, conv_b, dt_bias, a_log, d_skip, ssd_norm_w, q_norm_w, k_norm_w, lambda_q1, lambda_k1, lambda_q2, lambda_k2, subln_w, w_out, ffn_norm_w, peer_w_q, peer_sub_keys, peer_u, peer_v):
    b, s, d = x.shape
    cos, sin = rotary_tables(s)
    layer = 0
    lambda_init = 0.8 - 0.6 * math.exp(-0.3 * layer)
    proj = norm_proj(x.reshape(b * s, d), mix_norm_w[layer],
                     w_in[layer].astype(jnp.bfloat16)).reshape(b, s, IN_COLS)
    dt_raw = proj[..., SPLITS[1]:SPLITS[2]]
    q = proj[..., SPLITS[2]:SPLITS[3]]
    k = proj[..., SPLITS[3]:SPLITS[4]]
    v = proj[..., SPLITS[4]:]
    y_ssd = ssd_branch(proj, dt_raw, conv_w[layer], conv_b[layer], dt_bias[layer],
                       a_log[layer], d_skip[layer], ssd_norm_w[layer])
    y_diff = diff_attention(q, k, v, q_norm_w[layer], k_norm_w[layer], lambda_q1[layer],
                            lambda_k1[layer], lambda_q2[layer], lambda_k2[layer],
                            subln_w[layer], lambda_init, cos, sin)
    ycat = jnp.concatenate([y_ssd, y_diff], axis=-1).reshape(b * s, MIX_WIDTH)
    keys = peer_sub_keys[layer].reshape(N_SUBSETS, N_KEYS, PEER_HALF).astype(jnp.bfloat16)
    x2, xn, idx, gates = peer_route(x.reshape(b * s, d), ycat, w_out[layer].astype(jnp.bfloat16),
                                    ffn_norm_w[layer], peer_w_q[layer].astype(jnp.bfloat16), keys)
    out = peer_gather(xn, idx, gates, x2, pack_uv(peer_u[layer], peer_v[layer]))
    return out.reshape(b, s, d)
```

```python
import functools
import math

import jax
import jax.numpy as jnp
from jax import lax
from jax.experimental import pallas as pl
from jax.experimental.pallas import tpu as pltpu

D_MODEL = 1024
CHUNK = 64
EPS = 1e-6

SSD_WIDTH = D_MODEL // 2
SSD_HEAD_DIM = 64
SSD_HEADS = SSD_WIDTH // SSD_HEAD_DIM
SSD_GROUPS = 2
SSD_HEADS_PER_GROUP = SSD_HEADS // SSD_GROUPS
SSD_STATE = 128
CONV_WIDTH = 4
SSD_CONV_CH = SSD_WIDTH + 2 * SSD_GROUPS * SSD_STATE

DIFF_WIDTH = D_MODEL - SSD_WIDTH
DIFF_HEADS = 4
DIFF_VDIM = DIFF_WIDTH // DIFF_HEADS
DIFF_QKDIM = DIFF_VDIM // 2
ROPE_DIM = DIFF_QKDIM // 4
ROPE_THETA = 500000.0
Q_BLOCK = 128

QK_COLS = DIFF_HEADS * 2 * DIFF_QKDIM
SPLITS = (SSD_WIDTH,
          SSD_WIDTH + SSD_CONV_CH,
          SSD_WIDTH + SSD_CONV_CH + SSD_HEADS,
          SSD_WIDTH + SSD_CONV_CH + SSD_HEADS + QK_COLS,
          SSD_WIDTH + SSD_CONV_CH + SSD_HEADS + 2 * QK_COLS)
IN_COLS = SPLITS[-1] + DIFF_WIDTH
MIX_WIDTH = SSD_WIDTH + DIFF_WIDTH

PEER_HEADS = 8
N_KEYS = 128
N_EXPERTS = N_KEYS * N_KEYS
PEER_KEY_DIM = 256
PEER_HALF = PEER_KEY_DIM // 2
PEER_TOPK = 16
PEER_TOKEN_BLOCK = 128


def _norm_proj_kernel(x_ref, nw_ref, w_ref, o_ref):
    xf = x_ref[...]
    y = xf * lax.rsqrt(jnp.mean(xf * xf, axis=-1, keepdims=True) + EPS)
    y = (y * nw_ref[...]).astype(jnp.bfloat16)
    o_ref[...] = jnp.dot(y, w_ref[...], preferred_element_type=jnp.float32)


def norm_proj(x2d, norm_w, w_bf16, tm=512):
    m, d = x2d.shape
    n = w_bf16.shape[1]
    return pl.pallas_call(
        _norm_proj_kernel,
        grid=(m // tm,),
        in_specs=[
            pl.BlockSpec((tm, d), lambda i: (i, 0)),
            pl.BlockSpec((1, d), lambda i: (0, 0)),
            pl.BlockSpec((d, n), lambda i: (0, 0)),
        ],
        out_specs=pl.BlockSpec((tm, n), lambda i: (i, 0)),
        out_shape=jax.ShapeDtypeStruct((m, n), jnp.float32),
        compiler_params=pltpu.CompilerParams(
            dimension_semantics=("arbitrary",),
            vmem_limit_bytes=56 * 1024 * 1024),
        name="norm_proj",
    )(x2d, norm_w.reshape(1, d), w_bf16)


PEER_NSEL = PEER_HEADS * PEER_TOPK
PEER_TB = 8
PEER_SLOTS = 3
LANES = 128
SUBLANES = 8
_SQRT_HALF = 0.7071067811865476
_U_MASK = 0xFFFF0000


def _peer_gather_kernel(idx_cur, idx_mid, idx_nxt, xn_ref, g_ref, xres_ref, uv_hbm,
                        o_ref, buf, part_scr, wb_scr, sem):
    i = pl.program_id(0)
    nb = pl.num_programs(0)
    slot = i % PEER_SLOTS
    nxt = (i + PEER_SLOTS - 1) % PEER_SLOTS
    d = xn_ref.shape[-1]
    n_groups = PEER_NSEL // SUBLANES
    n_chunks = d // LANES
    half = PEER_TB // 2

    def start_row(idx_ref, s, tt, kk, j):
        e = idx_ref[tt * PEER_NSEL + kk * SUBLANES + j]
        pltpu.make_async_copy(uv_hbm.at[e, pl.ds(0, 1)], buf.at[s, tt, kk, pl.ds(j, 1)],
                              sem.at[s]).start()

    @pl.when(i == 0)
    def _():
        for s, idx_ref in ((0, idx_cur), (1, idx_mid)):
            for tt in range(PEER_TB):
                def body(kk, carry, tt=tt, s=s, idx_ref=idx_ref):
                    for j in range(SUBLANES):
                        start_row(idx_ref, s, tt, kk, j)
                    return carry
                lax.fori_loop(0, n_groups, body, 0)

    pltpu.make_async_copy(buf.at[slot], buf.at[slot], sem.at[slot]).wait()

    for tt in range(half):
        def body(kk, carry, tt=tt):
            r0 = pl.multiple_of(kk * SUBLANES, SUBLANES)
            for n, tok in enumerate((2 * tt, 2 * tt + 1)):
                urows = lax.bitcast_convert_type(buf[slot, tok, kk] & jnp.uint32(_U_MASK),
                                                 jnp.float32)
                prod = urows * xn_ref[tok:tok + 1, :]
                part = prod[:, 0:LANES]
                for c in range(1, n_chunks):
                    part = part + prod[:, c * LANES:(c + 1) * LANES]
                part_scr[tok, pl.ds(r0, SUBLANES), :] = part
                for j in range(n * half, (n + 1) * half):
                    start_row(idx_nxt, nxt, tt, kk, j)
            return carry
        lax.fori_loop(0, n_groups, body, 0)

    lane = lax.broadcasted_iota(jnp.int32, (PEER_NSEL, PEER_TB), 1)
    hm = jnp.zeros((PEER_NSEL, PEER_TB), jnp.float32)
    for tok in range(PEER_TB):
        h = jnp.sum(part_scr[tok], axis=1, keepdims=True)
        hm = jnp.where(lane == tok, h, hm)
    eye = (lax.broadcasted_iota(jnp.int32, (PEER_NSEL, PEER_NSEL), 0)
           == lax.broadcasted_iota(jnp.int32, (PEER_NSEL, PEER_NSEL), 1)).astype(jnp.float32)
    g_t = lax.dot_general(eye, g_ref[...], (((1,), (1,)), ((), ())),
                          precision=lax.Precision.HIGHEST,
                          preferred_element_type=jnp.float32)
    w = g_t * (0.5 * hm * (1.0 + lax.erf(hm * _SQRT_HALF)))
    for tok in range(PEER_TB):
        wb_scr[tok] = jnp.broadcast_to(w[:, tok:tok + 1], (PEER_NSEL, LANES))

    for tt in range(half, PEER_TB):
        toks = (2 * (tt - half), 2 * (tt - half) + 1)

        def body(kk, accs, tt=tt, toks=toks):
            r0 = pl.multiple_of(kk * SUBLANES, SUBLANES)
            out = []
            for n, tok in enumerate(toks):
                wg = wb_scr[tok, pl.ds(r0, SUBLANES), :]
                rows = lax.bitcast_convert_type(buf[slot, tok, kk] << 16, jnp.float32)
                out.append(accs[n] + rows * jnp.concatenate([wg] * n_chunks, axis=1))
                for j in range(n * half, (n + 1) * half):
                    start_row(idx_nxt, nxt, tt, kk, j)
            return tuple(out)
        zero = jnp.zeros((SUBLANES, d), jnp.float32)
        accs = lax.fori_loop(0, n_groups, body, (zero, zero))
        for n, tok in enumerate(toks):
            o_ref[tok:tok + 1, :] = (xres_ref[tok:tok + 1, :]
                                     + jnp.sum(accs[n], axis=0, keepdims=True))

    @pl.when(i == nb - 1)
    def _():
        for s in (nxt, (i + 1) % PEER_SLOTS):
            pltpu.make_async_copy(buf.at[s], buf.at[s], sem.at[s]).wait()


def pack_uv(u, v):
    to_bits = lambda a: lax.bitcast_convert_type(a.astype(jnp.bfloat16), jnp.uint16).astype(jnp.uint32)
    packed = (to_bits(u) << 16) | to_bits(v)
    return jnp.pad(packed[:, None, :], ((0, 0), (0, SUBLANES - 1), (0, 0)))


def peer_gather(xn, idx, gates, xres, uv):
    m, d = xn.shape
    nb = m // PEER_TB
    tok = lambda i: (i, 0)
    idx_flat = idx.reshape(m * PEER_NSEL)
    return pl.pallas_call(
        _peer_gather_kernel,
        grid=(nb,),
        in_specs=[
            pl.BlockSpec((PEER_TB * PEER_NSEL,), lambda i: (i,), memory_space=pltpu.SMEM),
            pl.BlockSpec((PEER_TB * PEER_NSEL,), lambda i: (jnp.minimum(i + 1, nb - 1),),
                         memory_space=pltpu.SMEM),
            pl.BlockSpec((PEER_TB * PEER_NSEL,), lambda i: (jnp.minimum(i + 2, nb - 1),),
                         memory_space=pltpu.SMEM),
            pl.BlockSpec((PEER_TB, d), tok),
            pl.BlockSpec((PEER_TB, PEER_NSEL), tok),
            pl.BlockSpec((PEER_TB, d), tok),
            pl.BlockSpec(memory_space=pl.ANY),
        ],
        out_specs=pl.BlockSpec((PEER_TB, d), tok),
        out_shape=jax.ShapeDtypeStruct((m, d), jnp.float32),
        scratch_shapes=[
            pltpu.VMEM((PEER_SLOTS, PEER_TB, PEER_NSEL // SUBLANES, SUBLANES, d), jnp.uint32),
            pltpu.VMEM((PEER_TB, PEER_NSEL, LANES), jnp.float32),
            pltpu.VMEM((PEER_TB, PEER_NSEL, LANES), jnp.float32),
            pltpu.SemaphoreType.DMA((PEER_SLOTS,)),
        ],
        compiler_params=pltpu.CompilerParams(
            dimension_semantics=("arbitrary",),
            vmem_limit_bytes=48 * 1024 * 1024,
            disable_bounds_checks=True),
        name="peer_gather",
    )(idx_flat, idx_flat, idx_flat, xn, gates, xres, uv)


ROUTE_TM = 256
ROUTE_GROUP = LANES
N_SUBSETS = PEER_HEADS * 2


_POS_SENTINEL = 1 << 20


def _take_top(vals, pos, payload=None):
    m = jnp.max(vals, axis=0, keepdims=True)
    p = jnp.min(jnp.where(vals == m, pos, _POS_SENTINEL), axis=0, keepdims=True)
    hit = pos == p
    picked = p if payload is None else jnp.max(jnp.where(hit, payload, -1), axis=0, keepdims=True)
    return m, picked, jnp.where(hit, -jnp.inf, vals)


def _route_kernel(x_ref, y_ref, wo_ref, nw_ref, wq_ref, keys_ref,
                  xo_ref, xn_ref, idx_ref, g_ref,
                  q_scr, sh_scr, ih_scr, idx_scr, g_scr):
    xa = x_ref[...] + jnp.dot(y_ref[...].astype(jnp.bfloat16), wo_ref[...],
                              preferred_element_type=jnp.float32)
    xo_ref[...] = xa
    xn = xa * lax.rsqrt(jnp.mean(xa * xa, axis=-1, keepdims=True) + EPS) * nw_ref[...]
    xn_ref[...] = xn
    q_scr[...] = jnp.dot(xn.astype(jnp.bfloat16), wq_ref[...],
                         preferred_element_type=jnp.float32).astype(jnp.bfloat16)

    key_row = lax.broadcasted_iota(jnp.int32, (N_KEYS, ROUTE_GROUP), 0)

    def group_body(gi, carry):
        t0 = pl.multiple_of(gi * ROUTE_GROUP, ROUTE_GROUP)

        def head_body(h, carry2):
            kk = PEER_TOPK
            sc = []
            for j in range(2):
                c0 = pl.multiple_of((2 * h + j) * PEER_HALF, PEER_HALF)
                qc = q_scr[pl.ds(t0, ROUTE_GROUP), pl.ds(c0, PEER_HALF)]
                sc.append(lax.dot_general(keys_ref[2 * h + j], qc, (((1,), (1,)), ((), ())),
                                          preferred_element_type=jnp.float32))
            for r in range(kk):
                for j in range(2):
                    m, ki, sc[j] = _take_top(sc[j], key_row)
                    sh_scr[j, r:r + 1, :] = m
                    ih_scr[j, r:r + 1, :] = ki

            row8 = lax.broadcasted_iota(jnp.int32, (SUBLANES, ROUTE_GROUP), 0)
            s1_lo, s1_hi = sh_scr[1, 0:SUBLANES, :], sh_scr[1, SUBLANES:kk, :]
            i1_lo, i1_hi = ih_scr[1, 0:SUBLANES, :], ih_scr[1, SUBLANES:kk, :]
            vals = [sh_scr[0, 0:1, :] + s1_lo, sh_scr[0, 0:1, :] + s1_hi]
            cids = [ih_scr[0, 0:1, :] * N_KEYS + i1_lo, ih_scr[0, 0:1, :] * N_KEYS + i1_hi]
            poss = [row8, row8 + SUBLANES]
            for a in range(1, SUBLANES):
                v = sh_scr[0, a:a + 1, :] + s1_lo
                if kk // (a + 1) < SUBLANES:
                    v = jnp.where(row8 < kk // (a + 1), v, -jnp.inf)
                vals.append(v)
                cids.append(ih_scr[0, a:a + 1, :] * N_KEYS + i1_lo)
                poss.append(row8 + a * kk)
            vals.append(sh_scr[0, SUBLANES:kk, :] + sh_scr[1, 0:1, :])
            cids.append(ih_scr[0, SUBLANES:kk, :] * N_KEYS + ih_scr[1, 0:1, :])
            poss.append((row8 + SUBLANES) * kk)
            cand = jnp.concatenate(vals, axis=0)
            cidx = jnp.concatenate(cids, axis=0)
            cpos = jnp.concatenate(poss, axis=0)
            tops, ids = [], []
            for r in range(kk):
                m, e, cand = _take_top(cand, cpos, cidx)
                tops.append(m)
                ids.append(e)
            top_s = jnp.concatenate(tops, axis=0)
            ex = jnp.exp(top_s - tops[0])
            gate = ex / jnp.sum(ex, axis=0, keepdims=True)
            r0 = pl.multiple_of(h * kk, kk)
            idx_scr[pl.ds(r0, kk), :] = jnp.concatenate(ids, axis=0)
            g_scr[pl.ds(r0, kk), :] = gate
            return carry2
        lax.fori_loop(0, PEER_HEADS, head_body, 0)

        idx_ref[pl.ds(t0, ROUTE_GROUP), :] = idx_scr[...].T
        g_ref[pl.ds(t0, ROUTE_GROUP), :] = g_scr[...].T
        return carry
    lax.fori_loop(0, ROUTE_TM // ROUTE_GROUP, group_body, 0)


def peer_route(x2d, ycat, w_out_bf16, ffn_norm_w, w_q_bf16, keys_bf16):
    m, d = x2d.shape
    tm = ROUTE_TM
    tok = lambda i: (i, 0)
    full2 = lambda i: (0, 0)
    return pl.pallas_call(
        _route_kernel,
        grid=(m // tm,),
        in_specs=[
            pl.BlockSpec((tm, d), tok),
            pl.BlockSpec((tm, MIX_WIDTH), tok),
            pl.BlockSpec((MIX_WIDTH, d), full2),
            pl.BlockSpec((1, d), full2),
            pl.BlockSpec((d, PEER_HEADS * PEER_KEY_DIM), full2),
            pl.BlockSpec((N_SUBSETS, N_KEYS, PEER_HALF), lambda i: (0, 0, 0)),
        ],
        out_specs=[
            pl.BlockSpec((tm, d), tok),
            pl.BlockSpec((tm, d), tok),
            pl.BlockSpec((tm, PEER_NSEL), tok),
            pl.BlockSpec((tm, PEER_NSEL), tok),
        ],
        out_shape=[
            jax.ShapeDtypeStruct((m, d), jnp.float32),
            jax.ShapeDtypeStruct((m, d), jnp.float32),
            jax.ShapeDtypeStruct((m, PEER_NSEL), jnp.int32),
            jax.ShapeDtypeStruct((m, PEER_NSEL), jnp.float32),
        ],
        scratch_shapes=[
            pltpu.VMEM((tm, PEER_HEADS * PEER_KEY_DIM), jnp.bfloat16),
            pltpu.VMEM((2, PEER_TOPK, ROUTE_GROUP), jnp.float32),
            pltpu.VMEM((2, PEER_TOPK, ROUTE_GROUP), jnp.int32),
            pltpu.VMEM((PEER_NSEL, ROUTE_GROUP), jnp.int32),
            pltpu.VMEM((PEER_NSEL, ROUTE_GROUP), jnp.float32),
        ],
        compiler_params=pltpu.CompilerParams(
            dimension_semantics=("arbitrary",),
            vmem_limit_bytes=48 * 1024 * 1024),
        name="peer_route",
    )(x2d, ycat, w_out_bf16, ffn_norm_w.reshape(1, d), w_q_bf16, keys_bf16)


ATT_T = 256


def _diff_attn_kernel(lam_ref, q0_ref, q1_ref, k_ref, v_ref, sw_ref, o_ref,
                      m_scr, l_scr, acc_scr, *, out_scale):
    qb = pl.program_id(2)
    m_scr[...] = jnp.full(m_scr.shape, -jnp.inf, jnp.float32)
    l_scr[...] = jnp.zeros(l_scr.shape, jnp.float32)
    acc_scr[...] = jnp.zeros(acc_scr.shape, jnp.float32)
    key_chunk = lax.broadcasted_iota(jnp.int32, (ATT_T, ATT_T), 0) // CHUNK
    qry_chunk = lax.broadcasted_iota(jnp.int32, (ATT_T, ATT_T), 1) // CHUNK
    diag_mask = key_chunk <= qry_chunk

    def update(kb, masked):
        k0 = pl.multiple_of(kb * ATT_T, ATT_T)
        ks = k_ref[0, pl.ds(k0, ATT_T), :]
        vt = v_ref[0, :, pl.ds(k0, ATT_T)]
        scores = [jnp.dot(ks, q_ref[0], preferred_element_type=jnp.float32)
                  for q_ref in (q0_ref, q1_ref)]
        m_old = [m_scr[0], m_scr[1]]
        l_old = [l_scr[0], l_scr[1]]
        acc_old = [acc_scr[0], acc_scr[1]]
        m_new, alpha, probs = [], [], []
        for j in range(2):
            s = jnp.where(diag_mask, scores[j], -jnp.inf) if masked else scores[j]
            m_new.append(jnp.maximum(m_old[j], jnp.max(s, axis=0, keepdims=True)))
            alpha.append(jnp.exp(m_old[j] - m_new[j]))
            probs.append(jnp.exp(s - m_new[j]))
        pv = [jnp.dot(vt, probs[j].astype(jnp.bfloat16), preferred_element_type=jnp.float32)
              for j in range(2)]
        for j in range(2):
            m_scr[j] = m_new[j]
            l_scr[j] = alpha[j] * l_old[j] + jnp.sum(probs[j], axis=0, keepdims=True)
            acc_scr[j] = alpha[j] * acc_old[j] + pv[j]

    def body(kb, carry):
        update(kb, False)
        return carry
    lax.fori_loop(0, qb, body, 0)
    update(qb, True)

    o = acc_scr[0] / l_scr[0] - lam_ref[0] * (acc_scr[1] / l_scr[1])
    o = o * lax.rsqrt(jnp.mean(o * o, axis=0, keepdims=True) + EPS) * sw_ref[...]
    o_ref[0] = (o * out_scale).T


def diff_attention_core(q0, q1, k, v, lam, subln_w, out_scale):
    b, s, _ = k.shape
    qspec = pl.BlockSpec((1, 2 * DIFF_QKDIM, ATT_T), lambda bi, h, i: (bi, h, i))
    kspec = pl.BlockSpec((1, s, 2 * DIFF_QKDIM), lambda bi, h, i: (bi, 0, h))
    vspec = pl.BlockSpec((1, DIFF_VDIM, s), lambda bi, h, i: (bi, h, 0))
    return pl.pallas_call(
        functools.partial(_diff_attn_kernel, out_scale=out_scale),
        grid=(b, DIFF_HEADS, s // ATT_T),
        in_specs=[
            pl.BlockSpec(memory_space=pltpu.SMEM),
            qspec, qspec, kspec, vspec,
            pl.BlockSpec((DIFF_VDIM, 1), lambda bi, h, i: (0, 0)),
        ],
        out_specs=pl.BlockSpec((1, ATT_T, DIFF_VDIM), lambda bi, h, i: (bi, i, h)),
        out_shape=jax.ShapeDtypeStruct((b, s, DIFF_WIDTH), jnp.float32),
        scratch_shapes=[
            pltpu.VMEM((2, 1, ATT_T), jnp.float32),
            pltpu.VMEM((2, 1, ATT_T), jnp.float32),
            pltpu.VMEM((2, DIFF_VDIM, ATT_T), jnp.float32),
        ],
        compiler_params=pltpu.CompilerParams(
            dimension_semantics=("arbitrary", "arbitrary", "arbitrary"),
            vmem_limit_bytes=32 * 1024 * 1024),
        name="diff_attention",
    )(lam, q0, q1, k, v, subln_w.reshape(DIFF_VDIM, 1))


SSD_T = 256
SSD_BC = 2 * SSD_GROUPS * SSD_STATE
SSD_GW = SSD_WIDTH // SSD_GROUPS
_HI = lax.Precision.HIGHEST


def _ssd_kernel(z_ref, xs_ref, bc_ref, dtc_ref, dtr_ref, cw_ref, cb_ref, dtb_c_ref, dtb_r_ref,
                al_c_ref, al_r_ref, dskip_ref, nw_ref, o_ref, xe_scr, st_scr, y_scr):
    t = SSD_T
    f32, bf16 = jnp.float32, jnp.bfloat16

    @pl.when(pl.program_id(1) == 0)
    def _():
        xe_scr[0:SUBLANES, :] = jnp.zeros((SUBLANES, SSD_CONV_CH), f32)
        st_scr[...] = jnp.zeros(st_scr.shape, f32)

    @pl.when(pl.program_id(1) > 0)
    def _():
        xe_scr[0:SUBLANES, :] = xe_scr[t:t + SUBLANES, :]

    xe_scr[SUBLANES:SUBLANES + t, 0:SSD_WIDTH] = xs_ref[0]
    xe_scr[SUBLANES:SUBLANES + t, SSD_WIDTH:SSD_CONV_CH] = bc_ref[0]
    conv = cb_ref[...]
    for w in range(CONV_WIDTH):
        r0 = SUBLANES - (CONV_WIDTH - 1) + w
        conv = conv + cw_ref[w:w + 1, :] * xe_scr[r0:r0 + t, :]
    xc = jax.nn.silu(conv)
    xs = xc[:, 0:SSD_WIDTH]

    dt_c = jax.nn.softplus(dtc_ref[0] + dtb_c_ref[...])
    dta_c = dt_c * (-jnp.exp(al_c_ref[...]))
    dta_r = jax.nn.softplus(dtr_ref[0] + dtb_r_ref[...]) * (-jnp.exp(al_r_ref[...]))
    row_i = lax.broadcasted_iota(jnp.int32, (t, t), 0)
    col_i = lax.broadcasted_iota(jnp.int32, (t, t), 1)
    same = (row_i // CHUNK) == (col_i // CHUNK)
    lower = (same & (col_i <= row_i)).astype(f32)
    upper = (same & (row_i <= col_i)).astype(f32)
    acs_c = jnp.dot(lower, dta_c, precision=_HI, preferred_element_type=f32)
    acs_r = jnp.dot(dta_r, upper, precision=_HI, preferred_element_type=f32)
    end_c = jnp.dot(same.astype(f32), dta_c, precision=_HI, preferred_element_type=f32)

    expand = (lax.broadcasted_iota(jnp.int32, (SSD_HEADS, SSD_WIDTH), 1) // SSD_HEAD_DIM
              == lax.broadcasted_iota(jnp.int32, (SSD_HEADS, SSD_WIDTH), 0)).astype(f32)
    widen = lambda a: jnp.dot(a, expand, precision=_HI, preferred_element_type=f32)
    xdt = xs * widen(dt_c)
    e_acs = widen(jnp.exp(acs_c))
    x_end = xdt * widen(jnp.exp(end_c - acs_c))
    e_end = widen(jnp.exp(end_c))

    causal = (lax.broadcasted_iota(jnp.int32, (CHUNK, CHUNK), 1)
              <= lax.broadcasted_iota(jnp.int32, (CHUNK, CHUNK), 0))
    head_of_lane = lax.broadcasted_iota(jnp.int32, (CHUNK, SSD_GW), 1) // SSD_HEAD_DIM
    nt = (((1,), (1,)), ((), ()))
    for c in range(t // CHUNK):
        r0 = c * CHUNK
        rows = slice(r0, r0 + CHUNK)
        new_state = []
        for g in range(SSD_GROUPS):
            cols = slice(g * SSD_GW, (g + 1) * SSD_GW)
            bm = xc[rows, SSD_WIDTH + g * SSD_STATE:SSD_WIDTH + (g + 1) * SSD_STATE].astype(bf16)
            cm = xc[rows, SSD_WIDTH + (SSD_GROUPS + g) * SSD_STATE:
                    SSD_WIDTH + (SSD_GROUPS + g + 1) * SSD_STATE].astype(bf16)
            cb = lax.dot_general(cm, bm, nt, preferred_element_type=f32)
            xdt_g = xdt[rows, cols]
            y = jnp.zeros((CHUNK, SSD_GW), f32)
            for hh in range(SSD_HEADS_PER_GROUP):
                h = g * SSD_HEADS_PER_GROUP + hh
                seg = acs_c[rows, h:h + 1] - acs_r[h:h + 1, r0:r0 + CHUNK]
                decay = jnp.exp(jnp.where(causal, seg, -jnp.inf))
                x_h = jnp.where(head_of_lane == hh, xdt_g, 0.0).astype(bf16)
                y = y + jnp.dot((cb * decay).astype(bf16), x_h, preferred_element_type=f32)
            y = y + e_acs[rows, cols] * jnp.dot(cm, st_scr[:, cols].astype(bf16),
                                                preferred_element_type=f32)
            y_scr[rows, cols] = y
            new_state.append(lax.dot_general(bm, x_end[rows, cols].astype(bf16),
                                             (((0,), (0,)), ((), ())),
                                             preferred_element_type=f32))
        st_scr[...] = st_scr[...] * e_end[r0:r0 + 1, :] + jnp.concatenate(new_state, axis=1)

    y = y_scr[...] + xs * dskip_ref[...]
    y = y * jax.nn.silu(z_ref[0])
    for g in range(SSD_GROUPS):
        cols = slice(g * SSD_GW, (g + 1) * SSD_GW)
        yg = y[:, cols]
        yg = yg * lax.rsqrt(jnp.mean(yg * yg, axis=-1, keepdims=True) + EPS)
        o_ref[0, :, cols] = yg * nw_ref[:, cols]


def ssd_branch(proj, dt_raw, conv_w, conv_b, dt_bias, a_log, d_skip, norm_w):
    b, s, _ = proj.shape
    h = SSD_HEADS
    w = SSD_WIDTH
    blk = lambda col: pl.BlockSpec((1, SSD_T, w), lambda bi, j, col=col: (bi, j, col))
    const = lambda shape: pl.BlockSpec(shape, lambda bi, j: (0,) * len(shape))
    return pl.pallas_call(
        _ssd_kernel,
        grid=(b, s // SSD_T),
        in_specs=[
            blk(0), blk(1), blk(2),
            pl.BlockSpec((1, SSD_T, h), lambda bi, j: (bi, j, 0)),
            pl.BlockSpec((1, h, SSD_T), lambda bi, j: (bi, 0, j)),
            const((CONV_WIDTH, SSD_CONV_CH)), const((1, SSD_CONV_CH)),
            const((1, h)), const((h, 1)), const((1, h)), const((h, 1)),
            const((1, w)), const((1, w)),
        ],
        out_specs=pl.BlockSpec((1, SSD_T, w), lambda bi, j: (bi, j, 0)),
        out_shape=jax.ShapeDtypeStruct((b, s, w), jnp.float32),
        scratch_shapes=[
            pltpu.VMEM((SUBLANES + SSD_T, SSD_CONV_CH), jnp.float32),
            pltpu.VMEM((SSD_STATE, w), jnp.float32),
            pltpu.VMEM((SSD_T, w), jnp.float32),
        ],
        compiler_params=pltpu.CompilerParams(
            dimension_semantics=("arbitrary", "arbitrary"),
            vmem_limit_bytes=32 * 1024 * 1024),
        name="ssd_branch",
    )(proj, proj, proj, dt_raw, jnp.swapaxes(dt_raw, 1, 2), conv_w, conv_b.reshape(1, -1),
      dt_bias.reshape(1, h), dt_bias.reshape(h, 1), a_log.reshape(1, h), a_log.reshape(h, 1),
      jnp.repeat(d_skip, SSD_HEAD_DIM).reshape(1, w), norm_w.reshape(1, w))


def rmsnorm(x, w):
    xf = x.astype(jnp.float32)
    y = xf * lax.rsqrt(jnp.mean(xf * xf, axis=-1, keepdims=True) + EPS)
    return (y * w.astype(jnp.float32)).astype(x.dtype)


def rotary_tables(seq):
    inv = jnp.power(ROPE_THETA, -jnp.arange(0, ROPE_DIM, 2, dtype=jnp.float32) / ROPE_DIM)
    ang = jnp.arange(seq, dtype=jnp.float32)[:, None] * inv[None, :]
    return jnp.cos(ang), jnp.sin(ang)


def partial_rotary(x, cos, sin):
    half = ROPE_DIM // 2
    c = cos[:, None, None, :]
    s = sin[:, None, None, :]
    xf = x.astype(jnp.float32)
    x1, x2, xp = xf[..., :half], xf[..., half:ROPE_DIM], xf[..., ROPE_DIM:]
    out = jnp.concatenate([x1 * c - x2 * s, x1 * s + x2 * c, xp], axis=-1)
    return out.astype(x.dtype)


def diff_attention(q, k, v, qn_w, kn_w, lq1, lk1, lq2, lk2, subln_w, lambda_init, cos, sin):
    f32 = jnp.float32
    b, s, _ = q.shape
    q = q.reshape(b, s, DIFF_HEADS, 2, DIFF_QKDIM)
    k = k.reshape(b, s, DIFF_HEADS, 2, DIFF_QKDIM)
    q = partial_rotary(rmsnorm(q, qn_w), cos, sin).astype(f32)
    kf = partial_rotary(rmsnorm(k, kn_w), cos, sin).astype(f32)
    lam = (jnp.exp(jnp.sum(lq1.astype(f32) * lk1.astype(f32)))
           - jnp.exp(jnp.sum(lq2.astype(f32) * lk2.astype(f32))) + lambda_init)
    scale = DIFF_QKDIM ** -0.5
    branch = jnp.arange(2)
    to_cols = lambda a: jnp.swapaxes(a.reshape(b, s, -1).astype(jnp.bfloat16), 1, 2)
    q0 = to_cols(q * scale * (branch == 0)[:, None])
    q1 = to_cols(q * scale * (branch == 1)[:, None])
    kb = kf.reshape(b, s, QK_COLS).astype(jnp.bfloat16)
    vb = to_cols(v)
    return diff_attention_core(q0, q1, kb, vb, lam.reshape(1), subln_w, 1.0 - lambda_init)


def kernel(x, mix_norm_w, w_in, conv_w, conv_b, dt_bias, a_log, d_skip, ssd_norm_w, q_norm_w, k_norm_w, lambda_q1, lambda_k1, lambda_q2, lambda_k2, subln_w, w_out, ffn_norm_w, peer_w_q, peer_sub_keys, peer_u, peer_v):
    b, s, d = x.shape
    cos, sin = rotary_tables(s)
    layer = 0
    lambda_init = 0.8 - 0.6 * math.exp(-0.3 * layer)
    proj = norm_proj(x.reshape(b * s, d), mix_norm_w[layer],
                     w_in[layer].astype(jnp.bfloat16)).reshape(b, s, IN_COLS)
    dt_raw = proj[..., SPLITS[1]:SPLITS[2]]
    q = proj[..., SPLITS[2]:SPLITS[3]]
    k = proj[..., SPLITS[3]:SPLITS[4]]
    v = proj[..., SPLITS[4]:]
    y_ssd = ssd_branch(proj, dt_raw, conv_w[layer], conv_b[layer], dt_bias[layer],
                       a_log[layer], d_skip[layer], ssd_norm_w[layer])
    y_diff = diff_attention(q, k, v, q_norm_w[layer], k_norm_w[layer], lambda_q1[layer],
                            lambda_k1[layer], lambda_q2[layer], lambda_k2[layer],
                            subln_w[layer], lambda_init, cos, sin)
    ycat = jnp.concatenate([y_ssd, y_diff], axis=-1).reshape(b * s, MIX_WIDTH)
    keys = peer_sub_keys[layer].reshape(N_SUBSETS, N_KEYS, PEER_HALF).astype(jnp.bfloat16)
    x2, xn, idx, gates = peer_route(x.reshape(b * s, d), ycat, w_out[layer].astype(jnp.bfloat16),
                                    ffn_norm_w[layer], peer_w_q[layer].astype(jnp.bfloat16), keys)
    out = peer_gather(xn, idx, gates, x2, pack_uv(peer_u[layer], peer_v[layer]))
    return out.reshape(b, s, d)
```

```python
import functools
import math

import jax
import jax.numpy as jnp
from jax import lax
from jax.experimental import pallas as pl
from jax.experimental.pallas import tpu as pltpu

D_MODEL = 1024
CHUNK = 64
EPS = 1e-6

SSD_WIDTH = D_MODEL // 2
SSD_HEAD_DIM = 64
SSD_HEADS = SSD_WIDTH // SSD_HEAD_DIM
SSD_GROUPS = 2
SSD_HEADS_PER_GROUP = SSD_HEADS // SSD_GROUPS
SSD_STATE = 128
CONV_WIDTH = 4
SSD_CONV_CH = SSD_WIDTH + 2 * SSD_GROUPS * SSD_STATE

DIFF_WIDTH = D_MODEL - SSD_WIDTH
DIFF_HEADS = 4
DIFF_VDIM = DIFF_WIDTH // DIFF_HEADS
DIFF_QKDIM = DIFF_VDIM // 2
ROPE_DIM = DIFF_QKDIM // 4
ROPE_THETA = 500000.0
Q_BLOCK = 128

QK_COLS = DIFF_HEADS * 2 * DIFF_QKDIM
SPLITS = (SSD_WIDTH,
          SSD_WIDTH + SSD_CONV_CH,
          SSD_WIDTH + SSD_CONV_CH + SSD_HEADS,
          SSD_WIDTH + SSD_CONV_CH + SSD_HEADS + QK_COLS,
          SSD_WIDTH + SSD_CONV_CH + SSD_HEADS + 2 * QK_COLS)
IN_COLS = SPLITS[-1] + DIFF_WIDTH
MIX_WIDTH = SSD_WIDTH + DIFF_WIDTH

PEER_HEADS = 8
N_KEYS = 128
N_EXPERTS = N_KEYS * N_KEYS
PEER_KEY_DIM = 256
PEER_HALF = PEER_KEY_DIM // 2
PEER_TOPK = 16
PEER_TOKEN_BLOCK = 128


def _norm_proj_kernel(x_ref, nw_ref, w_ref, o_ref):
    xf = x_ref[...]
    y = xf * lax.rsqrt(jnp.mean(xf * xf, axis=-1, keepdims=True) + EPS)
    y = (y * nw_ref[...]).astype(jnp.bfloat16)
    o_ref[...] = jnp.dot(y, w_ref[...], preferred_element_type=jnp.float32)


def norm_proj(x2d, norm_w, w_bf16, tm=512):
    m, d = x2d.shape
    n = w_bf16.shape[1]
    return pl.pallas_call(
        _norm_proj_kernel,
        grid=(m // tm,),
        in_specs=[
            pl.BlockSpec((tm, d), lambda i: (i, 0)),
            pl.BlockSpec((1, d), lambda i: (0, 0)),
            pl.BlockSpec((d, n), lambda i: (0, 0)),
        ],
        out_specs=pl.BlockSpec((tm, n), lambda i: (i, 0)),
        out_shape=jax.ShapeDtypeStruct((m, n), jnp.float32),
        compiler_params=pltpu.CompilerParams(
            dimension_semantics=("arbitrary",),
            vmem_limit_bytes=56 * 1024 * 1024),
        name="norm_proj",
    )(x2d, norm_w.reshape(1, d), w_bf16)


PEER_NSEL = PEER_HEADS * PEER_TOPK
PEER_TB = 8
PEER_SLOTS = 3
LANES = 128
SUBLANES = 8
_SQRT_HALF = 0.7071067811865476
_U_MASK = 0xFFFF0000


def _peer_gather_kernel(idx_cur, idx_mid, idx_nxt, xn_ref, g_ref, xres_ref, uv_hbm,
                        o_ref, buf, part_scr, wb_scr, sem):
    i = pl.program_id(0)
    nb = pl.num_programs(0)
    slot = i % PEER_SLOTS
    nxt = (i + PEER_SLOTS - 1) % PEER_SLOTS
    d = xn_ref.shape[-1]
    n_groups = PEER_NSEL // SUBLANES
    n_chunks = d // LANES
    half = PEER_TB // 2

    def start_row(idx_ref, s, tt, kk, j):
        e = idx_ref[tt * PEER_NSEL + kk * SUBLANES + j]
        pltpu.make_async_copy(uv_hbm.at[e, pl.ds(0, 1)], buf.at[s, tt, kk, pl.ds(j, 1)],
                              sem.at[s]).start()

    @pl.when(i == 0)
    def _():
        for s, idx_ref in ((0, idx_cur), (1, idx_mid)):
            for tt in range(PEER_TB):
                def body(kk, carry, tt=tt, s=s, idx_ref=idx_ref):
                    for j in range(SUBLANES):
                        start_row(idx_ref, s, tt, kk, j)
                    return carry
                lax.fori_loop(0, n_groups, body, 0)

    pltpu.make_async_copy(buf.at[slot], buf.at[slot], sem.at[slot]).wait()

    for tt in range(half):
        def body(kk, carry, tt=tt):
            r0 = pl.multiple_of(kk * SUBLANES, SUBLANES)
            for n, tok in enumerate((2 * tt, 2 * tt + 1)):
                urows = lax.bitcast_convert_type(buf[slot, tok, kk] & jnp.uint32(_U_MASK),
                                                 jnp.float32)
                prod = urows * xn_ref[tok:tok + 1, :]
                part = prod[:, 0:LANES]
                for c in range(1, n_chunks):
                    part = part + prod[:, c * LANES:(c + 1) * LANES]
                part_scr[tok, pl.ds(r0, SUBLANES), :] = part
                for j in range(n * half, (n + 1) * half):
                    start_row(idx_nxt, nxt, tt, kk, j)
            return carry
        lax.fori_loop(0, n_groups, body, 0)

    for kk in range(n_groups):
        for j in range(SUBLANES):
            start_row(idx_nxt, nxt, PEER_TB - 1, kk, j)

    lane = lax.broadcasted_iota(jnp.int32, (PEER_NSEL, PEER_TB), 1)
    hm = jnp.zeros((PEER_NSEL, PEER_TB), jnp.float32)
    for tok in range(PEER_TB):
        h = jnp.sum(part_scr[tok], axis=1, keepdims=True)
        hm = jnp.where(lane == tok, h, hm)
    eye = (lax.broadcasted_iota(jnp.int32, (PEER_NSEL, PEER_NSEL), 0)
           == lax.broadcasted_iota(jnp.int32, (PEER_NSEL, PEER_NSEL), 1)).astype(jnp.float32)
    g_t = lax.dot_general(eye, g_ref[...], (((1,), (1,)), ((), ())),
                          precision=lax.Precision.HIGHEST,
                          preferred_element_type=jnp.float32)
    w = g_t * (0.5 * hm * (1.0 + lax.erf(hm * _SQRT_HALF)))
    for tok in range(PEER_TB):
        wb_scr[tok] = jnp.broadcast_to(w[:, tok:tok + 1], (PEER_NSEL, LANES))

    for tt, toks in ((half, (0, 1, 2)), (half + 1, (3, 4, 5)), (half + 2, (6, 7))):
        def body(kk, accs, tt=tt, toks=toks):
            r0 = pl.multiple_of(kk * SUBLANES, SUBLANES)
            out = []
            for n, tok in enumerate(toks):
                wg = wb_scr[tok, pl.ds(r0, SUBLANES), :]
                rows = lax.bitcast_convert_type(buf[slot, tok, kk] << 16, jnp.float32)
                out.append(accs[n] + rows * jnp.concatenate([wg] * n_chunks, axis=1))
                for j in range(n * SUBLANES // len(toks), (n + 1) * SUBLANES // len(toks)):
                    start_row(idx_nxt, nxt, tt, kk, j)
            return tuple(out)
        zero = jnp.zeros((SUBLANES, d), jnp.float32)
        accs = lax.fori_loop(0, n_groups, body, (zero,) * len(toks))
        for n, tok in enumerate(toks):
            o_ref[tok:tok + 1, :] = (xres_ref[tok:tok + 1, :]
                                     + jnp.sum(accs[n], axis=0, keepdims=True))

    @pl.when(i == nb - 1)
    def _():
        for s in (nxt, (i + 1) % PEER_SLOTS):
            pltpu.make_async_copy(buf.at[s], buf.at[s], sem.at[s]).wait()


def pack_uv(u, v):
    to_bits = lambda a: lax.bitcast_convert_type(a.astype(jnp.bfloat16), jnp.uint16).astype(jnp.uint32)
    packed = (to_bits(u) << 16) | to_bits(v)
    return jnp.pad(packed[:, None, :], ((0, 0), (0, SUBLANES - 1), (0, 0)))


def peer_gather(xn, idx, gates, xres, uv):
    m, d = xn.shape
    nb = m // PEER_TB
    tok = lambda i: (i, 0)
    idx_flat = idx.reshape(m * PEER_NSEL)
    return pl.pallas_call(
        _peer_gather_kernel,
        grid=(nb,),
        in_specs=[
            pl.BlockSpec((PEER_TB * PEER_NSEL,), lambda i: (i,), memory_space=pltpu.SMEM),
            pl.BlockSpec((PEER_TB * PEER_NSEL,), lambda i: (jnp.minimum(i + 1, nb - 1),),
                         memory_space=pltpu.SMEM),
            pl.BlockSpec((PEER_TB * PEER_NSEL,), lambda i: (jnp.minimum(i + 2, nb - 1),),
                         memory_space=pltpu.SMEM),
            pl.BlockSpec((PEER_TB, d), tok),
            pl.BlockSpec((PEER_TB, PEER_NSEL), tok),
            pl.BlockSpec((PEER_TB, d), tok),
            pl.BlockSpec(memory_space=pl.ANY),
        ],
        out_specs=pl.BlockSpec((PEER_TB, d), tok),
        out_shape=jax.ShapeDtypeStruct((m, d), jnp.float32),
        scratch_shapes=[
            pltpu.VMEM((PEER_SLOTS, PEER_TB, PEER_NSEL // SUBLANES, SUBLANES, d), jnp.uint32),
            pltpu.VMEM((PEER_TB, PEER_NSEL, LANES), jnp.float32),
            pltpu.VMEM((PEER_TB, PEER_NSEL, LANES), jnp.float32),
            pltpu.SemaphoreType.DMA((PEER_SLOTS,)),
        ],
        compiler_params=pltpu.CompilerParams(
            dimension_semantics=("arbitrary",),
            vmem_limit_bytes=48 * 1024 * 1024,
            disable_bounds_checks=True),
        name="peer_gather",
    )(idx_flat, idx_flat, idx_flat, xn, gates, xres, uv)


ROUTE_TM = 256
ROUTE_GROUP = LANES
N_SUBSETS = PEER_HEADS * 2


_POS_SENTINEL = 1 << 20


def _take_top(vals, pos, payload=None):
    m = jnp.max(vals, axis=0, keepdims=True)
    p = jnp.min(jnp.where(vals == m, pos, _POS_SENTINEL), axis=0, keepdims=True)
    hit = pos == p
    picked = p if payload is None else jnp.max(jnp.where(hit, payload, -1), axis=0, keepdims=True)
    return m, picked, jnp.where(hit, -jnp.inf, vals)


def _route_kernel(x_ref, y_ref, wo_ref, nw_ref, wq_ref, keys_ref,
                  xo_ref, xn_ref, idx_ref, g_ref,
                  q_scr, sh_scr, ih_scr, idx_scr, g_scr):
    xa = x_ref[...] + jnp.dot(y_ref[...].astype(jnp.bfloat16), wo_ref[...],
                              preferred_element_type=jnp.float32)
    xo_ref[...] = xa
    xn = xa * lax.rsqrt(jnp.mean(xa * xa, axis=-1, keepdims=True) + EPS) * nw_ref[...]
    xn_ref[...] = xn
    q_scr[...] = jnp.dot(xn.astype(jnp.bfloat16), wq_ref[...],
                         preferred_element_type=jnp.float32).astype(jnp.bfloat16)

    key_row = lax.broadcasted_iota(jnp.int32, (N_KEYS, ROUTE_GROUP), 0)

    def group_body(gi, carry):
        t0 = pl.multiple_of(gi * ROUTE_GROUP, ROUTE_GROUP)

        def head_body(h, carry2):
            kk = PEER_TOPK
            sc = []
            for j in range(2):
                c0 = pl.multiple_of((2 * h + j) * PEER_HALF, PEER_HALF)
                qc = q_scr[pl.ds(t0, ROUTE_GROUP), pl.ds(c0, PEER_HALF)]
                sc.append(lax.dot_general(keys_ref[2 * h + j], qc, (((1,), (1,)), ((), ())),
                                          preferred_element_type=jnp.float32))
            for r in range(kk):
                for j in range(2):
                    m, ki, sc[j] = _take_top(sc[j], key_row)
                    sh_scr[j, r:r + 1, :] = m
                    ih_scr[j, r:r + 1, :] = ki

            row8 = lax.broadcasted_iota(jnp.int32, (SUBLANES, ROUTE_GROUP), 0)
            s1_lo, s1_hi = sh_scr[1, 0:SUBLANES, :], sh_scr[1, SUBLANES:kk, :]
            i1_lo, i1_hi = ih_scr[1, 0:SUBLANES, :], ih_scr[1, SUBLANES:kk, :]
            vals = [sh_scr[0, 0:1, :] + s1_lo, sh_scr[0, 0:1, :] + s1_hi]
            cids = [ih_scr[0, 0:1, :] * N_KEYS + i1_lo, ih_scr[0, 0:1, :] * N_KEYS + i1_hi]
            poss = [row8, row8 + SUBLANES]
            for a in range(1, SUBLANES):
                v = sh_scr[0, a:a + 1, :] + s1_lo
                if kk // (a + 1) < SUBLANES:
                    v = jnp.where(row8 < kk // (a + 1), v, -jnp.inf)
                vals.append(v)
                cids.append(ih_scr[0, a:a + 1, :] * N_KEYS + i1_lo)
                poss.append(row8 + a * kk)
            vals.append(sh_scr[0, SUBLANES:kk, :] + sh_scr[1, 0:1, :])
            cids.append(ih_scr[0, SUBLANES:kk, :] * N_KEYS + ih_scr[1, 0:1, :])
            poss.append((row8 + SUBLANES) * kk)
            cand = jnp.concatenate(vals, axis=0)
            cidx = jnp.concatenate(cids, axis=0)
            cpos = jnp.concatenate(poss, axis=0)
            tops, ids = [], []
            for r in range(kk):
                m, e, cand = _take_top(cand, cpos, cidx)
                tops.append(m)
                ids.append(e)
            top_s = jnp.concatenate(tops, axis=0)
            ex = jnp.exp(top_s - tops[0])
            gate = ex / jnp.sum(ex, axis=0, keepdims=True)
            r0 = pl.multiple_of(h * kk, kk)
            idx_scr[pl.ds(r0, kk), :] = jnp.concatenate(ids, axis=0)
            g_scr[pl.ds(r0, kk), :] = gate
            return carry2
        lax.fori_loop(0, PEER_HEADS, head_body, 0)

        idx_ref[pl.ds(t0, ROUTE_GROUP), :] = idx_scr[...].T
        g_ref[pl.ds(t0, ROUTE_GROUP), :] = g_scr[...].T
        return carry
    lax.fori_loop(0, ROUTE_TM // ROUTE_GROUP, group_body, 0)


def peer_route(x2d, ycat, w_out_bf16, ffn_norm_w, w_q_bf16, keys_bf16):
    m, d = x2d.shape
    tm = ROUTE_TM
    tok = lambda i: (i, 0)
    full2 = lambda i: (0, 0)
    return pl.pallas_call(
        _route_kernel,
        grid=(m // tm,),
        in_specs=[
            pl.BlockSpec((tm, d), tok),
            pl.BlockSpec((tm, MIX_WIDTH), tok),
            pl.BlockSpec((MIX_WIDTH, d), full2),
            pl.BlockSpec((1, d), full2),
            pl.BlockSpec((d, PEER_HEADS * PEER_KEY_DIM), full2),
            pl.BlockSpec((N_SUBSETS, N_KEYS, PEER_HALF), lambda i: (0, 0, 0)),
        ],
        out_specs=[
            pl.BlockSpec((tm, d), tok),
            pl.BlockSpec((tm, d), tok),
            pl.BlockSpec((tm, PEER_NSEL), tok),
            pl.BlockSpec((tm, PEER_NSEL), tok),
        ],
        out_shape=[
            jax.ShapeDtypeStruct((m, d), jnp.float32),
            jax.ShapeDtypeStruct((m, d), jnp.float32),
            jax.ShapeDtypeStruct((m, PEER_NSEL), jnp.int32),
            jax.ShapeDtypeStruct((m, PEER_NSEL), jnp.float32),
        ],
        scratch_shapes=[
            pltpu.VMEM((tm, PEER_HEADS * PEER_KEY_DIM), jnp.bfloat16),
            pltpu.VMEM((2, PEER_TOPK, ROUTE_GROUP), jnp.float32),
            pltpu.VMEM((2, PEER_TOPK, ROUTE_GROUP), jnp.int32),
            pltpu.VMEM((PEER_NSEL, ROUTE_GROUP), jnp.int32),
            pltpu.VMEM((PEER_NSEL, ROUTE_GROUP), jnp.float32),
        ],
        compiler_params=pltpu.CompilerParams(
            dimension_semantics=("arbitrary",),
            vmem_limit_bytes=48 * 1024 * 1024),
        name="peer_route",
    )(x2d, ycat, w_out_bf16, ffn_norm_w.reshape(1, d), w_q_bf16, keys_bf16)


ATT_T = 256


def _diff_attn_kernel(lam_ref, q0_ref, q1_ref, k_ref, v_ref, sw_ref, o_ref,
                      m_scr, l_scr, acc_scr, *, out_scale):
    qb = pl.program_id(2)
    m_scr[...] = jnp.full(m_scr.shape, -jnp.inf, jnp.float32)
    l_scr[...] = jnp.zeros(l_scr.shape, jnp.float32)
    acc_scr[...] = jnp.zeros(acc_scr.shape, jnp.float32)
    key_chunk = lax.broadcasted_iota(jnp.int32, (ATT_T, ATT_T), 0) // CHUNK
    qry_chunk = lax.broadcasted_iota(jnp.int32, (ATT_T, ATT_T), 1) // CHUNK
    diag_mask = key_chunk <= qry_chunk

    def update(kb, masked):
        k0 = pl.multiple_of(kb * ATT_T, ATT_T)
        ks = k_ref[0, pl.ds(k0, ATT_T), :]
        vt = v_ref[0, :, pl.ds(k0, ATT_T)]
        scores = [jnp.dot(ks, q_ref[0], preferred_element_type=jnp.float32)
                  for q_ref in (q0_ref, q1_ref)]
        m_old = [m_scr[0], m_scr[1]]
        l_old = [l_scr[0], l_scr[1]]
        acc_old = [acc_scr[0], acc_scr[1]]
        m_new, alpha, probs = [], [], []
        for j in range(2):
            s = jnp.where(diag_mask, scores[j], -jnp.inf) if masked else scores[j]
            m_new.append(jnp.maximum(m_old[j], jnp.max(s, axis=0, keepdims=True)))
            alpha.append(jnp.exp(m_old[j] - m_new[j]))
            probs.append(jnp.exp(s - m_new[j]))
        pv = [jnp.dot(vt, probs[j].astype(jnp.bfloat16), preferred_element_type=jnp.float32)
              for j in range(2)]
        for j in range(2):
            m_scr[j] = m_new[j]
            l_scr[j] = alpha[j] * l_old[j] + jnp.sum(probs[j], axis=0, keepdims=True)
            acc_scr[j] = alpha[j] * acc_old[j] + pv[j]

    def body(kb, carry):
        update(kb, False)
        return carry
    lax.fori_loop(0, qb, body, 0)
    update(qb, True)

    o = acc_scr[0] / l_scr[0] - lam_ref[0] * (acc_scr[1] / l_scr[1])
    o = o * lax.rsqrt(jnp.mean(o * o, axis=0, keepdims=True) + EPS) * sw_ref[...]
    o_ref[0] = (o * out_scale).T


def diff_attention_core(q0, q1, k, v, lam, subln_w, out_scale):
    b, s, _ = k.shape
    qspec = pl.BlockSpec((1, 2 * DIFF_QKDIM, ATT_T), lambda bi, h, i: (bi, h, i))
    kspec = pl.BlockSpec((1, s, 2 * DIFF_QKDIM), lambda bi, h, i: (bi, 0, h))
    vspec = pl.BlockSpec((1, DIFF_VDIM, s), lambda bi, h, i: (bi, h, 0))
    return pl.pallas_call(
        functools.partial(_diff_attn_kernel, out_scale=out_scale),
        grid=(b, DIFF_HEADS, s // ATT_T),
        in_specs=[
            pl.BlockSpec(memory_space=pltpu.SMEM),
            qspec, qspec, kspec, vspec,
            pl.BlockSpec((DIFF_VDIM, 1), lambda bi, h, i: (0, 0)),
        ],
        out_specs=pl.BlockSpec((1, ATT_T, DIFF_VDIM), lambda bi, h, i: (bi, i, h)),
        out_shape=jax.ShapeDtypeStruct((b, s, DIFF_WIDTH), jnp.float32),
        scratch_shapes=[
            pltpu.VMEM((2, 1, ATT_T), jnp.float32),
            pltpu.VMEM((2, 1, ATT_T), jnp.float32),
            pltpu.VMEM((2, DIFF_VDIM, ATT_T), jnp.float32),
        ],
        compiler_params=pltpu.CompilerParams(
            dimension_semantics=("arbitrary", "arbitrary", "arbitrary"),
            vmem_limit_bytes=32 * 1024 * 1024),
        name="diff_attention",
    )(lam, q0, q1, k, v, subln_w.reshape(DIFF_VDIM, 1))


SSD_T = 256
SSD_BC = 2 * SSD_GROUPS * SSD_STATE
SSD_GW = SSD_WIDTH // SSD_GROUPS
_HI = lax.Precision.HIGHEST


def _ssd_kernel(z_ref, xs_ref, bc_ref, dtc_ref, dtr_ref, cw_ref, cb_ref, dtb_c_ref, dtb_r_ref,
                al_c_ref, al_r_ref, dskip_ref, nw_ref, o_ref, xe_scr, st_scr, y_scr):
    t = SSD_T
    f32, bf16 = jnp.float32, jnp.bfloat16

    @pl.when(pl.program_id(1) == 0)
    def _():
        xe_scr[0:SUBLANES, :] = jnp.zeros((SUBLANES, SSD_CONV_CH), f32)
        st_scr[...] = jnp.zeros(st_scr.shape, f32)

    @pl.when(pl.program_id(1) > 0)
    def _():
        xe_scr[0:SUBLANES, :] = xe_scr[t:t + SUBLANES, :]

    xe_scr[SUBLANES:SUBLANES + t, 0:SSD_WIDTH] = xs_ref[0]
    xe_scr[SUBLANES:SUBLANES + t, SSD_WIDTH:SSD_CONV_CH] = bc_ref[0]
    conv = cb_ref[...]
    for w in range(CONV_WIDTH):
        r0 = SUBLANES - (CONV_WIDTH - 1) + w
        conv = conv + cw_ref[w:w + 1, :] * xe_scr[r0:r0 + t, :]
    xc = jax.nn.silu(conv)
    xs = xc[:, 0:SSD_WIDTH]

    dt_c = jax.nn.softplus(dtc_ref[0] + dtb_c_ref[...])
    dta_c = dt_c * (-jnp.exp(al_c_ref[...]))
    dta_r = jax.nn.softplus(dtr_ref[0] + dtb_r_ref[...]) * (-jnp.exp(al_r_ref[...]))
    row_i = lax.broadcasted_iota(jnp.int32, (t, t), 0)
    col_i = lax.broadcasted_iota(jnp.int32, (t, t), 1)
    same = (row_i // CHUNK) == (col_i // CHUNK)
    lower = (same & (col_i <= row_i)).astype(f32)
    upper = (same & (row_i <= col_i)).astype(f32)
    acs_c = jnp.dot(lower, dta_c, precision=_HI, preferred_element_type=f32)
    acs_r = jnp.dot(dta_r, upper, precision=_HI, preferred_element_type=f32)
    end_c = jnp.dot(same.astype(f32), dta_c, precision=_HI, preferred_element_type=f32)

    expand = (lax.broadcasted_iota(jnp.int32, (SSD_HEADS, SSD_WIDTH), 1) // SSD_HEAD_DIM
              == lax.broadcasted_iota(jnp.int32, (SSD_HEADS, SSD_WIDTH), 0)).astype(f32)
    widen = lambda a: jnp.dot(a, expand, precision=_HI, preferred_element_type=f32)
    xdt = xs * widen(dt_c)
    e_acs = widen(jnp.exp(acs_c))
    x_end = xdt * widen(jnp.exp(end_c - acs_c))
    e_end = widen(jnp.exp(end_c))

    causal = (lax.broadcasted_iota(jnp.int32, (CHUNK, CHUNK), 1)
              <= lax.broadcasted_iota(jnp.int32, (CHUNK, CHUNK), 0))
    head_of_lane = lax.broadcasted_iota(jnp.int32, (CHUNK, SSD_GW), 1) // SSD_HEAD_DIM
    nt = (((1,), (1,)), ((), ()))
    for c in range(t // CHUNK):
        r0 = c * CHUNK
        rows = slice(r0, r0 + CHUNK)
        new_state = []
        for g in range(SSD_GROUPS):
            cols = slice(g * SSD_GW, (g + 1) * SSD_GW)
            bm = xc[rows, SSD_WIDTH + g * SSD_STATE:SSD_WIDTH + (g + 1) * SSD_STATE].astype(bf16)
            cm = xc[rows, SSD_WIDTH + (SSD_GROUPS + g) * SSD_STATE:
                    SSD_WIDTH + (SSD_GROUPS + g + 1) * SSD_STATE].astype(bf16)
            cb = lax.dot_general(cm, bm, nt, preferred_element_type=f32)
            xdt_g = xdt[rows, cols]
            y = jnp.zeros((CHUNK, SSD_GW), f32)
            for hh in range(SSD_HEADS_PER_GROUP):
                h = g * SSD_HEADS_PER_GROUP + hh
                seg = acs_c[rows, h:h + 1] - acs_r[h:h + 1, r0:r0 + CHUNK]
                decay = jnp.exp(jnp.where(causal, seg, -jnp.inf))
                x_h = jnp.where(head_of_lane == hh, xdt_g, 0.0).astype(bf16)
                y = y + jnp.dot((cb * decay).astype(bf16), x_h, preferred_element_type=f32)
            y = y + e_acs[rows, cols] * jnp.dot(cm, st_scr[:, cols].astype(bf16),
                                                preferred_element_type=f32)
            y_scr[rows, cols] = y
            new_state.append(lax.dot_general(bm, x_end[rows, cols].astype(bf16),
                                             (((0,), (0,)), ((), ())),
                                             preferred_element_type=f32))
        st_scr[...] = st_scr[...] * e_end[r0:r0 + 1, :] + jnp.concatenate(new_state, axis=1)

    y = y_scr[...] + xs * dskip_ref[...]
    y = y * jax.nn.silu(z_ref[0])
    for g in range(SSD_GROUPS):
        cols = slice(g * SSD_GW, (g + 1) * SSD_GW)
        yg = y[:, cols]
        yg = yg * lax.rsqrt(jnp.mean(yg * yg, axis=-1, keepdims=True) + EPS)
        o_ref[0, :, cols] = yg * nw_ref[:, cols]


def ssd_branch(proj, dt_raw, conv_w, conv_b, dt_bias, a_log, d_skip, norm_w):
    b, s, _ = proj.shape
    h = SSD_HEADS
    w = SSD_WIDTH
    blk = lambda col: pl.BlockSpec((1, SSD_T, w), lambda bi, j, col=col: (bi, j, col))
    const = lambda shape: pl.BlockSpec(shape, lambda bi, j: (0,) * len(shape))
    return pl.pallas_call(
        _ssd_kernel,
        grid=(b, s // SSD_T),
        in_specs=[
            blk(0), blk(1), blk(2),
            pl.BlockSpec((1, SSD_T, h), lambda bi, j: (bi, j, 0)),
            pl.BlockSpec((1, h, SSD_T), lambda bi, j: (bi, 0, j)),
            const((CONV_WIDTH, SSD_CONV_CH)), const((1, SSD_CONV_CH)),
            const((1, h)), const((h, 1)), const((1, h)), const((h, 1)),
            const((1, w)), const((1, w)),
        ],
        out_specs=pl.BlockSpec((1, SSD_T, w), lambda bi, j: (bi, j, 0)),
        out_shape=jax.ShapeDtypeStruct((b, s, w), jnp.float32),
        scratch_shapes=[
            pltpu.VMEM((SUBLANES + SSD_T, SSD_CONV_CH), jnp.float32),
            pltpu.VMEM((SSD_STATE, w), jnp.float32),
            pltpu.VMEM((SSD_T, w), jnp.float32),
        ],
        compiler_params=pltpu.CompilerParams(
            dimension_semantics=("arbitrary", "arbitrary"),
            vmem_limit_bytes=32 * 1024 * 1024),
        name="ssd_branch",
    )(proj, proj, proj, dt_raw, jnp.swapaxes(dt_raw, 1, 2), conv_w, conv_b.reshape(1, -1),
      dt_bias.reshape(1, h), dt_bias.reshape(h, 1), a_log.reshape(1, h), a_log.reshape(h, 1),
      jnp.repeat(d_skip, SSD_HEAD_DIM).reshape(1, w), norm_w.reshape(1, w))


def rmsnorm(x, w):
    xf = x.astype(jnp.float32)
    y = xf * lax.rsqrt(jnp.mean(xf * xf, axis=-1, keepdims=True) + EPS)
    return (y * w.astype(jnp.float32)).astype(x.dtype)


def rotary_tables(seq):
    inv = jnp.power(ROPE_THETA, -jnp.arange(0, ROPE_DIM, 2, dtype=jnp.float32) / ROPE_DIM)
    ang = jnp.arange(seq, dtype=jnp.float32)[:, None] * inv[None, :]
    return jnp.cos(ang), jnp.sin(ang)


def partial_rotary(x, cos, sin):
    half = ROPE_DIM // 2
    c = cos[:, None, None, :]
    s = sin[:, None, None, :]
    xf = x.astype(jnp.float32)
    x1, x2, xp = xf[..., :half], xf[..., half:ROPE_DIM], xf[..., ROPE_DIM:]
    out = jnp.concatenate([x1 * c - x2 * s, x1 * s + x2 * c, xp], axis=-1)
    return out.astype(x.dtype)


def diff_attention(q, k, v, qn_w, kn_w, lq1, lk1, lq2, lk2, subln_w, lambda_init, cos, sin):
    f32 = jnp.float32
    b, s, _ = q.shape
    q = q.reshape(b, s, DIFF_HEADS, 2, DIFF_QKDIM)
    k = k.reshape(b, s, DIFF_HEADS, 2, DIFF_QKDIM)
    q = partial_rotary(rmsnorm(q, qn_w), cos, sin).astype(f32)
    kf = partial_rotary(rmsnorm(k, kn_w), cos, sin).astype(f32)
    lam = (jnp.exp(jnp.sum(lq1.astype(f32) * lk1.astype(f32)))
           - jnp.exp(jnp.sum(lq2.astype(f32) * lk2.astype(f32))) + lambda_init)
    scale = DIFF_QKDIM ** -0.5
    branch = jnp.arange(2)
    to_cols = lambda a: jnp.swapaxes(a.reshape(b, s, -1).astype(jnp.bfloat16), 1, 2)
    q0 = to_cols(q * scale * (branch == 0)[:, None])
    q1 = to_cols(q * scale * (branch == 1)[:, None])
    kb = kf.reshape(b, s, QK_COLS).astype(jnp.bfloat16)
    vb = to_cols(v)
    return diff_attention_core(q0, q1, kb, vb, lam.reshape(1), subln_w, 1.0 - lambda_init)


def kernel(x, mix_norm_w, w_in, conv_w, conv_b, dt_bias, a_log, d_skip, ssd_norm_w, q_norm_w, k_norm_w, lambda_q1, lambda_k1, lambda_q2, lambda_k2, subln_w, w_out, ffn_norm_w, peer_w_q, peer_sub_keys, peer_u, peer_v):
    b, s, d = x.shape
    cos, sin = rotary_tables(s)
    layer = 0
    lambda_init = 0.8 - 0.6 * math.exp(-0.3 * layer)
    proj = norm_proj(x.reshape(b * s, d), mix_norm_w[layer],
                     w_in[layer].astype(jnp.bfloat16)).reshape(b, s, IN_COLS)
    dt_raw = proj[..., SPLITS[1]:SPLITS[2]]
    q = proj[..., SPLITS[2]:SPLITS[3]]
    k = proj[..., SPLITS[3]:SPLITS[4]]
    v = proj[..., SPLITS[4]:]
    y_ssd = ssd_branch(proj, dt_raw, conv_w[layer], conv_b[layer], dt_bias[layer],
                       a_log[layer], d_skip[layer], ssd_norm_w[layer])
    y_diff = diff_attention(q, k, v, q_norm_w[layer], k_norm_w[layer], lambda_q1[layer],
                            lambda_k1[layer], lambda_q2[layer], lambda_k2[layer],
                            subln_w[layer], lambda_init, cos, sin)
    ycat = jnp.concatenate([y_ssd, y_diff], axis=-1).reshape(b * s, MIX_WIDTH)
    keys = peer_sub_keys[layer].reshape(N_SUBSETS, N_KEYS, PEER_HALF).astype(jnp.bfloat16)
    x2, xn, idx, gates = peer_route(x.reshape(b * s, d), ycat, w_out[layer].astype(jnp.bfloat16),
                                    ffn_norm_w[layer], peer_w_q[layer].astype(jnp.bfloat16), keys)
    out = peer_gather(xn, idx, gates, x2, pack_uv(peer_u[layer], peer_v[layer]))
    return out.reshape(b, s, d)
```

```python
import functools
import math

import jax
import jax.numpy as jnp
from jax import lax
from jax.experimental import pallas as pl
from jax.experimental.pallas import tpu as pltpu

D_MODEL = 1024
CHUNK = 64
EPS = 1e-6

SSD_WIDTH = D_MODEL // 2
SSD_HEAD_DIM = 64
SSD_HEADS = SSD_WIDTH // SSD_HEAD_DIM
SSD_GROUPS = 2
SSD_HEADS_PER_GROUP = SSD_HEADS // SSD_GROUPS
SSD_STATE = 128
CONV_WIDTH = 4
SSD_CONV_CH = SSD_WIDTH + 2 * SSD_GROUPS * SSD_STATE

DIFF_WIDTH = D_MODEL - SSD_WIDTH
DIFF_HEADS = 4
DIFF_VDIM = DIFF_WIDTH // DIFF_HEADS
DIFF_QKDIM = DIFF_VDIM // 2
ROPE_DIM = DIFF_QKDIM // 4
ROPE_THETA = 500000.0
Q_BLOCK = 128

QK_COLS = DIFF_HEADS * 2 * DIFF_QKDIM
SPLITS = (SSD_WIDTH,
          SSD_WIDTH + SSD_CONV_CH,
          SSD_WIDTH + SSD_CONV_CH + SSD_HEADS,
          SSD_WIDTH + SSD_CONV_CH + SSD_HEADS + QK_COLS,
          SSD_WIDTH + SSD_CONV_CH + SSD_HEADS + 2 * QK_COLS)
IN_COLS = SPLITS[-1] + DIFF_WIDTH
MIX_WIDTH = SSD_WIDTH + DIFF_WIDTH

PEER_HEADS = 8
N_KEYS = 128
N_EXPERTS = N_KEYS * N_KEYS
PEER_KEY_DIM = 256
PEER_HALF = PEER_KEY_DIM // 2
PEER_TOPK = 16
PEER_TOKEN_BLOCK = 128


def _norm_proj_kernel(x_ref, nw_ref, w_ref, o_ref):
    xf = x_ref[...]
    y = xf * lax.rsqrt(jnp.mean(xf * xf, axis=-1, keepdims=True) + EPS)
    y = (y * nw_ref[...]).astype(jnp.bfloat16)
    o_ref[...] = jnp.dot(y, w_ref[...], preferred_element_type=jnp.float32)


def norm_proj(x2d, norm_w, w_bf16, tm=512):
    m, d = x2d.shape
    n = w_bf16.shape[1]
    return pl.pallas_call(
        _norm_proj_kernel,
        grid=(m // tm,),
        in_specs=[
            pl.BlockSpec((tm, d), lambda i: (i, 0)),
            pl.BlockSpec((1, d), lambda i: (0, 0)),
            pl.BlockSpec((d, n), lambda i: (0, 0)),
        ],
        out_specs=pl.BlockSpec((tm, n), lambda i: (i, 0)),
        out_shape=jax.ShapeDtypeStruct((m, n), jnp.float32),
        compiler_params=pltpu.CompilerParams(
            dimension_semantics=("arbitrary",),
            vmem_limit_bytes=56 * 1024 * 1024),
        name="norm_proj",
    )(x2d, norm_w.reshape(1, d), w_bf16)


PEER_NSEL = PEER_HEADS * PEER_TOPK
PEER_TB = 8
PEER_SLOTS = 3
LANES = 128
SUBLANES = 8
_SQRT_HALF = 0.7071067811865476
_U_MASK = 0xFFFF0000


def _peer_gather_kernel(idx_cur, idx_mid, idx_nxt, xn_ref, g_ref, xres_ref, uv_hbm,
                        o_ref, buf, part_scr, wb_scr, sem):
    i = pl.program_id(0)
    nb = pl.num_programs(0)
    slot = i % PEER_SLOTS
    nxt = (i + PEER_SLOTS - 1) % PEER_SLOTS
    d = xn_ref.shape[-1]
    n_groups = PEER_NSEL // SUBLANES
    n_chunks = d // LANES
    half = PEER_TB // 2

    def start_row(idx_ref, s, tt, kk, j):
        e = idx_ref[tt * PEER_NSEL + kk * SUBLANES + j]
        pltpu.make_async_copy(uv_hbm.at[e, pl.ds(0, 1)], buf.at[s, tt, kk, pl.ds(j, 1)],
                              sem.at[s]).start(priority=j % 2)

    @pl.when(i == 0)
    def _():
        for s, idx_ref in ((0, idx_cur), (1, idx_mid)):
            for tt in range(PEER_TB):
                def body(kk, carry, tt=tt, s=s, idx_ref=idx_ref):
                    for j in range(SUBLANES):
                        start_row(idx_ref, s, tt, kk, j)
                    return carry
                lax.fori_loop(0, n_groups, body, 0)

    pltpu.make_async_copy(buf.at[slot], buf.at[slot], sem.at[slot]).wait()

    for tt in range(half):
        def body(kk, carry, tt=tt):
            r0 = pl.multiple_of(kk * SUBLANES, SUBLANES)
            for n, tok in enumerate((2 * tt, 2 * tt + 1)):
                urows = lax.bitcast_convert_type(buf[slot, tok, kk] & jnp.uint32(_U_MASK),
                                                 jnp.float32)
                prod = urows * xn_ref[tok:tok + 1, :]
                part = prod[:, 0:LANES]
                for c in range(1, n_chunks):
                    part = part + prod[:, c * LANES:(c + 1) * LANES]
                part_scr[tok, pl.ds(r0, SUBLANES), :] = part
                for j in range(n * half, (n + 1) * half):
                    start_row(idx_nxt, nxt, tt, kk, j)
            return carry
        lax.fori_loop(0, n_groups, body, 0)

    for kk in range(n_groups):
        for j in range(SUBLANES):
            start_row(idx_nxt, nxt, PEER_TB - 1, kk, j)

    lane = lax.broadcasted_iota(jnp.int32, (PEER_NSEL, PEER_TB), 1)
    hm = jnp.zeros((PEER_NSEL, PEER_TB), jnp.float32)
    for tok in range(PEER_TB):
        h = jnp.sum(part_scr[tok], axis=1, keepdims=True)
        hm = jnp.where(lane == tok, h, hm)
    eye = (lax.broadcasted_iota(jnp.int32, (PEER_NSEL, PEER_NSEL), 0)
           == lax.broadcasted_iota(jnp.int32, (PEER_NSEL, PEER_NSEL), 1)).astype(jnp.float32)
    g_t = lax.dot_general(eye, g_ref[...], (((1,), (1,)), ((), ())),
                          precision=lax.Precision.HIGHEST,
                          preferred_element_type=jnp.float32)
    w = g_t * (0.5 * hm * (1.0 + lax.erf(hm * _SQRT_HALF)))
    for tok in range(PEER_TB):
        wb_scr[tok] = jnp.broadcast_to(w[:, tok:tok + 1], (PEER_NSEL, LANES))

    for tt, toks in ((half, (0, 1, 2)), (half + 1, (3, 4, 5)), (half + 2, (6, 7))):
        def body(kk, accs, tt=tt, toks=toks):
            r0 = pl.multiple_of(kk * SUBLANES, SUBLANES)
            out = []
            for n, tok in enumerate(toks):
                wg = wb_scr[tok, pl.ds(r0, SUBLANES), :]
                rows = lax.bitcast_convert_type(buf[slot, tok, kk] << 16, jnp.float32)
                out.append(accs[n] + rows * jnp.concatenate([wg] * n_chunks, axis=1))
                for j in range(n * SUBLANES // len(toks), (n + 1) * SUBLANES // len(toks)):
                    start_row(idx_nxt, nxt, tt, kk, j)
            return tuple(out)
        zero = jnp.zeros((SUBLANES, d), jnp.float32)
        accs = lax.fori_loop(0, n_groups, body, (zero,) * len(toks))
        for n, tok in enumerate(toks):
            o_ref[tok:tok + 1, :] = (xres_ref[tok:tok + 1, :]
                                     + jnp.sum(accs[n], axis=0, keepdims=True))

    @pl.when(i == nb - 1)
    def _():
        for s in (nxt, (i + 1) % PEER_SLOTS):
            pltpu.make_async_copy(buf.at[s], buf.at[s], sem.at[s]).wait()


def pack_uv(u, v):
    to_bits = lambda a: lax.bitcast_convert_type(a.astype(jnp.bfloat16), jnp.uint16).astype(jnp.uint32)
    packed = (to_bits(u) << 16) | to_bits(v)
    return jnp.pad(packed[:, None, :], ((0, 0), (0, SUBLANES - 1), (0, 0)))


def peer_gather(xn, idx, gates, xres, uv):
    m, d = xn.shape
    nb = m // PEER_TB
    tok = lambda i: (i, 0)
    idx_flat = idx.reshape(m * PEER_NSEL)
    return pl.pallas_call(
        _peer_gather_kernel,
        grid=(nb,),
        in_specs=[
            pl.BlockSpec((PEER_TB * PEER_NSEL,), lambda i: (i,), memory_space=pltpu.SMEM),
            pl.BlockSpec((PEER_TB * PEER_NSEL,), lambda i: (jnp.minimum(i + 1, nb - 1),),
                         memory_space=pltpu.SMEM),
            pl.BlockSpec((PEER_TB * PEER_NSEL,), lambda i: (jnp.minimum(i + 2, nb - 1),),
                         memory_space=pltpu.SMEM),
            pl.BlockSpec((PEER_TB, d), tok),
            pl.BlockSpec((PEER_TB, PEER_NSEL), tok),
            pl.BlockSpec((PEER_TB, d), tok),
            pl.BlockSpec(memory_space=pl.ANY),
        ],
        out_specs=pl.BlockSpec((PEER_TB, d), tok),
        out_shape=jax.ShapeDtypeStruct((m, d), jnp.float32),
        scratch_shapes=[
            pltpu.VMEM((PEER_SLOTS, PEER_TB, PEER_NSEL // SUBLANES, SUBLANES, d), jnp.uint32),
            pltpu.VMEM((PEER_TB, PEER_NSEL, LANES), jnp.float32),
            pltpu.VMEM((PEER_TB, PEER_NSEL, LANES), jnp.float32),
            pltpu.SemaphoreType.DMA((PEER_SLOTS,)),
        ],
        compiler_params=pltpu.CompilerParams(
            dimension_semantics=("arbitrary",),
            vmem_limit_bytes=48 * 1024 * 1024,
            disable_bounds_checks=True),
        name="peer_gather",
    )(idx_flat, idx_flat, idx_flat, xn, gates, xres, uv)


ROUTE_TM = 256
ROUTE_GROUP = LANES
N_SUBSETS = PEER_HEADS * 2


_POS_SENTINEL = 1 << 20


def _take_top(vals, pos, payload=None):
    m = jnp.max(vals, axis=0, keepdims=True)
    p = jnp.min(jnp.where(vals == m, pos, _POS_SENTINEL), axis=0, keepdims=True)
    hit = pos == p
    picked = p if payload is None else jnp.max(jnp.where(hit, payload, -1), axis=0, keepdims=True)
    return m, picked, jnp.where(hit, -jnp.inf, vals)


def _route_kernel(x_ref, y_ref, wo_ref, nw_ref, wq_ref, keys_ref,
                  xo_ref, xn_ref, idx_ref, g_ref,
                  q_scr, sh_scr, ih_scr, idx_scr, g_scr):
    xa = x_ref[...] + jnp.dot(y_ref[...].astype(jnp.bfloat16), wo_ref[...],
                              preferred_element_type=jnp.float32)
    xo_ref[...] = xa
    xn = xa * lax.rsqrt(jnp.mean(xa * xa, axis=-1, keepdims=True) + EPS) * nw_ref[...]
    xn_ref[...] = xn
    q_scr[...] = jnp.dot(xn.astype(jnp.bfloat16), wq_ref[...],
                         preferred_element_type=jnp.float32).astype(jnp.bfloat16)

    key_row = lax.broadcasted_iota(jnp.int32, (N_KEYS, ROUTE_GROUP), 0)

    def group_body(gi, carry):
        t0 = pl.multiple_of(gi * ROUTE_GROUP, ROUTE_GROUP)

        def head_body(h, carry2):
            kk = PEER_TOPK
            sc = []
            for j in range(2):
                c0 = pl.multiple_of((2 * h + j) * PEER_HALF, PEER_HALF)
                qc = q_scr[pl.ds(t0, ROUTE_GROUP), pl.ds(c0, PEER_HALF)]
                sc.append(lax.dot_general(keys_ref[2 * h + j], qc, (((1,), (1,)), ((), ())),
                                          preferred_element_type=jnp.float32))
            for r in range(kk):
                for j in range(2):
                    m, ki, sc[j] = _take_top(sc[j], key_row)
                    sh_scr[j, r:r + 1, :] = m
                    ih_scr[j, r:r + 1, :] = ki

            row8 = lax.broadcasted_iota(jnp.int32, (SUBLANES, ROUTE_GROUP), 0)
            s1_lo, s1_hi = sh_scr[1, 0:SUBLANES, :], sh_scr[1, SUBLANES:kk, :]
            i1_lo, i1_hi = ih_scr[1, 0:SUBLANES, :], ih_scr[1, SUBLANES:kk, :]
            vals = [sh_scr[0, 0:1, :] + s1_lo, sh_scr[0, 0:1, :] + s1_hi]
            cids = [ih_scr[0, 0:1, :] * N_KEYS + i1_lo, ih_scr[0, 0:1, :] * N_KEYS + i1_hi]
            poss = [row8, row8 + SUBLANES]
            for a in range(1, SUBLANES):
                v = sh_scr[0, a:a + 1, :] + s1_lo
                if kk // (a + 1) < SUBLANES:
                    v = jnp.where(row8 < kk // (a + 1), v, -jnp.inf)
                vals.append(v)
                cids.append(ih_scr[0, a:a + 1, :] * N_KEYS + i1_lo)
                poss.append(row8 + a * kk)
            vals.append(sh_scr[0, SUBLANES:kk, :] + sh_scr[1, 0:1, :])
            cids.append(ih_scr[0, SUBLANES:kk, :] * N_KEYS + ih_scr[1, 0:1, :])
            poss.append((row8 + SUBLANES) * kk)
            cand = jnp.concatenate(vals, axis=0)
            cidx = jnp.concatenate(cids, axis=0)
            cpos = jnp.concatenate(poss, axis=0)
            tops, ids = [], []
            for r in range(kk):
                m, e, cand = _take_top(cand, cpos, cidx)
                tops.append(m)
                ids.append(e)
            top_s = jnp.concatenate(tops, axis=0)
            ex = jnp.exp(top_s - tops[0])
            gate = ex / jnp.sum(ex, axis=0, keepdims=True)
            r0 = pl.multiple_of(h * kk, kk)
            idx_scr[pl.ds(r0, kk), :] = jnp.concatenate(ids, axis=0)
            g_scr[pl.ds(r0, kk), :] = gate
            return carry2
        lax.fori_loop(0, PEER_HEADS, head_body, 0)

        idx_ref[pl.ds(t0, ROUTE_GROUP), :] = idx_scr[...].T
        g_ref[pl.ds(t0, ROUTE_GROUP), :] = g_scr[...].T
        return carry
    lax.fori_loop(0, ROUTE_TM // ROUTE_GROUP, group_body, 0)


def peer_route(x2d, ycat, w_out_bf16, ffn_norm_w, w_q_bf16, keys_bf16):
    m, d = x2d.shape
    tm = ROUTE_TM
    tok = lambda i: (i, 0)
    full2 = lambda i: (0, 0)
    return pl.pallas_call(
        _route_kernel,
        grid=(m // tm,),
        in_specs=[
            pl.BlockSpec((tm, d), tok),
            pl.BlockSpec((tm, MIX_WIDTH), tok),
            pl.BlockSpec((MIX_WIDTH, d), full2),
            pl.BlockSpec((1, d), full2),
            pl.BlockSpec((d, PEER_HEADS * PEER_KEY_DIM), full2),
            pl.BlockSpec((N_SUBSETS, N_KEYS, PEER_HALF), lambda i: (0, 0, 0)),
        ],
        out_specs=[
            pl.BlockSpec((tm, d), tok),
            pl.BlockSpec((tm, d), tok),
            pl.BlockSpec((tm, PEER_NSEL), tok),
            pl.BlockSpec((tm, PEER_NSEL), tok),
        ],
        out_shape=[
            jax.ShapeDtypeStruct((m, d), jnp.float32),
            jax.ShapeDtypeStruct((m, d), jnp.float32),
            jax.ShapeDtypeStruct((m, PEER_NSEL), jnp.int32),
            jax.ShapeDtypeStruct((m, PEER_NSEL), jnp.float32),
        ],
        scratch_shapes=[
            pltpu.VMEM((tm, PEER_HEADS * PEER_KEY_DIM), jnp.bfloat16),
            pltpu.VMEM((2, PEER_TOPK, ROUTE_GROUP), jnp.float32),
            pltpu.VMEM((2, PEER_TOPK, ROUTE_GROUP), jnp.int32),
            pltpu.VMEM((PEER_NSEL, ROUTE_GROUP), jnp.int32),
            pltpu.VMEM((PEER_NSEL, ROUTE_GROUP), jnp.float32),
        ],
        compiler_params=pltpu.CompilerParams(
            dimension_semantics=("arbitrary",),
            vmem_limit_bytes=48 * 1024 * 1024),
        name="peer_route",
    )(x2d, ycat, w_out_bf16, ffn_norm_w.reshape(1, d), w_q_bf16, keys_bf16)


ATT_T = 256


def _diff_attn_kernel(lam_ref, q0_ref, q1_ref, k_ref, v_ref, sw_ref, o_ref,
                      m_scr, l_scr, acc_scr, *, out_scale):
    qb = pl.program_id(2)
    m_scr[...] = jnp.full(m_scr.shape, -jnp.inf, jnp.float32)
    l_scr[...] = jnp.zeros(l_scr.shape, jnp.float32)
    acc_scr[...] = jnp.zeros(acc_scr.shape, jnp.float32)
    key_chunk = lax.broadcasted_iota(jnp.int32, (ATT_T, ATT_T), 0) // CHUNK
    qry_chunk = lax.broadcasted_iota(jnp.int32, (ATT_T, ATT_T), 1) // CHUNK
    diag_mask = key_chunk <= qry_chunk

    def update(kb, masked):
        k0 = pl.multiple_of(kb * ATT_T, ATT_T)
        ks = k_ref[0, pl.ds(k0, ATT_T), :]
        vt = v_ref[0, :, pl.ds(k0, ATT_T)]
        scores = [jnp.dot(ks, q_ref[0], preferred_element_type=jnp.float32)
                  for q_ref in (q0_ref, q1_ref)]
        m_old = [m_scr[0], m_scr[1]]
        l_old = [l_scr[0], l_scr[1]]
        acc_old = [acc_scr[0], acc_scr[1]]
        m_new, alpha, probs = [], [], []
        for j in range(2):
            s = jnp.where(diag_mask, scores[j], -jnp.inf) if masked else scores[j]
            m_new.append(jnp.maximum(m_old[j], jnp.max(s, axis=0, keepdims=True)))
            alpha.append(jnp.exp(m_old[j] - m_new[j]))
            probs.append(jnp.exp(s - m_new[j]))
        pv = [jnp.dot(vt, probs[j].astype(jnp.bfloat16), preferred_element_type=jnp.float32)
              for j in range(2)]
        for j in range(2):
            m_scr[j] = m_new[j]
            l_scr[j] = alpha[j] * l_old[j] + jnp.sum(probs[j], axis=0, keepdims=True)
            acc_scr[j] = alpha[j] * acc_old[j] + pv[j]

    def body(kb, carry):
        update(kb, False)
        return carry
    lax.fori_loop(0, qb, body, 0)
    update(qb, True)

    o = acc_scr[0] / l_scr[0] - lam_ref[0] * (acc_scr[1] / l_scr[1])
    o = o * lax.rsqrt(jnp.mean(o * o, axis=0, keepdims=True) + EPS) * sw_ref[...]
    o_ref[0] = (o * out_scale).T


def diff_attention_core(q0, q1, k, v, lam, subln_w, out_scale):
    b, s, _ = k.shape
    qspec = pl.BlockSpec((1, 2 * DIFF_QKDIM, ATT_T), lambda bi, h, i: (bi, h, i))
    kspec = pl.BlockSpec((1, s, 2 * DIFF_QKDIM), lambda bi, h, i: (bi, 0, h))
    vspec = pl.BlockSpec((1, DIFF_VDIM, s), lambda bi, h, i: (bi, h, 0))
    return pl.pallas_call(
        functools.partial(_diff_attn_kernel, out_scale=out_scale),
        grid=(b, DIFF_HEADS, s // ATT_T),
        in_specs=[
            pl.BlockSpec(memory_space=pltpu.SMEM),
            qspec, qspec, kspec, vspec,
            pl.BlockSpec((DIFF_VDIM, 1), lambda bi, h, i: (0, 0)),
        ],
        out_specs=pl.BlockSpec((1, ATT_T, DIFF_VDIM), lambda bi, h, i: (bi, i, h)),
        out_shape=jax.ShapeDtypeStruct((b, s, DIFF_WIDTH), jnp.float32),
        scratch_shapes=[
            pltpu.VMEM((2, 1, ATT_T), jnp.float32),
            pltpu.VMEM((2, 1, ATT_T), jnp.float32),
            pltpu.VMEM((2, DIFF_VDIM, ATT_T), jnp.float32),
        ],
        compiler_params=pltpu.CompilerParams(
            dimension_semantics=("arbitrary", "arbitrary", "arbitrary"),
            vmem_limit_bytes=32 * 1024 * 1024),
        name="diff_attention",
    )(lam, q0, q1, k, v, subln_w.reshape(DIFF_VDIM, 1))


SSD_T = 256
SSD_BC = 2 * SSD_GROUPS * SSD_STATE
SSD_GW = SSD_WIDTH // SSD_GROUPS
_HI = lax.Precision.HIGHEST


def _ssd_kernel(z_ref, xs_ref, bc_ref, dtc_ref, dtr_ref, cw_ref, cb_ref, dtb_c_ref, dtb_r_ref,
                al_c_ref, al_r_ref, dskip_ref, nw_ref, o_ref, xe_scr, st_scr, y_scr):
    t = SSD_T
    f32, bf16 = jnp.float32, jnp.bfloat16

    @pl.when(pl.program_id(1) == 0)
    def _():
        xe_scr[0:SUBLANES, :] = jnp.zeros((SUBLANES, SSD_CONV_CH), f32)
        st_scr[...] = jnp.zeros(st_scr.shape, f32)

    @pl.when(pl.program_id(1) > 0)
    def _():
        xe_scr[0:SUBLANES, :] = xe_scr[t:t + SUBLANES, :]

    xe_scr[SUBLANES:SUBLANES + t, 0:SSD_WIDTH] = xs_ref[0]
    xe_scr[SUBLANES:SUBLANES + t, SSD_WIDTH:SSD_CONV_CH] = bc_ref[0]
    conv = cb_ref[...]
    for w in range(CONV_WIDTH):
        r0 = SUBLANES - (CONV_WIDTH - 1) + w
        conv = conv + cw_ref[w:w + 1, :] * xe_scr[r0:r0 + t, :]
    xc = jax.nn.silu(conv)
    xs = xc[:, 0:SSD_WIDTH]

    dt_c = jax.nn.softplus(dtc_ref[0] + dtb_c_ref[...])
    dta_c = dt_c * (-jnp.exp(al_c_ref[...]))
    dta_r = jax.nn.softplus(dtr_ref[0] + dtb_r_ref[...]) * (-jnp.exp(al_r_ref[...]))
    row_i = lax.broadcasted_iota(jnp.int32, (t, t), 0)
    col_i = lax.broadcasted_iota(jnp.int32, (t, t), 1)
    same = (row_i // CHUNK) == (col_i // CHUNK)
    lower = (same & (col_i <= row_i)).astype(f32)
    upper = (same & (row_i <= col_i)).astype(f32)
    acs_c = jnp.dot(lower, dta_c, precision=_HI, preferred_element_type=f32)
    acs_r = jnp.dot(dta_r, upper, precision=_HI, preferred_element_type=f32)
    end_c = jnp.dot(same.astype(f32), dta_c, precision=_HI, preferred_element_type=f32)

    expand = (lax.broadcasted_iota(jnp.int32, (SSD_HEADS, SSD_WIDTH), 1) // SSD_HEAD_DIM
              == lax.broadcasted_iota(jnp.int32, (SSD_HEADS, SSD_WIDTH), 0)).astype(f32)
    widen = lambda a: jnp.dot(a, expand, precision=_HI, preferred_element_type=f32)
    xdt = xs * widen(dt_c)
    e_acs = widen(jnp.exp(acs_c))
    x_end = xdt * widen(jnp.exp(end_c - acs_c))
    e_end = widen(jnp.exp(end_c))

    causal = (lax.broadcasted_iota(jnp.int32, (CHUNK, CHUNK), 1)
              <= lax.broadcasted_iota(jnp.int32, (CHUNK, CHUNK), 0))
    head_of_lane = lax.broadcasted_iota(jnp.int32, (CHUNK, SSD_GW), 1) // SSD_HEAD_DIM
    nt = (((1,), (1,)), ((), ()))
    for c in range(t // CHUNK):
        r0 = c * CHUNK
        rows = slice(r0, r0 + CHUNK)
        new_state = []
        for g in range(SSD_GROUPS):
            cols = slice(g * SSD_GW, (g + 1) * SSD_GW)
            bm = xc[rows, SSD_WIDTH + g * SSD_STATE:SSD_WIDTH + (g + 1) * SSD_STATE].astype(bf16)
            cm = xc[rows, SSD_WIDTH + (SSD_GROUPS + g) * SSD_STATE:
                    SSD_WIDTH + (SSD_GROUPS + g + 1) * SSD_STATE].astype(bf16)
            cb = lax.dot_general(cm, bm, nt, preferred_element_type=f32)
            xdt_g = xdt[rows, cols]
            y = jnp.zeros((CHUNK, SSD_GW), f32)
            for hh in range(SSD_HEADS_PER_GROUP):
                h = g * SSD_HEADS_PER_GROUP + hh
                seg = acs_c[rows, h:h + 1] - acs_r[h:h + 1, r0:r0 + CHUNK]
                decay = jnp.exp(jnp.where(causal, seg, -jnp.inf))
                x_h = jnp.where(head_of_lane == hh, xdt_g, 0.0).astype(bf16)
                y = y + jnp.dot((cb * decay).astype(bf16), x_h, preferred_element_type=f32)
            y = y + e_acs[rows, cols] * jnp.dot(cm, st_scr[:, cols].astype(bf16),
                                                preferred_element_type=f32)
            y_scr[rows, cols] = y
            new_state.append(lax.dot_general(bm, x_end[rows, cols].astype(bf16),
                                             (((0,), (0,)), ((), ())),
                                             preferred_element_type=f32))
        st_scr[...] = st_scr[...] * e_end[r0:r0 + 1, :] + jnp.concatenate(new_state, axis=1)

    y = y_scr[...] + xs * dskip_ref[...]
    y = y * jax.nn.silu(z_ref[0])
    for g in range(SSD_GROUPS):
        cols = slice(g * SSD_GW, (g + 1) * SSD_GW)
        yg = y[:, cols]
        yg = yg * lax.rsqrt(jnp.mean(yg * yg, axis=-1, keepdims=True) + EPS)
        o_ref[0, :, cols] = yg * nw_ref[:, cols]


def ssd_branch(proj, dt_raw, conv_w, conv_b, dt_bias, a_log, d_skip, norm_w):
    b, s, _ = proj.shape
    h = SSD_HEADS
    w = SSD_WIDTH
    blk = lambda col: pl.BlockSpec((1, SSD_T, w), lambda bi, j, col=col: (bi, j, col))
    const = lambda shape: pl.BlockSpec(shape, lambda bi, j: (0,) * len(shape))
    return pl.pallas_call(
        _ssd_kernel,
        grid=(b, s // SSD_T),
        in_specs=[
            blk(0), blk(1), blk(2),
            pl.BlockSpec((1, SSD_T, h), lambda bi, j: (bi, j, 0)),
            pl.BlockSpec((1, h, SSD_T), lambda bi, j: (bi, 0, j)),
            const((CONV_WIDTH, SSD_CONV_CH)), const((1, SSD_CONV_CH)),
            const((1, h)), const((h, 1)), const((1, h)), const((h, 1)),
            const((1, w)), const((1, w)),
        ],
        out_specs=pl.BlockSpec((1, SSD_T, w), lambda bi, j: (bi, j, 0)),
        out_shape=jax.ShapeDtypeStruct((b, s, w), jnp.float32),
        scratch_shapes=[
            pltpu.VMEM((SUBLANES + SSD_T, SSD_CONV_CH), jnp.float32),
            pltpu.VMEM((SSD_STATE, w), jnp.float32),
            pltpu.VMEM((SSD_T, w), jnp.float32),
        ],
        compiler_params=pltpu.CompilerParams(
            dimension_semantics=("arbitrary", "arbitrary"),
            vmem_limit_bytes=32 * 1024 * 1024),
        name="ssd_branch",
    )(proj, proj, proj, dt_raw, jnp.swapaxes(dt_raw, 1, 2), conv_w, conv_b.reshape(1, -1),
      dt_bias.reshape(1, h), dt_bias.reshape(h, 1), a_log.reshape(1, h), a_log.reshape(h, 1),
      jnp.repeat(d_skip, SSD_HEAD_DIM).reshape(1, w), norm_w.reshape(1, w))


def rmsnorm(x, w):
    xf = x.astype(jnp.float32)
    y = xf * lax.rsqrt(jnp.mean(xf * xf, axis=-1, keepdims=True) + EPS)
    return (y * w.astype(jnp.float32)).astype(x.dtype)


def rotary_tables(seq):
    inv = jnp.power(ROPE_THETA, -jnp.arange(0, ROPE_DIM, 2, dtype=jnp.float32) / ROPE_DIM)
    ang = jnp.arange(seq, dtype=jnp.float32)[:, None] * inv[None, :]
    return jnp.cos(ang), jnp.sin(ang)


def partial_rotary(x, cos, sin):
    half = ROPE_DIM // 2
    c = cos[:, None, None, :]
    s = sin[:, None, None, :]
    xf = x.astype(jnp.float32)
    x1, x2, xp = xf[..., :half], xf[..., half:ROPE_DIM], xf[..., ROPE_DIM:]
    out = jnp.concatenate([x1 * c - x2 * s, x1 * s + x2 * c, xp], axis=-1)
    return out.astype(x.dtype)


def diff_attention(q, k, v, qn_w, kn_w, lq1, lk1, lq2, lk2, subln_w, lambda_init, cos, sin):
    f32 = jnp.float32
    b, s, _ = q.shape
    q = q.reshape(b, s, DIFF_HEADS, 2, DIFF_QKDIM)
    k = k.reshape(b, s, DIFF_HEADS, 2, DIFF_QKDIM)
    q = partial_rotary(rmsnorm(q, qn_w), cos, sin).astype(f32)
    kf = partial_rotary(rmsnorm(k, kn_w), cos, sin).astype(f32)
    lam = (jnp.exp(jnp.sum(lq1.astype(f32) * lk1.astype(f32)))
           - jnp.exp(jnp.sum(lq2.astype(f32) * lk2.astype(f32))) + lambda_init)
    scale = DIFF_QKDIM ** -0.5
    branch = jnp.arange(2)
    to_cols = lambda a: jnp.swapaxes(a.reshape(b, s, -1).astype(jnp.bfloat16), 1, 2)
    q0 = to_cols(q * scale * (branch == 0)[:, None])
    q1 = to_cols(q * scale * (branch == 1)[:, None])
    kb = kf.reshape(b, s, QK_COLS).astype(jnp.bfloat16)
    vb = to_cols(v)
    return diff_attention_core(q0, q1, kb, vb, lam.reshape(1), subln_w, 1.0 - lambda_init)


def kernel(x, mix_norm_w, w_in, conv_w, conv_b, dt_bias, a_log, d_skip, ssd_norm_w, q_norm_w, k_norm_w, lambda_q1, lambda_k1, lambda_q2, lambda_k2, subln_w, w_out, ffn_norm_w, peer_w_q, peer_sub_keys, peer_u, peer_v):
    b, s, d = x.shape
    cos, sin = rotary_tables(s)
    layer = 0
    lambda_init = 0.8 - 0.6 * math.exp(-0.3 * layer)
    proj = norm_proj(x.reshape(b * s, d), mix_norm_w[layer],
                     w_in[layer].astype(jnp.bfloat16)).reshape(b, s, IN_COLS)
    dt_raw = proj[..., SPLITS[1]:SPLITS[2]]
    q = proj[..., SPLITS[2]:SPLITS[3]]
    k = proj[..., SPLITS[3]:SPLITS[4]]
    v = proj[..., SPLITS[4]:]
    y_ssd = ssd_branch(proj, dt_raw, conv_w[layer], conv_b[layer], dt_bias[layer],
                       a_log[layer], d_skip[layer], ssd_norm_w[layer])
    y_diff = diff_attention(q, k, v, q_norm_w[layer], k_norm_w[layer], lambda_q1[layer],
                            lambda_k1[layer], lambda_q2[layer], lambda_k2[layer],
                            subln_w[layer], lambda_init, cos, sin)
    ycat = jnp.concatenate([y_ssd, y_diff], axis=-1).reshape(b * s, MIX_WIDTH)
    keys = peer_sub_keys[layer].reshape(N_SUBSETS, N_KEYS, PEER_HALF).astype(jnp.bfloat16)
    x2, xn, idx, gates = peer_route(x.reshape(b * s, d), ycat, w_out[layer].astype(jnp.bfloat16),
                                    ffn_norm_w[layer], peer_w_q[layer].astype(jnp.bfloat16), keys)
    out = peer_gather(xn, idx, gates, x2, pack_uv(peer_u[layer], peer_v[layer]))
    return out.reshape(b, s, d)
```

```python
import functools
import math

import jax
import jax.numpy as jnp
from jax import lax
from jax.experimental import pallas as pl
from jax.experimental.pallas import tpu as pltpu

D_MODEL = 1024
CHUNK = 64
EPS = 1e-6

SSD_WIDTH = D_MODEL // 2
SSD_HEAD_DIM = 64
SSD_HEADS = SSD_WIDTH // SSD_HEAD_DIM
SSD_GROUPS = 2
SSD_HEADS_PER_GROUP = SSD_HEADS // SSD_GROUPS
SSD_STATE = 128
CONV_WIDTH = 4
SSD_CONV_CH = SSD_WIDTH + 2 * SSD_GROUPS * SSD_STATE

DIFF_WIDTH = D_MODEL - SSD_WIDTH
DIFF_HEADS = 4
DIFF_VDIM = DIFF_WIDTH // DIFF_HEADS
DIFF_QKDIM = DIFF_VDIM // 2
ROPE_DIM = DIFF_QKDIM // 4
ROPE_THETA = 500000.0
Q_BLOCK = 128

QK_COLS = DIFF_HEADS * 2 * DIFF_QKDIM
SPLITS = (SSD_WIDTH,
          SSD_WIDTH + SSD_CONV_CH,
          SSD_WIDTH + SSD_CONV_CH + SSD_HEADS,
          SSD_WIDTH + SSD_CONV_CH + SSD_HEADS + QK_COLS,
          SSD_WIDTH + SSD_CONV_CH + SSD_HEADS + 2 * QK_COLS)
IN_COLS = SPLITS[-1] + DIFF_WIDTH
MIX_WIDTH = SSD_WIDTH + DIFF_WIDTH

PEER_HEADS = 8
N_KEYS = 128
N_EXPERTS = N_KEYS * N_KEYS
PEER_KEY_DIM = 256
PEER_HALF = PEER_KEY_DIM // 2
PEER_TOPK = 16
PEER_TOKEN_BLOCK = 128


def _norm_proj_kernel(x_ref, nw_ref, w_ref, o_ref):
    xf = x_ref[...]
    y = xf * lax.rsqrt(jnp.mean(xf * xf, axis=-1, keepdims=True) + EPS)
    y = (y * nw_ref[...]).astype(jnp.bfloat16)
    o_ref[...] = jnp.dot(y, w_ref[...], preferred_element_type=jnp.float32)


def norm_proj(x2d, norm_w, w_bf16, tm=512):
    m, d = x2d.shape
    n = w_bf16.shape[1]
    return pl.pallas_call(
        _norm_proj_kernel,
        grid=(m // tm,),
        in_specs=[
            pl.BlockSpec((tm, d), lambda i: (i, 0)),
            pl.BlockSpec((1, d), lambda i: (0, 0)),
            pl.BlockSpec((d, n), lambda i: (0, 0)),
        ],
        out_specs=pl.BlockSpec((tm, n), lambda i: (i, 0)),
        out_shape=jax.ShapeDtypeStruct((m, n), jnp.float32),
        compiler_params=pltpu.CompilerParams(
            dimension_semantics=("arbitrary",),
            vmem_limit_bytes=56 * 1024 * 1024),
        name="norm_proj",
    )(x2d, norm_w.reshape(1, d), w_bf16)


PEER_NSEL = PEER_HEADS * PEER_TOPK
PEER_TB = 8
PEER_SLOTS = 3
LANES = 128
SUBLANES = 8
_SQRT_HALF = 0.7071067811865476
_U_MASK = 0xFFFF0000


def _peer_gather_kernel(idx_cur, idx_mid, idx_nxt, xn_ref, g_ref, xres_ref, uv_hbm,
                        o_ref, buf, part_scr, wb_scr, sem):
    i = pl.program_id(0)
    nb = pl.num_programs(0)
    slot = i % PEER_SLOTS
    nxt = (i + PEER_SLOTS - 1) % PEER_SLOTS
    d = xn_ref.shape[-1]
    n_groups = PEER_NSEL // SUBLANES
    n_chunks = d // LANES
    half = PEER_TB // 2

    def start_row(idx_ref, s, tt, kk, j):
        e = idx_ref[tt * PEER_NSEL + kk * SUBLANES + j]
        pltpu.make_async_copy(uv_hbm.at[e, pl.ds(0, 1)], buf.at[s, tt, kk, pl.ds(j, 1)],
                              sem.at[s]).start(priority=j % 2)

    @pl.when(i == 0)
    def _():
        for s, idx_ref in ((0, idx_cur), (1, idx_mid)):
            for tt in range(PEER_TB):
                def body(kk, carry, tt=tt, s=s, idx_ref=idx_ref):
                    for j in range(SUBLANES):
                        start_row(idx_ref, s, tt, kk, j)
                    return carry
                lax.fori_loop(0, n_groups, body, 0)

    pltpu.make_async_copy(buf.at[slot], buf.at[slot], sem.at[slot]).wait()

    for tt in range(half):
        def body(kk, carry, tt=tt):
            r0 = pl.multiple_of(kk * SUBLANES, SUBLANES)
            for n, tok in enumerate((2 * tt, 2 * tt + 1)):
                urows = lax.bitcast_convert_type(buf[slot, tok, kk] & jnp.uint32(_U_MASK),
                                                 jnp.float32)
                prod = urows * xn_ref[tok:tok + 1, :]
                part = prod[:, 0:LANES]
                for c in range(1, n_chunks):
                    part = part + prod[:, c * LANES:(c + 1) * LANES]
                part_scr[tok, pl.ds(r0, SUBLANES), :] = part
                for j in range(n * half, (n + 1) * half):
                    start_row(idx_nxt, nxt, tt, kk, j)
            return carry
        lax.fori_loop(0, n_groups, body, 0)

    for kk in range(n_groups):
        for j in range(SUBLANES):
            start_row(idx_nxt, nxt, PEER_TB - 1, kk, j)

    lane = lax.broadcasted_iota(jnp.int32, (PEER_NSEL, PEER_TB), 1)
    hm = jnp.zeros((PEER_NSEL, PEER_TB), jnp.float32)
    for tok in range(PEER_TB):
        h = jnp.sum(part_scr[tok], axis=1, keepdims=True)
        hm = jnp.where(lane == tok, h, hm)
    eye = (lax.broadcasted_iota(jnp.int32, (PEER_NSEL, PEER_NSEL), 0)
           == lax.broadcasted_iota(jnp.int32, (PEER_NSEL, PEER_NSEL), 1)).astype(jnp.float32)
    g_t = lax.dot_general(eye, g_ref[...], (((1,), (1,)), ((), ())),
                          precision=lax.Precision.HIGHEST,
                          preferred_element_type=jnp.float32)
    w = g_t * (0.5 * hm * (1.0 + lax.erf(hm * _SQRT_HALF)))
    for tok in range(PEER_TB):
        wb_scr[tok] = jnp.broadcast_to(w[:, tok:tok + 1], (PEER_NSEL, LANES))

    for tt, toks in ((half, (0, 1, 2)), (half + 1, (3, 4, 5)), (half + 2, (6, 7))):
        def body(kk, accs, tt=tt, toks=toks):
            r0 = pl.multiple_of(kk * SUBLANES, SUBLANES)
            out = []
            for n, tok in enumerate(toks):
                wg = wb_scr[tok, pl.ds(r0, SUBLANES), :]
                rows = lax.bitcast_convert_type(buf[slot, tok, kk] << 16, jnp.float32)
                out.append(accs[n] + rows * jnp.concatenate([wg] * n_chunks, axis=1))
                for j in range(n * SUBLANES // len(toks), (n + 1) * SUBLANES // len(toks)):
                    start_row(idx_nxt, nxt, tt, kk, j)
            return tuple(out)
        zero = jnp.zeros((SUBLANES, d), jnp.float32)
        accs = lax.fori_loop(0, n_groups, body, (zero,) * len(toks))
        for n, tok in enumerate(toks):
            o_ref[tok:tok + 1, :] = (xres_ref[tok:tok + 1, :]
                                     + jnp.sum(accs[n], axis=0, keepdims=True))

    @pl.when(i == nb - 1)
    def _():
        for s in (nxt, (i + 1) % PEER_SLOTS):
            pltpu.make_async_copy(buf.at[s], buf.at[s], sem.at[s]).wait()


def pack_uv(u, v):
    to_bits = lambda a: lax.bitcast_convert_type(a.astype(jnp.bfloat16), jnp.uint16).astype(jnp.uint32)
    packed = (to_bits(u) << 16) | to_bits(v)
    return jnp.pad(packed[:, None, :], ((0, 0), (0, SUBLANES - 1), (0, 0)))


def peer_gather(xn, idx, gates, xres, uv):
    m, d = xn.shape
    nb = m // PEER_TB
    tok = lambda i: (i, 0)
    idx_flat = idx.reshape(m * PEER_NSEL)
    return pl.pallas_call(
        _peer_gather_kernel,
        grid=(nb,),
        in_specs=[
            pl.BlockSpec((PEER_TB * PEER_NSEL,), lambda i: (i,), memory_space=pltpu.SMEM),
            pl.BlockSpec((PEER_TB * PEER_NSEL,), lambda i: (jnp.minimum(i + 1, nb - 1),),
                         memory_space=pltpu.SMEM),
            pl.BlockSpec((PEER_TB * PEER_NSEL,), lambda i: (jnp.minimum(i + 2, nb - 1),),
                         memory_space=pltpu.SMEM),
            pl.BlockSpec((PEER_TB, d), tok),
            pl.BlockSpec((PEER_TB, PEER_NSEL), tok),
            pl.BlockSpec((PEER_TB, d), tok),
            pl.BlockSpec(memory_space=pl.ANY),
        ],
        out_specs=pl.BlockSpec((PEER_TB, d), tok),
        out_shape=jax.ShapeDtypeStruct((m, d), jnp.float32),
        scratch_shapes=[
            pltpu.VMEM((PEER_SLOTS, PEER_TB, PEER_NSEL // SUBLANES, SUBLANES, d), jnp.uint32),
            pltpu.VMEM((PEER_TB, PEER_NSEL, LANES), jnp.float32),
            pltpu.VMEM((PEER_TB, PEER_NSEL, LANES), jnp.float32),
            pltpu.SemaphoreType.DMA((PEER_SLOTS,)),
        ],
        compiler_params=pltpu.CompilerParams(
            dimension_semantics=("arbitrary",),
            vmem_limit_bytes=48 * 1024 * 1024,
            disable_bounds_checks=True),
        name="peer_gather",
    )(idx_flat, idx_flat, idx_flat, xn, gates, xres, uv)


ROUTE_TM = 256
ROUTE_GROUP = LANES
N_SUBSETS = PEER_HEADS * 2


_POS_SENTINEL = 1 << 20


def _take_top(vals, pos, payload=None):
    m = jnp.max(vals, axis=0, keepdims=True)
    p = jnp.min(jnp.where(vals == m, pos, _POS_SENTINEL), axis=0, keepdims=True)
    hit = pos == p
    picked = p if payload is None else jnp.max(jnp.where(hit, payload, -1), axis=0, keepdims=True)
    return m, picked, jnp.where(hit, -jnp.inf, vals)


def _route_kernel(x_ref, y_ref, wo_ref, nw_ref, wq_ref, keys_ref,
                  xo_ref, xn_ref, idx_ref, g_ref,
                  q_scr, sh_scr, ih_scr, idx_scr, g_scr):
    xa = x_ref[...] + jnp.dot(y_ref[...].astype(jnp.bfloat16), wo_ref[...],
                              preferred_element_type=jnp.float32)
    xo_ref[...] = xa
    xn = xa * lax.rsqrt(jnp.mean(xa * xa, axis=-1, keepdims=True) + EPS) * nw_ref[...]
    xn_ref[...] = xn
    q_scr[...] = jnp.dot(xn.astype(jnp.bfloat16), wq_ref[...],
                         preferred_element_type=jnp.float32).astype(jnp.bfloat16)

    key_row = lax.broadcasted_iota(jnp.int32, (N_KEYS, ROUTE_GROUP), 0)

    n_groups = ROUTE_TM // ROUTE_GROUP
    kk = PEER_TOPK
    row8 = lax.broadcasted_iota(jnp.int32, (SUBLANES, ROUTE_GROUP), 0)

    def head_body(h, carry):
        sc = [[None, None] for _ in range(n_groups)]
        for gi in range(n_groups):
            for j in range(2):
                c0 = pl.multiple_of((2 * h + j) * PEER_HALF, PEER_HALF)
                qc = q_scr[gi * ROUTE_GROUP:(gi + 1) * ROUTE_GROUP, pl.ds(c0, PEER_HALF)]
                sc[gi][j] = lax.dot_general(keys_ref[2 * h + j], qc, (((1,), (1,)), ((), ())),
                                            preferred_element_type=jnp.float32)
        for r in range(kk):
            for gi in range(n_groups):
                for j in range(2):
                    m, ki, sc[gi][j] = _take_top(sc[gi][j], key_row)
                    sh_scr[gi, j, r:r + 1, :] = m
                    ih_scr[gi, j, r:r + 1, :] = ki

        cand, cidx = [], []
        poss = [row8, row8 + SUBLANES] + [row8 + a * kk for a in range(1, SUBLANES)]
        poss.append((row8 + SUBLANES) * kk)
        cpos = jnp.concatenate(poss, axis=0)
        for gi in range(n_groups):
            sh, ih = sh_scr.at[gi], ih_scr.at[gi]
            s1_lo, s1_hi = sh[1, 0:SUBLANES, :], sh[1, SUBLANES:kk, :]
            i1_lo, i1_hi = ih[1, 0:SUBLANES, :], ih[1, SUBLANES:kk, :]
            vals = [sh[0, 0:1, :] + s1_lo, sh[0, 0:1, :] + s1_hi]
            cids = [ih[0, 0:1, :] * N_KEYS + i1_lo, ih[0, 0:1, :] * N_KEYS + i1_hi]
            for a in range(1, SUBLANES):
                v = sh[0, a:a + 1, :] + s1_lo
                if kk // (a + 1) < SUBLANES:
                    v = jnp.where(row8 < kk // (a + 1), v, -jnp.inf)
                vals.append(v)
                cids.append(ih[0, a:a + 1, :] * N_KEYS + i1_lo)
            vals.append(sh[0, SUBLANES:kk, :] + sh[1, 0:1, :])
            cids.append(ih[0, SUBLANES:kk, :] * N_KEYS + ih[1, 0:1, :])
            cand.append(jnp.concatenate(vals, axis=0))
            cidx.append(jnp.concatenate(cids, axis=0))
        tops = [[] for _ in range(n_groups)]
        ids = [[] for _ in range(n_groups)]
        for r in range(kk):
            for gi in range(n_groups):
                m, e, cand[gi] = _take_top(cand[gi], cpos, cidx[gi])
                tops[gi].append(m)
                ids[gi].append(e)
        r0 = pl.multiple_of(h * kk, kk)
        for gi in range(n_groups):
            top_s = jnp.concatenate(tops[gi], axis=0)
            ex = jnp.exp(top_s - tops[gi][0])
            idx_scr[gi, pl.ds(r0, kk), :] = jnp.concatenate(ids[gi], axis=0)
            g_scr[gi, pl.ds(r0, kk), :] = ex / jnp.sum(ex, axis=0, keepdims=True)
        return carry
    lax.fori_loop(0, PEER_HEADS, head_body, 0)

    for gi in range(n_groups):
        idx_ref[gi * ROUTE_GROUP:(gi + 1) * ROUTE_GROUP, :] = idx_scr[gi].T
        g_ref[gi * ROUTE_GROUP:(gi + 1) * ROUTE_GROUP, :] = g_scr[gi].T


def peer_route(x2d, ycat, w_out_bf16, ffn_norm_w, w_q_bf16, keys_bf16):
    m, d = x2d.shape
    tm = ROUTE_TM
    tok = lambda i: (i, 0)
    full2 = lambda i: (0, 0)
    return pl.pallas_call(
        _route_kernel,
        grid=(m // tm,),
        in_specs=[
            pl.BlockSpec((tm, d), tok),
            pl.BlockSpec((tm, MIX_WIDTH), tok),
            pl.BlockSpec((MIX_WIDTH, d), full2),
            pl.BlockSpec((1, d), full2),
            pl.BlockSpec((d, PEER_HEADS * PEER_KEY_DIM), full2),
            pl.BlockSpec((N_SUBSETS, N_KEYS, PEER_HALF), lambda i: (0, 0, 0)),
        ],
        out_specs=[
            pl.BlockSpec((tm, d), tok),
            pl.BlockSpec((tm, d), tok),
            pl.BlockSpec((tm, PEER_NSEL), tok),
            pl.BlockSpec((tm, PEER_NSEL), tok),
        ],
        out_shape=[
            jax.ShapeDtypeStruct((m, d), jnp.float32),
            jax.ShapeDtypeStruct((m, d), jnp.float32),
            jax.ShapeDtypeStruct((m, PEER_NSEL), jnp.int32),
            jax.ShapeDtypeStruct((m, PEER_NSEL), jnp.float32),
        ],
        scratch_shapes=[
            pltpu.VMEM((tm, PEER_HEADS * PEER_KEY_DIM), jnp.bfloat16),
            pltpu.VMEM((tm // ROUTE_GROUP, 2, PEER_TOPK, ROUTE_GROUP), jnp.float32),
            pltpu.VMEM((tm // ROUTE_GROUP, 2, PEER_TOPK, ROUTE_GROUP), jnp.int32),
            pltpu.VMEM((tm // ROUTE_GROUP, PEER_NSEL, ROUTE_GROUP), jnp.int32),
            pltpu.VMEM((tm // ROUTE_GROUP, PEER_NSEL, ROUTE_GROUP), jnp.float32),
        ],
        compiler_params=pltpu.CompilerParams(
            dimension_semantics=("arbitrary",),
            vmem_limit_bytes=48 * 1024 * 1024),
        name="peer_route",
    )(x2d, ycat, w_out_bf16, ffn_norm_w.reshape(1, d), w_q_bf16, keys_bf16)


ATT_T = 256
ATT_TK = 512


def _diff_attn_kernel(lam_ref, q0_ref, q1_ref, k_ref, v_ref, sw_ref, o_ref,
                      m_scr, l_scr, acc_scr, *, out_scale):
    qb = pl.program_id(2)
    m_scr[...] = jnp.full(m_scr.shape, -jnp.inf, jnp.float32)
    l_scr[...] = jnp.zeros(l_scr.shape, jnp.float32)
    acc_scr[...] = jnp.zeros(acc_scr.shape, jnp.float32)
    n_full = (qb * ATT_T) // ATT_TK
    key_chunk = (lax.broadcasted_iota(jnp.int32, (ATT_TK, ATT_T), 0) + n_full * ATT_TK) // CHUNK
    qry_chunk = (lax.broadcasted_iota(jnp.int32, (ATT_TK, ATT_T), 1) + qb * ATT_T) // CHUNK
    diag_mask = key_chunk <= qry_chunk

    def update(kb, masked):
        k0 = pl.multiple_of(kb * ATT_TK, ATT_TK)
        ks = k_ref[0, pl.ds(k0, ATT_TK), :]
        vt = v_ref[0, :, pl.ds(k0, ATT_TK)]
        scores = [jnp.dot(ks, q_ref[0], preferred_element_type=jnp.float32)
                  for q_ref in (q0_ref, q1_ref)]
        m_old = [m_scr[0], m_scr[1]]
        l_old = [l_scr[0], l_scr[1]]
        acc_old = [acc_scr[0], acc_scr[1]]
        m_new, alpha, probs = [], [], []
        for j in range(2):
            s = jnp.where(diag_mask, scores[j], -jnp.inf) if masked else scores[j]
            m_new.append(jnp.maximum(m_old[j], jnp.max(s, axis=0, keepdims=True)))
            alpha.append(jnp.exp(m_old[j] - m_new[j]))
            probs.append(jnp.exp(s - m_new[j]))
        pv = [jnp.dot(vt, probs[j].astype(jnp.bfloat16), preferred_element_type=jnp.float32)
              for j in range(2)]
        for j in range(2):
            m_scr[j] = m_new[j]
            l_scr[j] = alpha[j] * l_old[j] + jnp.sum(probs[j], axis=0, keepdims=True)
            acc_scr[j] = alpha[j] * acc_old[j] + pv[j]

    def body(kb, carry):
        update(kb, False)
        return carry
    lax.fori_loop(0, n_full, body, 0)
    update(n_full, True)

    o = acc_scr[0] / l_scr[0] - lam_ref[0] * (acc_scr[1] / l_scr[1])
    o = o * lax.rsqrt(jnp.mean(o * o, axis=0, keepdims=True) + EPS) * sw_ref[...]
    o_ref[0] = (o * out_scale).T


def diff_attention_core(q0, q1, k, v, lam, subln_w, out_scale):
    b, s, _ = k.shape
    qspec = pl.BlockSpec((1, 2 * DIFF_QKDIM, ATT_T), lambda bi, h, i: (bi, h, i))
    kspec = pl.BlockSpec((1, s, 2 * DIFF_QKDIM), lambda bi, h, i: (bi, 0, h))
    vspec = pl.BlockSpec((1, DIFF_VDIM, s), lambda bi, h, i: (bi, h, 0))
    return pl.pallas_call(
        functools.partial(_diff_attn_kernel, out_scale=out_scale),
        grid=(b, DIFF_HEADS, s // ATT_T),
        in_specs=[
            pl.BlockSpec(memory_space=pltpu.SMEM),
            qspec, qspec, kspec, vspec,
            pl.BlockSpec((DIFF_VDIM, 1), lambda bi, h, i: (0, 0)),
        ],
        out_specs=pl.BlockSpec((1, ATT_T, DIFF_VDIM), lambda bi, h, i: (bi, i, h)),
        out_shape=jax.ShapeDtypeStruct((b, s, DIFF_WIDTH), jnp.float32),
        scratch_shapes=[
            pltpu.VMEM((2, 1, ATT_T), jnp.float32),
            pltpu.VMEM((2, 1, ATT_T), jnp.float32),
            pltpu.VMEM((2, DIFF_VDIM, ATT_T), jnp.float32),
        ],
        compiler_params=pltpu.CompilerParams(
            dimension_semantics=("arbitrary", "arbitrary", "arbitrary"),
            vmem_limit_bytes=32 * 1024 * 1024),
        name="diff_attention",
    )(lam, q0, q1, k, v, subln_w.reshape(DIFF_VDIM, 1))


SSD_T = 256
SSD_BC = 2 * SSD_GROUPS * SSD_STATE
SSD_GW = SSD_WIDTH // SSD_GROUPS
_HI = lax.Precision.HIGHEST


def _ssd_kernel(z_ref, xs_ref, bc_ref, dtc_ref, dtr_ref, cw_ref, cb_ref, dtb_c_ref, dtb_r_ref,
                al_c_ref, al_r_ref, dskip_ref, nw_ref, o_ref, xe_scr, st_scr, y_scr):
    t = SSD_T
    f32, bf16 = jnp.float32, jnp.bfloat16

    @pl.when(pl.program_id(1) == 0)
    def _():
        xe_scr[0:SUBLANES, :] = jnp.zeros((SUBLANES, SSD_CONV_CH), f32)
        st_scr[...] = jnp.zeros(st_scr.shape, f32)

    @pl.when(pl.program_id(1) > 0)
    def _():
        xe_scr[0:SUBLANES, :] = xe_scr[t:t + SUBLANES, :]

    xe_scr[SUBLANES:SUBLANES + t, 0:SSD_WIDTH] = xs_ref[0]
    xe_scr[SUBLANES:SUBLANES + t, SSD_WIDTH:SSD_CONV_CH] = bc_ref[0]
    conv = cb_ref[...]
    for w in range(CONV_WIDTH):
        r0 = SUBLANES - (CONV_WIDTH - 1) + w
        conv = conv + cw_ref[w:w + 1, :] * xe_scr[r0:r0 + t, :]
    xc = jax.nn.silu(conv)
    xs = xc[:, 0:SSD_WIDTH]

    dt_c = jax.nn.softplus(dtc_ref[0] + dtb_c_ref[...])
    dta_c = dt_c * (-jnp.exp(al_c_ref[...]))
    dta_r = jax.nn.softplus(dtr_ref[0] + dtb_r_ref[...]) * (-jnp.exp(al_r_ref[...]))
    row_i = lax.broadcasted_iota(jnp.int32, (t, t), 0)
    col_i = lax.broadcasted_iota(jnp.int32, (t, t), 1)
    same = (row_i // CHUNK) == (col_i // CHUNK)
    lower = (same & (col_i <= row_i)).astype(f32)
    upper = (same & (row_i <= col_i)).astype(f32)
    acs_c = jnp.dot(lower, dta_c, precision=_HI, preferred_element_type=f32)
    acs_r = jnp.dot(dta_r, upper, precision=_HI, preferred_element_type=f32)
    end_c = jnp.dot(same.astype(f32), dta_c, precision=_HI, preferred_element_type=f32)

    expand = (lax.broadcasted_iota(jnp.int32, (SSD_HEADS, SSD_WIDTH), 1) // SSD_HEAD_DIM
              == lax.broadcasted_iota(jnp.int32, (SSD_HEADS, SSD_WIDTH), 0)).astype(f32)
    widen = lambda a: jnp.dot(a, expand, precision=_HI, preferred_element_type=f32)
    xdt = xs * widen(dt_c)
    e_acs = widen(jnp.exp(acs_c))
    x_end = xdt * widen(jnp.exp(end_c - acs_c))
    e_end = widen(jnp.exp(end_c))

    causal = (lax.broadcasted_iota(jnp.int32, (CHUNK, CHUNK), 1)
              <= lax.broadcasted_iota(jnp.int32, (CHUNK, CHUNK), 0))
    head_of_lane = lax.broadcasted_iota(jnp.int32, (CHUNK, SSD_GW), 1) // SSD_HEAD_DIM
    nt = (((1,), (1,)), ((), ()))
    for c in range(t // CHUNK):
        r0 = c * CHUNK
        rows = slice(r0, r0 + CHUNK)
        new_state = []
        for g in range(SSD_GROUPS):
            cols = slice(g * SSD_GW, (g + 1) * SSD_GW)
            bm = xc[rows, SSD_WIDTH + g * SSD_STATE:SSD_WIDTH + (g + 1) * SSD_STATE].astype(bf16)
            cm = xc[rows, SSD_WIDTH + (SSD_GROUPS + g) * SSD_STATE:
                    SSD_WIDTH + (SSD_GROUPS + g + 1) * SSD_STATE].astype(bf16)
            cb = lax.dot_general(cm, bm, nt, preferred_element_type=f32)
            xdt_g = xdt[rows, cols]
            y = jnp.zeros((CHUNK, SSD_GW), f32)
            for hh in range(SSD_HEADS_PER_GROUP):
                h = g * SSD_HEADS_PER_GROUP + hh
                seg = acs_c[rows, h:h + 1] - acs_r[h:h + 1, r0:r0 + CHUNK]
                decay = jnp.exp(jnp.where(causal, seg, -jnp.inf))
                x_h = jnp.where(head_of_lane == hh, xdt_g, 0.0).astype(bf16)
                y = y + jnp.dot((cb * decay).astype(bf16), x_h, preferred_element_type=f32)
            y = y + e_acs[rows, cols] * jnp.dot(cm, st_scr[:, cols].astype(bf16),
                                                preferred_element_type=f32)
            y_scr[rows, cols] = y
            new_state.append(lax.dot_general(bm, x_end[rows, cols].astype(bf16),
                                             (((0,), (0,)), ((), ())),
                                             preferred_element_type=f32))
        st_scr[...] = st_scr[...] * e_end[r0:r0 + 1, :] + jnp.concatenate(new_state, axis=1)

    y = y_scr[...] + xs * dskip_ref[...]
    y = y * jax.nn.silu(z_ref[0])
    for g in range(SSD_GROUPS):
        cols = slice(g * SSD_GW, (g + 1) * SSD_GW)
        yg = y[:, cols]
        yg = yg * lax.rsqrt(jnp.mean(yg * yg, axis=-1, keepdims=True) + EPS)
        o_ref[0, :, cols] = yg * nw_ref[:, cols]


def ssd_branch(proj, dt_raw, conv_w, conv_b, dt_bias, a_log, d_skip, norm_w):
    b, s, _ = proj.shape
    h = SSD_HEADS
    w = SSD_WIDTH
    blk = lambda col: pl.BlockSpec((1, SSD_T, w), lambda bi, j, col=col: (bi, j, col))
    const = lambda shape: pl.BlockSpec(shape, lambda bi, j: (0,) * len(shape))
    return pl.pallas_call(
        _ssd_kernel,
        grid=(b, s // SSD_T),
        in_specs=[
            blk(0), blk(1), blk(2),
            pl.BlockSpec((1, SSD_T, h), lambda bi, j: (bi, j, 0)),
            pl.BlockSpec((1, h, SSD_T), lambda bi, j: (bi, 0, j)),
            const((CONV_WIDTH, SSD_CONV_CH)), const((1, SSD_CONV_CH)),
            const((1, h)), const((h, 1)), const((1, h)), const((h, 1)),
            const((1, w)), const((1, w)),
        ],
        out_specs=pl.BlockSpec((1, SSD_T, w), lambda bi, j: (bi, j, 0)),
        out_shape=jax.ShapeDtypeStruct((b, s, w), jnp.float32),
        scratch_shapes=[
            pltpu.VMEM((SUBLANES + SSD_T, SSD_CONV_CH), jnp.float32),
            pltpu.VMEM((SSD_STATE, w), jnp.float32),
            pltpu.VMEM((SSD_T, w), jnp.float32),
        ],
        compiler_params=pltpu.CompilerParams(
            dimension_semantics=("arbitrary", "arbitrary"),
            vmem_limit_bytes=32 * 1024 * 1024),
        name="ssd_branch",
    )(proj, proj, proj, dt_raw, jnp.swapaxes(dt_raw, 1, 2), conv_w, conv_b.reshape(1, -1),
      dt_bias.reshape(1, h), dt_bias.reshape(h, 1), a_log.reshape(1, h), a_log.reshape(h, 1),
      jnp.repeat(d_skip, SSD_HEAD_DIM).reshape(1, w), norm_w.reshape(1, w))


def rmsnorm(x, w):
    xf = x.astype(jnp.float32)
    y = xf * lax.rsqrt(jnp.mean(xf * xf, axis=-1, keepdims=True) + EPS)
    return (y * w.astype(jnp.float32)).astype(x.dtype)


def rotary_tables(seq):
    inv = jnp.power(ROPE_THETA, -jnp.arange(0, ROPE_DIM, 2, dtype=jnp.float32) / ROPE_DIM)
    ang = jnp.arange(seq, dtype=jnp.float32)[:, None] * inv[None, :]
    return jnp.cos(ang), jnp.sin(ang)


def partial_rotary(x, cos, sin):
    half = ROPE_DIM // 2
    c = cos[:, None, None, :]
    s = sin[:, None, None, :]
    xf = x.astype(jnp.float32)
    x1, x2, xp = xf[..., :half], xf[..., half:ROPE_DIM], xf[..., ROPE_DIM:]
    out = jnp.concatenate([x1 * c - x2 * s, x1 * s + x2 * c, xp], axis=-1)
    return out.astype(x.dtype)


def diff_attention(q, k, v, qn_w, kn_w, lq1, lk1, lq2, lk2, subln_w, lambda_init, cos, sin):
    f32 = jnp.float32
    b, s, _ = q.shape
    q = q.reshape(b, s, DIFF_HEADS, 2, DIFF_QKDIM)
    k = k.reshape(b, s, DIFF_HEADS, 2, DIFF_QKDIM)
    q = partial_rotary(rmsnorm(q, qn_w), cos, sin).astype(f32)
    kf = partial_rotary(rmsnorm(k, kn_w), cos, sin).astype(f32)
    lam = (jnp.exp(jnp.sum(lq1.astype(f32) * lk1.astype(f32)))
           - jnp.exp(jnp.sum(lq2.astype(f32) * lk2.astype(f32))) + lambda_init)
    scale = DIFF_QKDIM ** -0.5
    branch = jnp.arange(2)
    to_cols = lambda a: jnp.swapaxes(a.reshape(b, s, -1).astype(jnp.bfloat16), 1, 2)
    q0 = to_cols(q * scale * (branch == 0)[:, None])
    q1 = to_cols(q * scale * (branch == 1)[:, None])
    kb = kf.reshape(b, s, QK_COLS).astype(jnp.bfloat16)
    vb = to_cols(v)
    return diff_attention_core(q0, q1, kb, vb, lam.reshape(1), subln_w, 1.0 - lambda_init)


def kernel(x, mix_norm_w, w_in, conv_w, conv_b, dt_bias, a_log, d_skip, ssd_norm_w, q_norm_w, k_norm_w, lambda_q1, lambda_k1, lambda_q2, lambda_k2, subln_w, w_out, ffn_norm_w, peer_w_q, peer_sub_keys, peer_u, peer_v):
    b, s, d = x.shape
    cos, sin = rotary_tables(s)
    layer = 0
    lambda_init = 0.8 - 0.6 * math.exp(-0.3 * layer)
    proj = norm_proj(x.reshape(b * s, d), mix_norm_w[layer],
                     w_in[layer].astype(jnp.bfloat16)).reshape(b, s, IN_COLS)
    dt_raw = proj[..., SPLITS[1]:SPLITS[2]]
    q = proj[..., SPLITS[2]:SPLITS[3]]
    k = proj[..., SPLITS[3]:SPLITS[4]]
    v = proj[..., SPLITS[4]:]
    y_ssd = ssd_branch(proj, dt_raw, conv_w[layer], conv_b[layer], dt_bias[layer],
                       a_log[layer], d_skip[layer], ssd_norm_w[layer])
    y_diff = diff_attention(q, k, v, q_norm_w[layer], k_norm_w[layer], lambda_q1[layer],
                            lambda_k1[layer], lambda_q2[layer], lambda_k2[layer],
                            subln_w[layer], lambda_init, cos, sin)
    ycat = jnp.concatenate([y_ssd, y_diff], axis=-1).reshape(b * s, MIX_WIDTH)
    keys = peer_sub_keys[layer].reshape(N_SUBSETS, N_KEYS, PEER_HALF).astype(jnp.bfloat16)
    x2, xn, idx, gates = peer_route(x.reshape(b * s, d), ycat, w_out[layer].astype(jnp.bfloat16),
                                    ffn_norm_w[layer], peer_w_q[layer].astype(jnp.bfloat16), keys)
    out = peer_gather(xn, idx, gates, x2, pack_uv(peer_u[layer], peer_v[layer]))
    return out.reshape(b, s, d)
```

```python
import functools
import math

import jax
import jax.numpy as jnp
from jax import lax
from jax.experimental import pallas as pl
from jax.experimental.pallas import tpu as pltpu

D_MODEL = 1024
CHUNK = 64
EPS = 1e-6

SSD_WIDTH = D_MODEL // 2
SSD_HEAD_DIM = 64
SSD_HEADS = SSD_WIDTH // SSD_HEAD_DIM
SSD_GROUPS = 2
SSD_HEADS_PER_GROUP = SSD_HEADS // SSD_GROUPS
SSD_STATE = 128
CONV_WIDTH = 4
SSD_CONV_CH = SSD_WIDTH + 2 * SSD_GROUPS * SSD_STATE

DIFF_WIDTH = D_MODEL - SSD_WIDTH
DIFF_HEADS = 4
DIFF_VDIM = DIFF_WIDTH // DIFF_HEADS
DIFF_QKDIM = DIFF_VDIM // 2
ROPE_DIM = DIFF_QKDIM // 4
ROPE_THETA = 500000.0
Q_BLOCK = 128

QK_COLS = DIFF_HEADS * 2 * DIFF_QKDIM
SPLITS = (SSD_WIDTH,
          SSD_WIDTH + SSD_CONV_CH,
          SSD_WIDTH + SSD_CONV_CH + SSD_HEADS,
          SSD_WIDTH + SSD_CONV_CH + SSD_HEADS + QK_COLS,
          SSD_WIDTH + SSD_CONV_CH + SSD_HEADS + 2 * QK_COLS)
IN_COLS = SPLITS[-1] + DIFF_WIDTH
MIX_WIDTH = SSD_WIDTH + DIFF_WIDTH

PEER_HEADS = 8
N_KEYS = 128
N_EXPERTS = N_KEYS * N_KEYS
PEER_KEY_DIM = 256
PEER_HALF = PEER_KEY_DIM // 2
PEER_TOPK = 16
PEER_TOKEN_BLOCK = 128


def _norm_proj_kernel(x_ref, nw_ref, w_ref, o_ref):
    xf = x_ref[...]
    y = xf * lax.rsqrt(jnp.mean(xf * xf, axis=-1, keepdims=True) + EPS)
    y = (y * nw_ref[...]).astype(jnp.bfloat16)
    o_ref[...] = jnp.dot(y, w_ref[...], preferred_element_type=jnp.float32)


def norm_proj(x2d, norm_w, w_bf16, tm=512):
    m, d = x2d.shape
    n = w_bf16.shape[1]
    return pl.pallas_call(
        _norm_proj_kernel,
        grid=(m // tm,),
        in_specs=[
            pl.BlockSpec((tm, d), lambda i: (i, 0)),
            pl.BlockSpec((1, d), lambda i: (0, 0)),
            pl.BlockSpec((d, n), lambda i: (0, 0)),
        ],
        out_specs=pl.BlockSpec((tm, n), lambda i: (i, 0)),
        out_shape=jax.ShapeDtypeStruct((m, n), jnp.float32),
        compiler_params=pltpu.CompilerParams(
            dimension_semantics=("arbitrary",),
            vmem_limit_bytes=56 * 1024 * 1024),
        name="norm_proj",
    )(x2d, norm_w.reshape(1, d), w_bf16)


PEER_NSEL = PEER_HEADS * PEER_TOPK
PEER_TB = 8
PEER_SLOTS = 3
TOKEN_SPLIT = ((0, 1, 2), (3, 4, 5), (6, 7))
LANES = 128
SUBLANES = 8
_SQRT_HALF = 0.7071067811865476
_U_MASK = 0xFFFF0000


def _peer_gather_kernel(idx_cur, idx_mid, idx_nxt, xn_ref, g_ref, xres_ref, uv_hbm,
                        o_ref, buf, part_scr, wb_scr, sem):
    i = pl.program_id(0)
    nb = pl.num_programs(0)
    slot = i % PEER_SLOTS
    nxt = (i + PEER_SLOTS - 1) % PEER_SLOTS
    d = xn_ref.shape[-1]
    n_groups = PEER_NSEL // SUBLANES
    n_chunks = d // LANES
    half = PEER_TB // 2

    def start_row(idx_ref, s, tt, kk, j):
        e = idx_ref[tt * PEER_NSEL + kk * SUBLANES + j]
        pltpu.make_async_copy(uv_hbm.at[e, pl.ds(0, 1)], buf.at[s, tt, kk, pl.ds(j, 1)],
                              sem.at[s]).start(priority=j % 2)

    @pl.when(i == 0)
    def _():
        for s, idx_ref in ((0, idx_cur), (1, idx_mid)):
            for tt in range(PEER_TB):
                def body(kk, carry, tt=tt, s=s, idx_ref=idx_ref):
                    for j in range(SUBLANES):
                        start_row(idx_ref, s, tt, kk, j)
                    return carry
                lax.fori_loop(0, n_groups, body, 0)

    pltpu.make_async_copy(buf.at[slot], buf.at[slot], sem.at[slot]).wait()

    for tt, toks in enumerate(TOKEN_SPLIT):
        def body(kk, carry, tt=tt, toks=toks):
            r0 = pl.multiple_of(kk * SUBLANES, SUBLANES)
            for n, tok in enumerate(toks):
                urows = lax.bitcast_convert_type(buf[slot, tok, kk] & jnp.uint32(_U_MASK),
                                                 jnp.float32)
                prod = urows * xn_ref[tok:tok + 1, :]
                part = prod[:, 0:LANES]
                for c in range(1, n_chunks):
                    part = part + prod[:, c * LANES:(c + 1) * LANES]
                part_scr[tok, pl.ds(r0, SUBLANES), :] = part
                for j in range(n * SUBLANES // len(toks), (n + 1) * SUBLANES // len(toks)):
                    start_row(idx_nxt, nxt, tt, kk, j)
            return carry
        lax.fori_loop(0, n_groups, body, 0)

    for tt in (half - 1, PEER_TB - 1):
        for kk in range(n_groups):
            for j in range(SUBLANES):
                start_row(idx_nxt, nxt, tt, kk, j)

    lane = lax.broadcasted_iota(jnp.int32, (PEER_NSEL, PEER_TB), 1)
    hm = jnp.zeros((PEER_NSEL, PEER_TB), jnp.float32)
    for tok in range(PEER_TB):
        h = jnp.sum(part_scr[tok], axis=1, keepdims=True)
        hm = jnp.where(lane == tok, h, hm)
    eye = (lax.broadcasted_iota(jnp.int32, (PEER_NSEL, PEER_NSEL), 0)
           == lax.broadcasted_iota(jnp.int32, (PEER_NSEL, PEER_NSEL), 1)).astype(jnp.float32)
    g_t = lax.dot_general(eye, g_ref[...], (((1,), (1,)), ((), ())),
                          precision=lax.Precision.HIGHEST,
                          preferred_element_type=jnp.float32)
    w = g_t * (0.5 * hm * (1.0 + lax.erf(hm * _SQRT_HALF)))
    for tok in range(PEER_TB):
        wb_scr[tok] = jnp.broadcast_to(w[:, tok:tok + 1], (PEER_NSEL, LANES))

    for n_loop, toks in enumerate(TOKEN_SPLIT):
        tt = half + n_loop

        def body(kk, accs, tt=tt, toks=toks):
            r0 = pl.multiple_of(kk * SUBLANES, SUBLANES)
            out = []
            for n, tok in enumerate(toks):
                wg = wb_scr[tok, pl.ds(r0, SUBLANES), :]
                rows = lax.bitcast_convert_type(buf[slot, tok, kk] << 16, jnp.float32)
                out.append(accs[n] + rows * jnp.concatenate([wg] * n_chunks, axis=1))
                for j in range(n * SUBLANES // len(toks), (n + 1) * SUBLANES // len(toks)):
                    start_row(idx_nxt, nxt, tt, kk, j)
            return tuple(out)
        zero = jnp.zeros((SUBLANES, d), jnp.float32)
        accs = lax.fori_loop(0, n_groups, body, (zero,) * len(toks))
        for n, tok in enumerate(toks):
            o_ref[tok:tok + 1, :] = (xres_ref[tok:tok + 1, :]
                                     + jnp.sum(accs[n], axis=0, keepdims=True))

    @pl.when(i == nb - 1)
    def _():
        for s in (nxt, (i + 1) % PEER_SLOTS):
            pltpu.make_async_copy(buf.at[s], buf.at[s], sem.at[s]).wait()


def pack_uv(u, v):
    to_bits = lambda a: lax.bitcast_convert_type(a.astype(jnp.bfloat16), jnp.uint16).astype(jnp.uint32)
    packed = (to_bits(u) << 16) | to_bits(v)
    return jnp.pad(packed[:, None, :], ((0, 0), (0, SUBLANES - 1), (0, 0)))


def peer_gather(xn, idx, gates, xres, uv):
    m, d = xn.shape
    nb = m // PEER_TB
    tok = lambda i: (i, 0)
    idx_flat = idx.reshape(m * PEER_NSEL)
    return pl.pallas_call(
        _peer_gather_kernel,
        grid=(nb,),
        in_specs=[
            pl.BlockSpec((PEER_TB * PEER_NSEL,), lambda i: (i,), memory_space=pltpu.SMEM),
            pl.BlockSpec((PEER_TB * PEER_NSEL,), lambda i: (jnp.minimum(i + 1, nb - 1),),
                         memory_space=pltpu.SMEM),
            pl.BlockSpec((PEER_TB * PEER_NSEL,), lambda i: (jnp.minimum(i + 2, nb - 1),),
                         memory_space=pltpu.SMEM),
            pl.BlockSpec((PEER_TB, d), tok),
            pl.BlockSpec((PEER_TB, PEER_NSEL), tok),
            pl.BlockSpec((PEER_TB, d), tok),
            pl.BlockSpec(memory_space=pl.ANY),
        ],
        out_specs=pl.BlockSpec((PEER_TB, d), tok),
        out_shape=jax.ShapeDtypeStruct((m, d), jnp.float32),
        scratch_shapes=[
            pltpu.VMEM((PEER_SLOTS, PEER_TB, PEER_NSEL // SUBLANES, SUBLANES, d), jnp.uint32),
            pltpu.VMEM((PEER_TB, PEER_NSEL, LANES), jnp.float32),
            pltpu.VMEM((PEER_TB, PEER_NSEL, LANES), jnp.float32),
            pltpu.SemaphoreType.DMA((PEER_SLOTS,)),
        ],
        compiler_params=pltpu.CompilerParams(
            dimension_semantics=("arbitrary",),
            vmem_limit_bytes=48 * 1024 * 1024,
            disable_bounds_checks=True),
        name="peer_gather",
    )(idx_flat, idx_flat, idx_flat, xn, gates, xres, uv)


ROUTE_TM = 256
ROUTE_GROUP = LANES
N_SUBSETS = PEER_HEADS * 2


_POS_SENTINEL = 1 << 20


def _take_top(vals, pos, payload=None):
    m = jnp.max(vals, axis=0, keepdims=True)
    p = jnp.min(jnp.where(vals == m, pos, _POS_SENTINEL), axis=0, keepdims=True)
    hit = pos == p
    picked = p if payload is None else jnp.max(jnp.where(hit, payload, -1), axis=0, keepdims=True)
    return m, picked, jnp.where(hit, -jnp.inf, vals)


def _route_kernel(x_ref, y_ref, wo_ref, nw_ref, wq_ref, keys_ref,
                  xo_ref, xn_ref, idx_ref, g_ref,
                  q_scr, sh_scr, ih_scr, idx_scr, g_scr):
    xa = x_ref[...] + jnp.dot(y_ref[...].astype(jnp.bfloat16), wo_ref[...],
                              preferred_element_type=jnp.float32)
    xo_ref[...] = xa
    xn = xa * lax.rsqrt(jnp.mean(xa * xa, axis=-1, keepdims=True) + EPS) * nw_ref[...]
    xn_ref[...] = xn
    q_scr[...] = jnp.dot(xn.astype(jnp.bfloat16), wq_ref[...],
                         preferred_element_type=jnp.float32).astype(jnp.bfloat16)

    key_row = lax.broadcasted_iota(jnp.int32, (N_KEYS, ROUTE_GROUP), 0)

    n_groups = ROUTE_TM // ROUTE_GROUP
    kk = PEER_TOPK
    row8 = lax.broadcasted_iota(jnp.int32, (SUBLANES, ROUTE_GROUP), 0)

    def head_body(h, carry):
        sc = [[None, None] for _ in range(n_groups)]
        for gi in range(n_groups):
            for j in range(2):
                c0 = pl.multiple_of((2 * h + j) * PEER_HALF, PEER_HALF)
                qc = q_scr[gi * ROUTE_GROUP:(gi + 1) * ROUTE_GROUP, pl.ds(c0, PEER_HALF)]
                sc[gi][j] = lax.dot_general(keys_ref[2 * h + j], qc, (((1,), (1,)), ((), ())),
                                            preferred_element_type=jnp.float32)
        for r in range(kk):
            for gi in range(n_groups):
                for j in range(2):
                    m, ki, sc[gi][j] = _take_top(sc[gi][j], key_row)
                    sh_scr[gi, j, r:r + 1, :] = m
                    ih_scr[gi, j, r:r + 1, :] = ki

        cand, cidx = [], []
        poss = [row8, row8 + SUBLANES] + [row8 + a * kk for a in range(1, SUBLANES)]
        poss.append((row8 + SUBLANES) * kk)
        cpos = jnp.concatenate(poss, axis=0)
        for gi in range(n_groups):
            sh, ih = sh_scr.at[gi], ih_scr.at[gi]
            s1_lo, s1_hi = sh[1, 0:SUBLANES, :], sh[1, SUBLANES:kk, :]
            i1_lo, i1_hi = ih[1, 0:SUBLANES, :], ih[1, SUBLANES:kk, :]
            vals = [sh[0, 0:1, :] + s1_lo, sh[0, 0:1, :] + s1_hi]
            cids = [ih[0, 0:1, :] * N_KEYS + i1_lo, ih[0, 0:1, :] * N_KEYS + i1_hi]
            for a in range(1, SUBLANES):
                v = sh[0, a:a + 1, :] + s1_lo
                if kk // (a + 1) < SUBLANES:
                    v = jnp.where(row8 < kk // (a + 1), v, -jnp.inf)
                vals.append(v)
                cids.append(ih[0, a:a + 1, :] * N_KEYS + i1_lo)
            vals.append(sh[0, SUBLANES:kk, :] + sh[1, 0:1, :])
            cids.append(ih[0, SUBLANES:kk, :] * N_KEYS + ih[1, 0:1, :])
            cand.append(jnp.concatenate(vals, axis=0))
            cidx.append(jnp.concatenate(cids, axis=0))
        tops = [[] for _ in range(n_groups)]
        ids = [[] for _ in range(n_groups)]
        for r in range(kk):
            for gi in range(n_groups):
                m, e, cand[gi] = _take_top(cand[gi], cpos, cidx[gi])
                tops[gi].append(m)
                ids[gi].append(e)
        r0 = pl.multiple_of(h * kk, kk)
        for gi in range(n_groups):
            top_s = jnp.concatenate(tops[gi], axis=0)
            ex = jnp.exp(top_s - tops[gi][0])
            idx_scr[gi, pl.ds(r0, kk), :] = jnp.concatenate(ids[gi], axis=0)
            g_scr[gi, pl.ds(r0, kk), :] = ex / jnp.sum(ex, axis=0, keepdims=True)
        return carry
    lax.fori_loop(0, PEER_HEADS, head_body, 0)

    for gi in range(n_groups):
        idx_ref[gi * ROUTE_GROUP:(gi + 1) * ROUTE_GROUP, :] = idx_scr[gi].T
        g_ref[gi * ROUTE_GROUP:(gi + 1) * ROUTE_GROUP, :] = g_scr[gi].T


def peer_route(x2d, ycat, w_out_bf16, ffn_norm_w, w_q_bf16, keys_bf16):
    m, d = x2d.shape
    tm = ROUTE_TM
    tok = lambda i: (i, 0)
    full2 = lambda i: (0, 0)
    return pl.pallas_call(
        _route_kernel,
        grid=(m // tm,),
        in_specs=[
            pl.BlockSpec((tm, d), tok),
            pl.BlockSpec((tm, MIX_WIDTH), tok),
            pl.BlockSpec((MIX_WIDTH, d), full2),
            pl.BlockSpec((1, d), full2),
            pl.BlockSpec((d, PEER_HEADS * PEER_KEY_DIM), full2),
            pl.BlockSpec((N_SUBSETS, N_KEYS, PEER_HALF), lambda i: (0, 0, 0)),
        ],
        out_specs=[
            pl.BlockSpec((tm, d), tok),
            pl.BlockSpec((tm, d), tok),
            pl.BlockSpec((tm, PEER_NSEL), tok),
            pl.BlockSpec((tm, PEER_NSEL), tok),
        ],
        out_shape=[
            jax.ShapeDtypeStruct((m, d), jnp.float32),
            jax.ShapeDtypeStruct((m, d), jnp.float32),
            jax.ShapeDtypeStruct((m, PEER_NSEL), jnp.int32),
            jax.ShapeDtypeStruct((m, PEER_NSEL), jnp.float32),
        ],
        scratch_shapes=[
            pltpu.VMEM((tm, PEER_HEADS * PEER_KEY_DIM), jnp.bfloat16),
            pltpu.VMEM((tm // ROUTE_GROUP, 2, PEER_TOPK, ROUTE_GROUP), jnp.float32),
            pltpu.VMEM((tm // ROUTE_GROUP, 2, PEER_TOPK, ROUTE_GROUP), jnp.int32),
            pltpu.VMEM((tm // ROUTE_GROUP, PEER_NSEL, ROUTE_GROUP), jnp.int32),
            pltpu.VMEM((tm // ROUTE_GROUP, PEER_NSEL, ROUTE_GROUP), jnp.float32),
        ],
        compiler_params=pltpu.CompilerParams(
            dimension_semantics=("arbitrary",),
            vmem_limit_bytes=48 * 1024 * 1024),
        name="peer_route",
    )(x2d, ycat, w_out_bf16, ffn_norm_w.reshape(1, d), w_q_bf16, keys_bf16)


ATT_T = 256
ATT_TK = 512


def _diff_attn_kernel(lam_ref, q0_ref, q1_ref, k_ref, v_ref, sw_ref, o_ref,
                      m_scr, l_scr, acc_scr, *, out_scale):
    qb = pl.program_id(2)
    m_scr[...] = jnp.full(m_scr.shape, -jnp.inf, jnp.float32)
    l_scr[...] = jnp.zeros(l_scr.shape, jnp.float32)
    acc_scr[...] = jnp.zeros(acc_scr.shape, jnp.float32)
    n_full = (qb * ATT_T) // ATT_TK
    key_chunk = (lax.broadcasted_iota(jnp.int32, (ATT_TK, ATT_T), 0) + n_full * ATT_TK) // CHUNK
    qry_chunk = (lax.broadcasted_iota(jnp.int32, (ATT_TK, ATT_T), 1) + qb * ATT_T) // CHUNK
    diag_mask = key_chunk <= qry_chunk

    def update(kb, masked):
        k0 = pl.multiple_of(kb * ATT_TK, ATT_TK)
        ks = k_ref[0, pl.ds(k0, ATT_TK), :]
        vt = v_ref[0, :, pl.ds(k0, ATT_TK)]
        scores = [jnp.dot(ks, q_ref[0], preferred_element_type=jnp.float32)
                  for q_ref in (q0_ref, q1_ref)]
        m_old = [m_scr[0], m_scr[1]]
        l_old = [l_scr[0], l_scr[1]]
        acc_old = [acc_scr[0], acc_scr[1]]
        m_new, alpha, probs = [], [], []
        for j in range(2):
            s = jnp.where(diag_mask, scores[j], -jnp.inf) if masked else scores[j]
            m_new.append(jnp.maximum(m_old[j], jnp.max(s, axis=0, keepdims=True)))
            alpha.append(jnp.exp(m_old[j] - m_new[j]))
            probs.append(jnp.exp(s - m_new[j]))
        pv = [jnp.dot(vt, probs[j].astype(jnp.bfloat16), preferred_element_type=jnp.float32)
              for j in range(2)]
        for j in range(2):
            m_scr[j] = m_new[j]
            l_scr[j] = alpha[j] * l_old[j] + jnp.sum(probs[j], axis=0, keepdims=True)
            acc_scr[j] = alpha[j] * acc_old[j] + pv[j]

    def body(kb, carry):
        update(kb, False)
        return carry
    lax.fori_loop(0, n_full, body, 0)
    update(n_full, True)

    o = acc_scr[0] / l_scr[0] - lam_ref[0] * (acc_scr[1] / l_scr[1])
    o = o * lax.rsqrt(jnp.mean(o * o, axis=0, keepdims=True) + EPS) * sw_ref[...]
    o_ref[0] = (o * out_scale).T


def diff_attention_core(q0, q1, k, v, lam, subln_w, out_scale):
    b, s, _ = k.shape
    qspec = pl.BlockSpec((1, 2 * DIFF_QKDIM, ATT_T), lambda bi, h, i: (bi, h, i))
    kspec = pl.BlockSpec((1, s, 2 * DIFF_QKDIM), lambda bi, h, i: (bi, 0, h))
    vspec = pl.BlockSpec((1, DIFF_VDIM, s), lambda bi, h, i: (bi, h, 0))
    return pl.pallas_call(
        functools.partial(_diff_attn_kernel, out_scale=out_scale),
        grid=(b, DIFF_HEADS, s // ATT_T),
        in_specs=[
            pl.BlockSpec(memory_space=pltpu.SMEM),
            qspec, qspec, kspec, vspec,
            pl.BlockSpec((DIFF_VDIM, 1), lambda bi, h, i: (0, 0)),
        ],
        out_specs=pl.BlockSpec((1, ATT_T, DIFF_VDIM), lambda bi, h, i: (bi, i, h)),
        out_shape=jax.ShapeDtypeStruct((b, s, DIFF_WIDTH), jnp.float32),
        scratch_shapes=[
            pltpu.VMEM((2, 1, ATT_T), jnp.float32),
            pltpu.VMEM((2, 1, ATT_T), jnp.float32),
            pltpu.VMEM((2, DIFF_VDIM, ATT_T), jnp.float32),
        ],
        compiler_params=pltpu.CompilerParams(
            dimension_semantics=("arbitrary", "arbitrary", "arbitrary"),
            vmem_limit_bytes=32 * 1024 * 1024),
        name="diff_attention",
    )(lam, q0, q1, k, v, subln_w.reshape(DIFF_VDIM, 1))


SSD_T = 256
SSD_BC = 2 * SSD_GROUPS * SSD_STATE
SSD_GW = SSD_WIDTH // SSD_GROUPS
_HI = lax.Precision.HIGHEST


def _ssd_kernel(z_ref, xs_ref, bc_ref, dtc_ref, dtr_ref, cw_ref, cb_ref, dtb_c_ref, dtb_r_ref,
                al_c_ref, al_r_ref, dskip_ref, nw_ref, o_ref, xe_scr, st_scr, y_scr):
    t = SSD_T
    f32, bf16 = jnp.float32, jnp.bfloat16

    @pl.when(pl.program_id(1) == 0)
    def _():
        xe_scr[0:SUBLANES, :] = jnp.zeros((SUBLANES, SSD_CONV_CH), f32)
        st_scr[...] = jnp.zeros(st_scr.shape, f32)

    @pl.when(pl.program_id(1) > 0)
    def _():
        xe_scr[0:SUBLANES, :] = xe_scr[t:t + SUBLANES, :]

    xe_scr[SUBLANES:SUBLANES + t, 0:SSD_WIDTH] = xs_ref[0]
    xe_scr[SUBLANES:SUBLANES + t, SSD_WIDTH:SSD_CONV_CH] = bc_ref[0]
    conv = cb_ref[...]
    for w in range(CONV_WIDTH):
        r0 = SUBLANES - (CONV_WIDTH - 1) + w
        conv = conv + cw_ref[w:w + 1, :] * xe_scr[r0:r0 + t, :]
    xc = jax.nn.silu(conv)
    xs = xc[:, 0:SSD_WIDTH]

    dt_c = jax.nn.softplus(dtc_ref[0] + dtb_c_ref[...])
    dta_c = dt_c * (-jnp.exp(al_c_ref[...]))
    dta_r = jax.nn.softplus(dtr_ref[0] + dtb_r_ref[...]) * (-jnp.exp(al_r_ref[...]))
    row_i = lax.broadcasted_iota(jnp.int32, (t, t), 0)
    col_i = lax.broadcasted_iota(jnp.int32, (t, t), 1)
    same = (row_i // CHUNK) == (col_i // CHUNK)
    lower = (same & (col_i <= row_i)).astype(f32)
    upper = (same & (row_i <= col_i)).astype(f32)
    acs_c = jnp.dot(lower, dta_c, precision=_HI, preferred_element_type=f32)
    acs_r = jnp.dot(dta_r, upper, precision=_HI, preferred_element_type=f32)
    end_c = jnp.dot(same.astype(f32), dta_c, precision=_HI, preferred_element_type=f32)

    expand = (lax.broadcasted_iota(jnp.int32, (SSD_HEADS, SSD_WIDTH), 1) // SSD_HEAD_DIM
              == lax.broadcasted_iota(jnp.int32, (SSD_HEADS, SSD_WIDTH), 0)).astype(f32)
    widen = lambda a: jnp.dot(a, expand, precision=_HI, preferred_element_type=f32)
    xdt = xs * widen(dt_c)
    e_acs = widen(jnp.exp(acs_c))
    x_end = xdt * widen(jnp.exp(end_c - acs_c))
    e_end = widen(jnp.exp(end_c))

    causal = (lax.broadcasted_iota(jnp.int32, (CHUNK, CHUNK), 1)
              <= lax.broadcasted_iota(jnp.int32, (CHUNK, CHUNK), 0))
    head_of_lane = lax.broadcasted_iota(jnp.int32, (CHUNK, SSD_GW), 1) // SSD_HEAD_DIM
    nt = (((1,), (1,)), ((), ()))
    for c in range(t // CHUNK):
        r0 = c * CHUNK
        rows = slice(r0, r0 + CHUNK)
        new_state = []
        for g in range(SSD_GROUPS):
            cols = slice(g * SSD_GW, (g + 1) * SSD_GW)
            bm = xc[rows, SSD_WIDTH + g * SSD_STATE:SSD_WIDTH + (g + 1) * SSD_STATE].astype(bf16)
            cm = xc[rows, SSD_WIDTH + (SSD_GROUPS + g) * SSD_STATE:
                    SSD_WIDTH + (SSD_GROUPS + g + 1) * SSD_STATE].astype(bf16)
            cb = lax.dot_general(cm, bm, nt, preferred_element_type=f32)
            xdt_g = xdt[rows, cols]
            y = jnp.zeros((CHUNK, SSD_GW), f32)
            for hh in range(SSD_HEADS_PER_GROUP):
                h = g * SSD_HEADS_PER_GROUP + hh
                seg = acs_c[rows, h:h + 1] - acs_r[h:h + 1, r0:r0 + CHUNK]
                decay = jnp.exp(jnp.where(causal, seg, -jnp.inf))
                x_h = jnp.where(head_of_lane == hh, xdt_g, 0.0).astype(bf16)
                y = y + jnp.dot((cb * decay).astype(bf16), x_h, preferred_element_type=f32)
            y = y + e_acs[rows, cols] * jnp.dot(cm, st_scr[:, cols].astype(bf16),
                                                preferred_element_type=f32)
            y_scr[rows, cols] = y
            new_state.append(lax.dot_general(bm, x_end[rows, cols].astype(bf16),
                                             (((0,), (0,)), ((), ())),
                                             preferred_element_type=f32))
        st_scr[...] = st_scr[...] * e_end[r0:r0 + 1, :] + jnp.concatenate(new_state, axis=1)

    y = y_scr[...] + xs * dskip_ref[...]
    y = y * jax.nn.silu(z_ref[0])
    for g in range(SSD_GROUPS):
        cols = slice(g * SSD_GW, (g + 1) * SSD_GW)
        yg = y[:, cols]
        yg = yg * lax.rsqrt(jnp.mean(yg * yg, axis=-1, keepdims=True) + EPS)
        o_ref[0, :, cols] = yg * nw_ref[:, cols]


def ssd_branch(proj, dt_raw, conv_w, conv_b, dt_bias, a_log, d_skip, norm_w):
    b, s, _ = proj.shape
    h = SSD_HEADS
    w = SSD_WIDTH
    blk = lambda col: pl.BlockSpec((1, SSD_T, w), lambda bi, j, col=col: (bi, j, col))
    const = lambda shape: pl.BlockSpec(shape, lambda bi, j: (0,) * len(shape))
    return pl.pallas_call(
        _ssd_kernel,
        grid=(b, s // SSD_T),
        in_specs=[
            blk(0), blk(1), blk(2),
            pl.BlockSpec((1, SSD_T, h), lambda bi, j: (bi, j, 0)),
            pl.BlockSpec((1, h, SSD_T), lambda bi, j: (bi, 0, j)),
            const((CONV_WIDTH, SSD_CONV_CH)), const((1, SSD_CONV_CH)),
            const((1, h)), const((h, 1)), const((1, h)), const((h, 1)),
            const((1, w)), const((1, w)),
        ],
        out_specs=pl.BlockSpec((1, SSD_T, w), lambda bi, j: (bi, j, 0)),
        out_shape=jax.ShapeDtypeStruct((b, s, w), jnp.float32),
        scratch_shapes=[
            pltpu.VMEM((SUBLANES + SSD_T, SSD_CONV_CH), jnp.float32),
            pltpu.VMEM((SSD_STATE, w), jnp.float32),
            pltpu.VMEM((SSD_T, w), jnp.float32),
        ],
        compiler_params=pltpu.CompilerParams(
            dimension_semantics=("arbitrary", "arbitrary"),
            vmem_limit_bytes=32 * 1024 * 1024),
        name="ssd_branch",
    )(proj, proj, proj, dt_raw, jnp.swapaxes(dt_raw, 1, 2), conv_w, conv_b.reshape(1, -1),
      dt_bias.reshape(1, h), dt_bias.reshape(h, 1), a_log.reshape(1, h), a_log.reshape(h, 1),
      jnp.repeat(d_skip, SSD_HEAD_DIM).reshape(1, w), norm_w.reshape(1, w))


def rmsnorm(x, w):
    xf = x.astype(jnp.float32)
    y = xf * lax.rsqrt(jnp.mean(xf * xf, axis=-1, keepdims=True) + EPS)
    return (y * w.astype(jnp.float32)).astype(x.dtype)


def rotary_tables(seq):
    inv = jnp.power(ROPE_THETA, -jnp.arange(0, ROPE_DIM, 2, dtype=jnp.float32) / ROPE_DIM)
    ang = jnp.arange(seq, dtype=jnp.float32)[:, None] * inv[None, :]
    return jnp.cos(ang), jnp.sin(ang)


def partial_rotary(x, cos, sin):
    half = ROPE_DIM // 2
    c = cos[:, None, None, :]
    s = sin[:, None, None, :]
    xf = x.astype(jnp.float32)
    x1, x2, xp = xf[..., :half], xf[..., half:ROPE_DIM], xf[..., ROPE_DIM:]
    out = jnp.concatenate([x1 * c - x2 * s, x1 * s + x2 * c, xp], axis=-1)
    return out.astype(x.dtype)


def diff_attention(q, k, v, qn_w, kn_w, lq1, lk1, lq2, lk2, subln_w, lambda_init, cos, sin):
    f32 = jnp.float32
    b, s, _ = q.shape
    q = q.reshape(b, s, DIFF_HEADS, 2, DIFF_QKDIM)
    k = k.reshape(b, s, DIFF_HEADS, 2, DIFF_QKDIM)
    q = partial_rotary(rmsnorm(q, qn_w), cos, sin).astype(f32)
    kf = partial_rotary(rmsnorm(k, kn_w), cos, sin).astype(f32)
    lam = (jnp.exp(jnp.sum(lq1.astype(f32) * lk1.astype(f32)))
           - jnp.exp(jnp.sum(lq2.astype(f32) * lk2.astype(f32))) + lambda_init)
    scale = DIFF_QKDIM ** -0.5
    branch = jnp.arange(2)
    to_cols = lambda a: jnp.swapaxes(a.reshape(b, s, -1).astype(jnp.bfloat16), 1, 2)
    q0 = to_cols(q * scale * (branch == 0)[:, None])
    q1 = to_cols(q * scale * (branch == 1)[:, None])
    kb = kf.reshape(b, s, QK_COLS).astype(jnp.bfloat16)
    vb = to_cols(v)
    return diff_attention_core(q0, q1, kb, vb, lam.reshape(1), subln_w, 1.0 - lambda_init)


def kernel(x, mix_norm_w, w_in, conv_w, conv_b, dt_bias, a_log, d_skip, ssd_norm_w, q_norm_w, k_norm_w, lambda_q1, lambda_k1, lambda_q2, lambda_k2, subln_w, w_out, ffn_norm_w, peer_w_q, peer_sub_keys, peer_u, peer_v):
    b, s, d = x.shape
    cos, sin = rotary_tables(s)
    layer = 0
    lambda_init = 0.8 - 0.6 * math.exp(-0.3 * layer)
    proj = norm_proj(x.reshape(b * s, d), mix_norm_w[layer],
                     w_in[layer].astype(jnp.bfloat16)).reshape(b, s, IN_COLS)
    dt_raw = proj[..., SPLITS[1]:SPLITS[2]]
    q = proj[..., SPLITS[2]:SPLITS[3]]
    k = proj[..., SPLITS[3]:SPLITS[4]]
    v = proj[..., SPLITS[4]:]
    y_ssd = ssd_branch(proj, dt_raw, conv_w[layer], conv_b[layer], dt_bias[layer],
                       a_log[layer], d_skip[layer], ssd_norm_w[layer])
    y_diff = diff_attention(q, k, v, q_norm_w[layer], k_norm_w[layer], lambda_q1[layer],
                            lambda_k1[layer], lambda_q2[layer], lambda_k2[layer],
                            subln_w[layer], lambda_init, cos, sin)
    ycat = jnp.concatenate([y_ssd, y_diff], axis=-1).reshape(b * s, MIX_WIDTH)
    keys = peer_sub_keys[layer].reshape(N_SUBSETS, N_KEYS, PEER_HALF).astype(jnp.bfloat16)
    x2, xn, idx, gates = peer_route(x.reshape(b * s, d), ycat, w_out[layer].astype(jnp.bfloat16),
                                    ffn_norm_w[layer], peer_w_q[layer].astype(jnp.bfloat16), keys)
    out = peer_gather(xn, idx, gates, x2, pack_uv(peer_u[layer], peer_v[layer]))
    return out.reshape(b, s, d)
```

```python
import functools
import math

import jax
import jax.numpy as jnp
from jax import lax
from jax.experimental import pallas as pl
from jax.experimental.pallas import tpu as pltpu

D_MODEL = 1024
CHUNK = 64
EPS = 1e-6

SSD_WIDTH = D_MODEL // 2
SSD_HEAD_DIM = 64
SSD_HEADS = SSD_WIDTH // SSD_HEAD_DIM
SSD_GROUPS = 2
SSD_HEADS_PER_GROUP = SSD_HEADS // SSD_GROUPS
SSD_STATE = 128
CONV_WIDTH = 4
SSD_CONV_CH = SSD_WIDTH + 2 * SSD_GROUPS * SSD_STATE

DIFF_WIDTH = D_MODEL - SSD_WIDTH
DIFF_HEADS = 4
DIFF_VDIM = DIFF_WIDTH // DIFF_HEADS
DIFF_QKDIM = DIFF_VDIM // 2
ROPE_DIM = DIFF_QKDIM // 4
ROPE_THETA = 500000.0
Q_BLOCK = 128

QK_COLS = DIFF_HEADS * 2 * DIFF_QKDIM
SPLITS = (SSD_WIDTH,
          SSD_WIDTH + SSD_CONV_CH,
          SSD_WIDTH + SSD_CONV_CH + SSD_HEADS,
          SSD_WIDTH + SSD_CONV_CH + SSD_HEADS + QK_COLS,
          SSD_WIDTH + SSD_CONV_CH + SSD_HEADS + 2 * QK_COLS)
IN_COLS = SPLITS[-1] + DIFF_WIDTH
MIX_WIDTH = SSD_WIDTH + DIFF_WIDTH

PEER_HEADS = 8
N_KEYS = 128
N_EXPERTS = N_KEYS * N_KEYS
PEER_KEY_DIM = 256
PEER_HALF = PEER_KEY_DIM // 2
PEER_TOPK = 16
PEER_TOKEN_BLOCK = 128


def _norm_proj_kernel(x_ref, nw_ref, w_ref, o_ref):
    xf = x_ref[...]
    y = xf * lax.rsqrt(jnp.mean(xf * xf, axis=-1, keepdims=True) + EPS)
    y = (y * nw_ref[...]).astype(jnp.bfloat16)
    o_ref[...] = jnp.dot(y, w_ref[...], preferred_element_type=jnp.float32)


def norm_proj(x2d, norm_w, w_bf16, tm=512):
    m, d = x2d.shape
    n = w_bf16.shape[1]
    return pl.pallas_call(
        _norm_proj_kernel,
        grid=(m // tm,),
        in_specs=[
            pl.BlockSpec((tm, d), lambda i: (i, 0)),
            pl.BlockSpec((1, d), lambda i: (0, 0)),
            pl.BlockSpec((d, n), lambda i: (0, 0)),
        ],
        out_specs=pl.BlockSpec((tm, n), lambda i: (i, 0)),
        out_shape=jax.ShapeDtypeStruct((m, n), jnp.float32),
        compiler_params=pltpu.CompilerParams(
            dimension_semantics=("arbitrary",),
            vmem_limit_bytes=56 * 1024 * 1024),
        name="norm_proj",
    )(x2d, norm_w.reshape(1, d), w_bf16)


PEER_NSEL = PEER_HEADS * PEER_TOPK
PEER_TB = 8
PEER_SLOTS = 3
TOKEN_SPLIT = ((0, 1, 2), (3, 4, 5), (6, 7))
LANES = 128
SUBLANES = 8
_SQRT_HALF = 0.7071067811865476
_U_MASK = 0xFFFF0000


def _peer_gather_kernel(idx_cur, idx_mid, idx_nxt, xn_ref, g_ref, xres_ref, uv_hbm,
                        o_ref, buf, part_scr, wb_scr, sem):
    i = pl.program_id(0)
    nb = pl.num_programs(0)
    slot = i % PEER_SLOTS
    nxt = (i + PEER_SLOTS - 1) % PEER_SLOTS
    d = xn_ref.shape[-1]
    n_groups = PEER_NSEL // SUBLANES
    n_chunks = d // LANES
    half = PEER_TB // 2

    def start_row(idx_ref, s, tt, kk, j):
        e = idx_ref[tt * PEER_NSEL + kk * SUBLANES + j]
        pltpu.make_async_copy(uv_hbm.at[e, pl.ds(0, 1)], buf.at[s, tt, kk, pl.ds(j, 1)],
                              sem.at[s]).start(priority=j % 2)

    @pl.when(i == 0)
    def _():
        for s, idx_ref in ((0, idx_cur), (1, idx_mid)):
            for tt in range(PEER_TB):
                def body(kk, carry, tt=tt, s=s, idx_ref=idx_ref):
                    for j in range(SUBLANES):
                        start_row(idx_ref, s, tt, kk, j)
                    return carry
                lax.fori_loop(0, n_groups, body, 0)

    pltpu.make_async_copy(buf.at[slot], buf.at[slot], sem.at[slot]).wait()

    for tt, toks in enumerate(TOKEN_SPLIT):
        def body(kk, carry, tt=tt, toks=toks):
            r0 = pl.multiple_of(kk * SUBLANES, SUBLANES)
            for n, tok in enumerate(toks):
                urows = lax.bitcast_convert_type(buf[slot, tok, kk] & jnp.uint32(_U_MASK),
                                                 jnp.float32)
                prod = urows * xn_ref[tok:tok + 1, :]
                part = prod[:, 0:LANES]
                for c in range(1, n_chunks):
                    part = part + prod[:, c * LANES:(c + 1) * LANES]
                part_scr[tok, pl.ds(r0, SUBLANES), :] = part
                for j in range(n * SUBLANES // len(toks), (n + 1) * SUBLANES // len(toks)):
                    start_row(idx_nxt, nxt, tt, kk, j)
            return carry
        lax.fori_loop(0, n_groups, body, 0)

    for tt in (half - 1, PEER_TB - 1):
        for kk in range(n_groups):
            for j in range(SUBLANES):
                start_row(idx_nxt, nxt, tt, kk, j)

    lane = lax.broadcasted_iota(jnp.int32, (PEER_NSEL, PEER_TB), 1)
    hm = jnp.zeros((PEER_NSEL, PEER_TB), jnp.float32)
    for tok in range(PEER_TB):
        h = jnp.sum(part_scr[tok], axis=1, keepdims=True)
        hm = jnp.where(lane == tok, h, hm)
    eye = (lax.broadcasted_iota(jnp.int32, (PEER_NSEL, PEER_NSEL), 0)
           == lax.broadcasted_iota(jnp.int32, (PEER_NSEL, PEER_NSEL), 1)).astype(jnp.float32)
    g_t = lax.dot_general(eye, g_ref[...], (((1,), (1,)), ((), ())),
                          precision=lax.Precision.HIGHEST,
                          preferred_element_type=jnp.float32)
    w = g_t * (0.5 * hm * (1.0 + lax.erf(hm * _SQRT_HALF)))
    for tok in range(PEER_TB):
        wb_scr[tok] = jnp.broadcast_to(w[:, tok:tok + 1], (PEER_NSEL, LANES))

    for n_loop, toks in enumerate(TOKEN_SPLIT):
        tt = half + n_loop

        def body(kk, accs, tt=tt, toks=toks):
            r0 = pl.multiple_of(kk * SUBLANES, SUBLANES)
            out = []
            for n, tok in enumerate(toks):
                wg = wb_scr[tok, pl.ds(r0, SUBLANES), :]
                rows = lax.bitcast_convert_type(buf[slot, tok, kk] << 16, jnp.float32)
                out.append(accs[n] + rows * jnp.concatenate([wg] * n_chunks, axis=1))
                for j in range(n * SUBLANES // len(toks), (n + 1) * SUBLANES // len(toks)):
                    start_row(idx_nxt, nxt, tt, kk, j)
            return tuple(out)
        zero = jnp.zeros((SUBLANES, d), jnp.float32)
        accs = lax.fori_loop(0, n_groups, body, (zero,) * len(toks))
        for n, tok in enumerate(toks):
            o_ref[tok:tok + 1, :] = (xres_ref[tok:tok + 1, :]
                                     + jnp.sum(accs[n], axis=0, keepdims=True))

    @pl.when(i == nb - 1)
    def _():
        for s in (nxt, (i + 1) % PEER_SLOTS):
            pltpu.make_async_copy(buf.at[s], buf.at[s], sem.at[s]).wait()


def _peer_gather_static_kernel(idx_cur, idx_mid, idx_nxt, xn_ref, g_ref, xres_ref, uv_hbm,
                               o_ref, buf, part_scr, wb_scr, sem):
    i = pl.program_id(0)
    nb = pl.num_programs(0)
    d = xn_ref.shape[-1]
    n_groups = PEER_NSEL // SUBLANES
    n_chunks = d // LANES
    per_pair = 2 * PEER_TB * n_groups
    rows_per_unit = PEER_TB * PEER_NSEL // per_pair

    def start_row(idx_ref, s, r):
        tt, k = divmod(r, PEER_NSEL)
        e = idx_ref[r]
        pltpu.make_async_copy(uv_hbm.at[e, pl.ds(0, 1)],
                              buf.at[s, tt, k // SUBLANES, pl.ds(k % SUBLANES, 1)],
                              sem.at[s]).start(priority=r % 2)

    @pl.when(i == 0)
    def _():
        for s, idx_ref in ((0, idx_cur), (1, idx_mid)):
            for tt in range(PEER_TB):
                def body(kk, carry, tt=tt, s=s, idx_ref=idx_ref):
                    for j in range(SUBLANES):
                        e = idx_ref[tt * PEER_NSEL + kk * SUBLANES + j]
                        pltpu.make_async_copy(uv_hbm.at[e, pl.ds(0, 1)],
                                              buf.at[s, tt, kk, pl.ds(j, 1)], sem.at[s]).start()
                    return carry
                lax.fori_loop(0, n_groups, body, 0)

    def step(slot):
        nxt = (slot + 2) % PEER_SLOTS
        pltpu.make_async_copy(buf.at[slot], buf.at[slot], sem.at[slot]).wait()
        unit = 0
        for tok in range(PEER_TB):
            for kk in range(n_groups):
                urows = lax.bitcast_convert_type(buf[slot, tok, kk] & jnp.uint32(_U_MASK),
                                                 jnp.float32)
                prod = urows * xn_ref[tok:tok + 1, :]
                part = prod[:, 0:LANES]
                for c in range(1, n_chunks):
                    part = part + prod[:, c * LANES:(c + 1) * LANES]
                part_scr[tok, kk * SUBLANES:(kk + 1) * SUBLANES, :] = part
                for r in range(unit * rows_per_unit, (unit + 1) * rows_per_unit):
                    start_row(idx_nxt, nxt, r)
                unit += 1

        lane = lax.broadcasted_iota(jnp.int32, (PEER_NSEL, PEER_TB), 1)
        hm = jnp.zeros((PEER_NSEL, PEER_TB), jnp.float32)
        for tok in range(PEER_TB):
            h = jnp.sum(part_scr[tok], axis=1, keepdims=True)
            hm = jnp.where(lane == tok, h, hm)
        eye = (lax.broadcasted_iota(jnp.int32, (PEER_NSEL, PEER_NSEL), 0)
               == lax.broadcasted_iota(jnp.int32, (PEER_NSEL, PEER_NSEL), 1)).astype(jnp.float32)
        g_t = lax.dot_general(eye, g_ref[...], (((1,), (1,)), ((), ())),
                              precision=lax.Precision.HIGHEST,
                              preferred_element_type=jnp.float32)
        w = g_t * (0.5 * hm * (1.0 + lax.erf(hm * _SQRT_HALF)))
        for tok in range(PEER_TB):
            wb_scr[tok] = jnp.broadcast_to(w[:, tok:tok + 1], (PEER_NSEL, LANES))

        for tok in range(PEER_TB):
            acc = jnp.zeros((SUBLANES, d), jnp.float32)
            for kk in range(n_groups):
                wg = wb_scr[tok, kk * SUBLANES:(kk + 1) * SUBLANES, :]
                rows = lax.bitcast_convert_type(buf[slot, tok, kk] << 16, jnp.float32)
                acc = acc + rows * jnp.concatenate([wg] * n_chunks, axis=1)
                for r in range(unit * rows_per_unit, (unit + 1) * rows_per_unit):
                    start_row(idx_nxt, nxt, r)
                unit += 1
            o_ref[tok:tok + 1, :] = xres_ref[tok:tok + 1, :] + jnp.sum(acc, axis=0, keepdims=True)

    for slot in range(PEER_SLOTS):
        @pl.when(i % PEER_SLOTS == slot)
        def _(slot=slot):
            step(slot)

    @pl.when(i == nb - 1)
    def _():
        for k in (1, 2):
            s = (i + k) % PEER_SLOTS
            pltpu.make_async_copy(buf.at[s], buf.at[s], sem.at[s]).wait()


def pack_uv(u, v):
    to_bits = lambda a: lax.bitcast_convert_type(a.astype(jnp.bfloat16), jnp.uint16).astype(jnp.uint32)
    packed = (to_bits(u) << 16) | to_bits(v)
    return jnp.pad(packed[:, None, :], ((0, 0), (0, SUBLANES - 1), (0, 0)))


def peer_gather(xn, idx, gates, xres, uv):
    m, d = xn.shape
    nb = m // PEER_TB
    tok = lambda i: (i, 0)
    idx_flat = idx.reshape(m * PEER_NSEL)
    return pl.pallas_call(
        _peer_gather_static_kernel,
        grid=(nb,),
        in_specs=[
            pl.BlockSpec((PEER_TB * PEER_NSEL,), lambda i: (i,), memory_space=pltpu.SMEM),
            pl.BlockSpec((PEER_TB * PEER_NSEL,), lambda i: (jnp.minimum(i + 1, nb - 1),),
                         memory_space=pltpu.SMEM),
            pl.BlockSpec((PEER_TB * PEER_NSEL,), lambda i: (jnp.minimum(i + 2, nb - 1),),
                         memory_space=pltpu.SMEM),
            pl.BlockSpec((PEER_TB, d), tok),
            pl.BlockSpec((PEER_TB, PEER_NSEL), tok),
            pl.BlockSpec((PEER_TB, d), tok),
            pl.BlockSpec(memory_space=pl.ANY),
        ],
        out_specs=pl.BlockSpec((PEER_TB, d), tok),
        out_shape=jax.ShapeDtypeStruct((m, d), jnp.float32),
        scratch_shapes=[
            pltpu.VMEM((PEER_SLOTS, PEER_TB, PEER_NSEL // SUBLANES, SUBLANES, d), jnp.uint32),
            pltpu.VMEM((PEER_TB, PEER_NSEL, LANES), jnp.float32),
            pltpu.VMEM((PEER_TB, PEER_NSEL, LANES), jnp.float32),
            pltpu.SemaphoreType.DMA((PEER_SLOTS,)),
        ],
        compiler_params=pltpu.CompilerParams(
            dimension_semantics=("arbitrary",),
            vmem_limit_bytes=48 * 1024 * 1024,
            disable_bounds_checks=True),
        name="peer_gather",
    )(idx_flat, idx_flat, idx_flat, xn, gates, xres, uv)


ROUTE_TM = 256
ROUTE_GROUP = LANES
N_SUBSETS = PEER_HEADS * 2


_POS_SENTINEL = 1 << 20


def _take_top(vals, pos, payload=None):
    m = jnp.max(vals, axis=0, keepdims=True)
    p = jnp.min(jnp.where(vals == m, pos, _POS_SENTINEL), axis=0, keepdims=True)
    hit = pos == p
    picked = p if payload is None else jnp.max(jnp.where(hit, payload, -1), axis=0, keepdims=True)
    return m, picked, jnp.where(hit, -jnp.inf, vals)


def _route_kernel(x_ref, y_ref, wo_ref, nw_ref, wq_ref, keys_ref,
                  xo_ref, xn_ref, idx_ref, g_ref,
                  q_scr, sh_scr, ih_scr, idx_scr, g_scr):
    xa = x_ref[...] + jnp.dot(y_ref[...].astype(jnp.bfloat16), wo_ref[...],
                              preferred_element_type=jnp.float32)
    xo_ref[...] = xa
    xn = xa * lax.rsqrt(jnp.mean(xa * xa, axis=-1, keepdims=True) + EPS) * nw_ref[...]
    xn_ref[...] = xn
    q_scr[...] = jnp.dot(xn.astype(jnp.bfloat16), wq_ref[...],
                         preferred_element_type=jnp.float32).astype(jnp.bfloat16)

    key_row = lax.broadcasted_iota(jnp.int32, (N_KEYS, ROUTE_GROUP), 0)

    n_groups = ROUTE_TM // ROUTE_GROUP
    kk = PEER_TOPK
    row8 = lax.broadcasted_iota(jnp.int32, (SUBLANES, ROUTE_GROUP), 0)

    def head_body(h, carry):
        sc = [[None, None] for _ in range(n_groups)]
        for gi in range(n_groups):
            for j in range(2):
                c0 = pl.multiple_of((2 * h + j) * PEER_HALF, PEER_HALF)
                qc = q_scr[gi * ROUTE_GROUP:(gi + 1) * ROUTE_GROUP, pl.ds(c0, PEER_HALF)]
                sc[gi][j] = lax.dot_general(keys_ref[2 * h + j], qc, (((1,), (1,)), ((), ())),
                                            preferred_element_type=jnp.float32)
        for r in range(kk):
            for gi in range(n_groups):
                for j in range(2):
                    m, ki, sc[gi][j] = _take_top(sc[gi][j], key_row)
                    sh_scr[gi, j, r:r + 1, :] = m
                    ih_scr[gi, j, r:r + 1, :] = ki

        cand, cidx = [], []
        poss = [row8, row8 + SUBLANES] + [row8 + a * kk for a in range(1, SUBLANES)]
        poss.append((row8 + SUBLANES) * kk)
        cpos = jnp.concatenate(poss, axis=0)
        for gi in range(n_groups):
            sh, ih = sh_scr.at[gi], ih_scr.at[gi]
            s1_lo, s1_hi = sh[1, 0:SUBLANES, :], sh[1, SUBLANES:kk, :]
            i1_lo, i1_hi = ih[1, 0:SUBLANES, :], ih[1, SUBLANES:kk, :]
            vals = [sh[0, 0:1, :] + s1_lo, sh[0, 0:1, :] + s1_hi]
            cids = [ih[0, 0:1, :] * N_KEYS + i1_lo, ih[0, 0:1, :] * N_KEYS + i1_hi]
            for a in range(1, SUBLANES):
                v = sh[0, a:a + 1, :] + s1_lo
                if kk // (a + 1) < SUBLANES:
                    v = jnp.where(row8 < kk // (a + 1), v, -jnp.inf)
                vals.append(v)
                cids.append(ih[0, a:a + 1, :] * N_KEYS + i1_lo)
            vals.append(sh[0, SUBLANES:kk, :] + sh[1, 0:1, :])
            cids.append(ih[0, SUBLANES:kk, :] * N_KEYS + ih[1, 0:1, :])
            cand.append(jnp.concatenate(vals, axis=0))
            cidx.append(jnp.concatenate(cids, axis=0))
        tops = [[] for _ in range(n_groups)]
        ids = [[] for _ in range(n_groups)]
        for r in range(kk):
            for gi in range(n_groups):
                m, e, cand[gi] = _take_top(cand[gi], cpos, cidx[gi])
                tops[gi].append(m)
                ids[gi].append(e)
        r0 = pl.multiple_of(h * kk, kk)
        for gi in range(n_groups):
            top_s = jnp.concatenate(tops[gi], axis=0)
            ex = jnp.exp(top_s - tops[gi][0])
            idx_scr[gi, pl.ds(r0, kk), :] = jnp.concatenate(ids[gi], axis=0)
            g_scr[gi, pl.ds(r0, kk), :] = ex / jnp.sum(ex, axis=0, keepdims=True)
        return carry
    lax.fori_loop(0, PEER_HEADS, head_body, 0)

    for gi in range(n_groups):
        idx_ref[gi * ROUTE_GROUP:(gi + 1) * ROUTE_GROUP, :] = idx_scr[gi].T
        g_ref[gi * ROUTE_GROUP:(gi + 1) * ROUTE_GROUP, :] = g_scr[gi].T


def peer_route(x2d, ycat, w_out_bf16, ffn_norm_w, w_q_bf16, keys_bf16):
    m, d = x2d.shape
    tm = ROUTE_TM
    tok = lambda i: (i, 0)
    full2 = lambda i: (0, 0)
    return pl.pallas_call(
        _route_kernel,
        grid=(m // tm,),
        in_specs=[
            pl.BlockSpec((tm, d), tok),
            pl.BlockSpec((tm, MIX_WIDTH), tok),
            pl.BlockSpec((MIX_WIDTH, d), full2),
            pl.BlockSpec((1, d), full2),
            pl.BlockSpec((d, PEER_HEADS * PEER_KEY_DIM), full2),
            pl.BlockSpec((N_SUBSETS, N_KEYS, PEER_HALF), lambda i: (0, 0, 0)),
        ],
        out_specs=[
            pl.BlockSpec((tm, d), tok),
            pl.BlockSpec((tm, d), tok),
            pl.BlockSpec((tm, PEER_NSEL), tok),
            pl.BlockSpec((tm, PEER_NSEL), tok),
        ],
        out_shape=[
            jax.ShapeDtypeStruct((m, d), jnp.float32),
            jax.ShapeDtypeStruct((m, d), jnp.float32),
            jax.ShapeDtypeStruct((m, PEER_NSEL), jnp.int32),
            jax.ShapeDtypeStruct((m, PEER_NSEL), jnp.float32),
        ],
        scratch_shapes=[
            pltpu.VMEM((tm, PEER_HEADS * PEER_KEY_DIM), jnp.bfloat16),
            pltpu.VMEM((tm // ROUTE_GROUP, 2, PEER_TOPK, ROUTE_GROUP), jnp.float32),
            pltpu.VMEM((tm // ROUTE_GROUP, 2, PEER_TOPK, ROUTE_GROUP), jnp.int32),
            pltpu.VMEM((tm // ROUTE_GROUP, PEER_NSEL, ROUTE_GROUP), jnp.int32),
            pltpu.VMEM((tm // ROUTE_GROUP, PEER_NSEL, ROUTE_GROUP), jnp.float32),
        ],
        compiler_params=pltpu.CompilerParams(
            dimension_semantics=("arbitrary",),
            vmem_limit_bytes=48 * 1024 * 1024),
        name="peer_route",
    )(x2d, ycat, w_out_bf16, ffn_norm_w.reshape(1, d), w_q_bf16, keys_bf16)


ATT_T = 256
ATT_TK = 512


def _diff_attn_kernel(lam_ref, q0_ref, q1_ref, k_ref, v_ref, sw_ref, o_ref,
                      m_scr, l_scr, acc_scr, *, out_scale):
    qb = pl.program_id(2)
    m_scr[...] = jnp.full(m_scr.shape, -jnp.inf, jnp.float32)
    l_scr[...] = jnp.zeros(l_scr.shape, jnp.float32)
    acc_scr[...] = jnp.zeros(acc_scr.shape, jnp.float32)
    n_full = (qb * ATT_T) // ATT_TK
    key_chunk = (lax.broadcasted_iota(jnp.int32, (ATT_TK, ATT_T), 0) + n_full * ATT_TK) // CHUNK
    qry_chunk = (lax.broadcasted_iota(jnp.int32, (ATT_TK, ATT_T), 1) + qb * ATT_T) // CHUNK
    diag_mask = key_chunk <= qry_chunk

    def update(kb, masked):
        k0 = pl.multiple_of(kb * ATT_TK, ATT_TK)
        ks = k_ref[0, pl.ds(k0, ATT_TK), :]
        vt = v_ref[0, :, pl.ds(k0, ATT_TK)]
        scores = [jnp.dot(ks, q_ref[0], preferred_element_type=jnp.float32)
                  for q_ref in (q0_ref, q1_ref)]
        m_old = [m_scr[0], m_scr[1]]
        l_old = [l_scr[0], l_scr[1]]
        acc_old = [acc_scr[0], acc_scr[1]]
        m_new, alpha, probs = [], [], []
        for j in range(2):
            s = jnp.where(diag_mask, scores[j], -jnp.inf) if masked else scores[j]
            m_new.append(jnp.maximum(m_old[j], jnp.max(s, axis=0, keepdims=True)))
            alpha.append(jnp.exp(m_old[j] - m_new[j]))
            probs.append(jnp.exp(s - m_new[j]))
        pv = [jnp.dot(vt, probs[j].astype(jnp.bfloat16), preferred_element_type=jnp.float32)
              for j in range(2)]
        for j in range(2):
            m_scr[j] = m_new[j]
            l_scr[j] = alpha[j] * l_old[j] + jnp.sum(probs[j], axis=0, keepdims=True)
            acc_scr[j] = alpha[j] * acc_old[j] + pv[j]

    def body(kb, carry):
        update(kb, False)
        return carry
    lax.fori_loop(0, n_full, body, 0)
    update(n_full, True)

    o = acc_scr[0] / l_scr[0] - lam_ref[0] * (acc_scr[1] / l_scr[1])
    o = o * lax.rsqrt(jnp.mean(o * o, axis=0, keepdims=True) + EPS) * sw_ref[...]
    o_ref[0] = (o * out_scale).T


def diff_attention_core(q0, q1, k, v, lam, subln_w, out_scale):
    b, s, _ = k.shape
    qspec = pl.BlockSpec((1, 2 * DIFF_QKDIM, ATT_T), lambda bi, h, i: (bi, h, i))
    kspec = pl.BlockSpec((1, s, 2 * DIFF_QKDIM), lambda bi, h, i: (bi, 0, h))
    vspec = pl.BlockSpec((1, DIFF_VDIM, s), lambda bi, h, i: (bi, h, 0))
    return pl.pallas_call(
        functools.partial(_diff_attn_kernel, out_scale=out_scale),
        grid=(b, DIFF_HEADS, s // ATT_T),
        in_specs=[
            pl.BlockSpec(memory_space=pltpu.SMEM),
            qspec, qspec, kspec, vspec,
            pl.BlockSpec((DIFF_VDIM, 1), lambda bi, h, i: (0, 0)),
        ],
        out_specs=pl.BlockSpec((1, ATT_T, DIFF_VDIM), lambda bi, h, i: (bi, i, h)),
        out_shape=jax.ShapeDtypeStruct((b, s, DIFF_WIDTH), jnp.float32),
        scratch_shapes=[
            pltpu.VMEM((2, 1, ATT_T), jnp.float32),
            pltpu.VMEM((2, 1, ATT_T), jnp.float32),
            pltpu.VMEM((2, DIFF_VDIM, ATT_T), jnp.float32),
        ],
        compiler_params=pltpu.CompilerParams(
            dimension_semantics=("arbitrary", "arbitrary", "arbitrary"),
            vmem_limit_bytes=32 * 1024 * 1024),
        name="diff_attention",
    )(lam, q0, q1, k, v, subln_w.reshape(DIFF_VDIM, 1))


SSD_T = 256
SSD_BC = 2 * SSD_GROUPS * SSD_STATE
SSD_GW = SSD_WIDTH // SSD_GROUPS
_HI = lax.Precision.HIGHEST


def _ssd_kernel(z_ref, xs_ref, bc_ref, dtc_ref, dtr_ref, cw_ref, cb_ref, dtb_c_ref, dtb_r_ref,
                al_c_ref, al_r_ref, dskip_ref, nw_ref, o_ref, xe_scr, st_scr, y_scr):
    t = SSD_T
    f32, bf16 = jnp.float32, jnp.bfloat16

    @pl.when(pl.program_id(1) == 0)
    def _():
        xe_scr[0:SUBLANES, :] = jnp.zeros((SUBLANES, SSD_CONV_CH), f32)
        st_scr[...] = jnp.zeros(st_scr.shape, f32)

    @pl.when(pl.program_id(1) > 0)
    def _():
        xe_scr[0:SUBLANES, :] = xe_scr[t:t + SUBLANES, :]

    xe_scr[SUBLANES:SUBLANES + t, 0:SSD_WIDTH] = xs_ref[0]
    xe_scr[SUBLANES:SUBLANES + t, SSD_WIDTH:SSD_CONV_CH] = bc_ref[0]
    conv = cb_ref[...]
    for w in range(CONV_WIDTH):
        r0 = SUBLANES - (CONV_WIDTH - 1) + w
        conv = conv + cw_ref[w:w + 1, :] * xe_scr[r0:r0 + t, :]
    xc = jax.nn.silu(conv)
    xs = xc[:, 0:SSD_WIDTH]

    dt_c = jax.nn.softplus(dtc_ref[0] + dtb_c_ref[...])
    dta_c = dt_c * (-jnp.exp(al_c_ref[...]))
    dta_r = jax.nn.softplus(dtr_ref[0] + dtb_r_ref[...]) * (-jnp.exp(al_r_ref[...]))
    row_i = lax.broadcasted_iota(jnp.int32, (t, t), 0)
    col_i = lax.broadcasted_iota(jnp.int32, (t, t), 1)
    same = (row_i // CHUNK) == (col_i // CHUNK)
    lower = (same & (col_i <= row_i)).astype(f32)
    upper = (same & (row_i <= col_i)).astype(f32)
    acs_c = jnp.dot(lower, dta_c, precision=_HI, preferred_element_type=f32)
    acs_r = jnp.dot(dta_r, upper, precision=_HI, preferred_element_type=f32)
    end_c = jnp.dot(same.astype(f32), dta_c, precision=_HI, preferred_element_type=f32)

    expand = (lax.broadcasted_iota(jnp.int32, (SSD_HEADS, SSD_WIDTH), 1) // SSD_HEAD_DIM
              == lax.broadcasted_iota(jnp.int32, (SSD_HEADS, SSD_WIDTH), 0)).astype(f32)
    widen = lambda a: jnp.dot(a, expand, precision=_HI, preferred_element_type=f32)
    xdt = xs * widen(dt_c)
    e_acs = widen(jnp.exp(acs_c))
    x_end = xdt * widen(jnp.exp(end_c - acs_c))
    e_end = widen(jnp.exp(end_c))

    causal = (lax.broadcasted_iota(jnp.int32, (CHUNK, CHUNK), 1)
              <= lax.broadcasted_iota(jnp.int32, (CHUNK, CHUNK), 0))
    head_of_lane = lax.broadcasted_iota(jnp.int32, (CHUNK, SSD_GW), 1) // SSD_HEAD_DIM
    nt = (((1,), (1,)), ((), ()))
    for c in range(t // CHUNK):
        r0 = c * CHUNK
        rows = slice(r0, r0 + CHUNK)
        new_state = []
        for g in range(SSD_GROUPS):
            cols = slice(g * SSD_GW, (g + 1) * SSD_GW)
            bm = xc[rows, SSD_WIDTH + g * SSD_STATE:SSD_WIDTH + (g + 1) * SSD_STATE].astype(bf16)
            cm = xc[rows, SSD_WIDTH + (SSD_GROUPS + g) * SSD_STATE:
                    SSD_WIDTH + (SSD_GROUPS + g + 1) * SSD_STATE].astype(bf16)
            cb = lax.dot_general(cm, bm, nt, preferred_element_type=f32)
            xdt_g = xdt[rows, cols]
            y = jnp.zeros((CHUNK, SSD_GW), f32)
            for hh in range(SSD_HEADS_PER_GROUP):
                h = g * SSD_HEADS_PER_GROUP + hh
                seg = acs_c[rows, h:h + 1] - acs_r[h:h + 1, r0:r0 + CHUNK]
                decay = jnp.exp(jnp.where(causal, seg, -jnp.inf))
                x_h = jnp.where(head_of_lane == hh, xdt_g, 0.0).astype(bf16)
                y = y + jnp.dot((cb * decay).astype(bf16), x_h, preferred_element_type=f32)
            y = y + e_acs[rows, cols] * jnp.dot(cm, st_scr[:, cols].astype(bf16),
                                                preferred_element_type=f32)
            y_scr[rows, cols] = y
            new_state.append(lax.dot_general(bm, x_end[rows, cols].astype(bf16),
                                             (((0,), (0,)), ((), ())),
                                             preferred_element_type=f32))
        st_scr[...] = st_scr[...] * e_end[r0:r0 + 1, :] + jnp.concatenate(new_state, axis=1)

    y = y_scr[...] + xs * dskip_ref[...]
    y = y * jax.nn.silu(z_ref[0])
    for g in range(SSD_GROUPS):
        cols = slice(g * SSD_GW, (g + 1) * SSD_GW)
        yg = y[:, cols]
        yg = yg * lax.rsqrt(jnp.mean(yg * yg, axis=-1, keepdims=True) + EPS)
        o_ref[0, :, cols] = yg * nw_ref[:, cols]


def ssd_branch(proj, dt_raw, conv_w, conv_b, dt_bias, a_log, d_skip, norm_w):
    b, s, _ = proj.shape
    h = SSD_HEADS
    w = SSD_WIDTH
    blk = lambda col: pl.BlockSpec((1, SSD_T, w), lambda bi, j, col=col: (bi, j, col))
    const = lambda shape: pl.BlockSpec(shape, lambda bi, j: (0,) * len(shape))
    return pl.pallas_call(
        _ssd_kernel,
        grid=(b, s // SSD_T),
        in_specs=[
            blk(0), blk(1), blk(2),
            pl.BlockSpec((1, SSD_T, h), lambda bi, j: (bi, j, 0)),
            pl.BlockSpec((1, h, SSD_T), lambda bi, j: (bi, 0, j)),
            const((CONV_WIDTH, SSD_CONV_CH)), const((1, SSD_CONV_CH)),
            const((1, h)), const((h, 1)), const((1, h)), const((h, 1)),
            const((1, w)), const((1, w)),
        ],
        out_specs=pl.BlockSpec((1, SSD_T, w), lambda bi, j: (bi, j, 0)),
        out_shape=jax.ShapeDtypeStruct((b, s, w), jnp.float32),
        scratch_shapes=[
            pltpu.VMEM((SUBLANES + SSD_T, SSD_CONV_CH), jnp.float32),
            pltpu.VMEM((SSD_STATE, w), jnp.float32),
            pltpu.VMEM((SSD_T, w), jnp.float32),
        ],
        compiler_params=pltpu.CompilerParams(
            dimension_semantics=("arbitrary", "arbitrary"),
            vmem_limit_bytes=32 * 1024 * 1024),
        name="ssd_branch",
    )(proj, proj, proj, dt_raw, jnp.swapaxes(dt_raw, 1, 2), conv_w, conv_b.reshape(1, -1),
      dt_bias.reshape(1, h), dt_bias.reshape(h, 1), a_log.reshape(1, h), a_log.reshape(h, 1),
      jnp.repeat(d_skip, SSD_HEAD_DIM).reshape(1, w), norm_w.reshape(1, w))


def rmsnorm(x, w):
    xf = x.astype(jnp.float32)
    y = xf * lax.rsqrt(jnp.mean(xf * xf, axis=-1, keepdims=True) + EPS)
    return (y * w.astype(jnp.float32)).astype(x.dtype)


def rotary_tables(seq):
    inv = jnp.power(ROPE_THETA, -jnp.arange(0, ROPE_DIM, 2, dtype=jnp.float32) / ROPE_DIM)
    ang = jnp.arange(seq, dtype=jnp.float32)[:, None] * inv[None, :]
    return jnp.cos(ang), jnp.sin(ang)


def partial_rotary(x, cos, sin):
    half = ROPE_DIM // 2
    c = cos[:, None, None, :]
    s = sin[:, None, None, :]
    xf = x.astype(jnp.float32)
    x1, x2, xp = xf[..., :half], xf[..., half:ROPE_DIM], xf[..., ROPE_DIM:]
    out = jnp.concatenate([x1 * c - x2 * s, x1 * s + x2 * c, xp], axis=-1)
    return out.astype(x.dtype)


def diff_attention(q, k, v, qn_w, kn_w, lq1, lk1, lq2, lk2, subln_w, lambda_init, cos, sin):
    f32 = jnp.float32
    b, s, _ = q.shape
    q = q.reshape(b, s, DIFF_HEADS, 2, DIFF_QKDIM)
    k = k.reshape(b, s, DIFF_HEADS, 2, DIFF_QKDIM)
    q = partial_rotary(rmsnorm(q, qn_w), cos, sin).astype(f32)
    kf = partial_rotary(rmsnorm(k, kn_w), cos, sin).astype(f32)
    lam = (jnp.exp(jnp.sum(lq1.astype(f32) * lk1.astype(f32)))
           - jnp.exp(jnp.sum(lq2.astype(f32) * lk2.astype(f32))) + lambda_init)
    scale = DIFF_QKDIM ** -0.5
    branch = jnp.arange(2)
    to_cols = lambda a: jnp.swapaxes(a.reshape(b, s, -1).astype(jnp.bfloat16), 1, 2)
    q0 = to_cols(q * scale * (branch == 0)[:, None])
    q1 = to_cols(q * scale * (branch == 1)[:, None])
    kb = kf.reshape(b, s, QK_COLS).astype(jnp.bfloat16)
    vb = to_cols(v)
    return diff_attention_core(q0, q1, kb, vb, lam.reshape(1), subln_w, 1.0 - lambda_init)


def kernel(x, mix_norm_w, w_in, conv_w, conv_b, dt_bias, a_log, d_skip, ssd_norm_w, q_norm_w, k_norm_w, lambda_q1, lambda_k1, lambda_q2, lambda_k2, subln_w, w_out, ffn_norm_w, peer_w_q, peer_sub_keys, peer_u, peer_v):
    b, s, d = x.shape
    cos, sin = rotary_tables(s)
    layer = 0
    lambda_init = 0.8 - 0.6 * math.exp(-0.3 * layer)
    proj = norm_proj(x.reshape(b * s, d), mix_norm_w[layer],
                     w_in[layer].astype(jnp.bfloat16)).reshape(b, s, IN_COLS)
    dt_raw = proj[..., SPLITS[1]:SPLITS[2]]
    q = proj[..., SPLITS[2]:SPLITS[3]]
    k = proj[..., SPLITS[3]:SPLITS[4]]
    v = proj[..., SPLITS[4]:]
    y_ssd = ssd_branch(proj, dt_raw, conv_w[layer], conv_b[layer], dt_bias[layer],
                       a_log[layer], d_skip[layer], ssd_norm_w[layer])
    y_diff = diff_attention(q, k, v, q_norm_w[layer], k_norm_w[layer], lambda_q1[layer],
                            lambda_k1[layer], lambda_q2[layer], lambda_k2[layer],
                            subln_w[layer], lambda_init, cos, sin)
    ycat = jnp.concatenate([y_ssd, y_diff], axis=-1).reshape(b * s, MIX_WIDTH)
    keys = peer_sub_keys[layer].reshape(N_SUBSETS, N_KEYS, PEER_HALF).astype(jnp.bfloat16)
    x2, xn, idx, gates = peer_route(x.reshape(b * s, d), ycat, w_out[layer].astype(jnp.bfloat16),
                                    ffn_norm_w[layer], peer_w_q[layer].astype(jnp.bfloat16), keys)
    out = peer_gather(xn, idx, gates, x2, pack_uv(peer_u[layer], peer_v[layer]))
    return out.reshape(b, s, d)
```

```python
import functools
import math

import jax
import jax.numpy as jnp
from jax import lax
from jax.experimental import pallas as pl
from jax.experimental.pallas import tpu as pltpu

D_MODEL = 1024
CHUNK = 64
EPS = 1e-6

SSD_WIDTH = D_MODEL // 2
SSD_HEAD_DIM = 64
SSD_HEADS = SSD_WIDTH // SSD_HEAD_DIM
SSD_GROUPS = 2
SSD_HEADS_PER_GROUP = SSD_HEADS // SSD_GROUPS
SSD_STATE = 128
CONV_WIDTH = 4
SSD_CONV_CH = SSD_WIDTH + 2 * SSD_GROUPS * SSD_STATE

DIFF_WIDTH = D_MODEL - SSD_WIDTH
DIFF_HEADS = 4
DIFF_VDIM = DIFF_WIDTH // DIFF_HEADS
DIFF_QKDIM = DIFF_VDIM // 2
ROPE_DIM = DIFF_QKDIM // 4
ROPE_THETA = 500000.0

QK_COLS = DIFF_HEADS * 2 * DIFF_QKDIM
SPLITS = (SSD_WIDTH,
          SSD_WIDTH + SSD_CONV_CH,
          SSD_WIDTH + SSD_CONV_CH + SSD_HEADS,
          SSD_WIDTH + SSD_CONV_CH + SSD_HEADS + QK_COLS,
          SSD_WIDTH + SSD_CONV_CH + SSD_HEADS + 2 * QK_COLS)
IN_COLS = SPLITS[-1] + DIFF_WIDTH
MIX_WIDTH = SSD_WIDTH + DIFF_WIDTH

PEER_HEADS = 8
N_KEYS = 128
N_EXPERTS = N_KEYS * N_KEYS
PEER_KEY_DIM = 256
PEER_HALF = PEER_KEY_DIM // 2
PEER_TOPK = 16


def _norm_proj_kernel(x_ref, nw_ref, w_ref, o_ref):
    xf = x_ref[...]
    y = xf * lax.rsqrt(jnp.mean(xf * xf, axis=-1, keepdims=True) + EPS)
    y = (y * nw_ref[...]).astype(jnp.bfloat16)
    o_ref[...] = jnp.dot(y, w_ref[...], preferred_element_type=jnp.float32)


def norm_proj(x2d, norm_w, w_bf16, tm=512):
    m, d = x2d.shape
    n = w_bf16.shape[1]
    return pl.pallas_call(
        _norm_proj_kernel,
        grid=(m // tm,),
        in_specs=[
            pl.BlockSpec((tm, d), lambda i: (i, 0)),
            pl.BlockSpec((1, d), lambda i: (0, 0)),
            pl.BlockSpec((d, n), lambda i: (0, 0)),
        ],
        out_specs=pl.BlockSpec((tm, n), lambda i: (i, 0)),
        out_shape=jax.ShapeDtypeStruct((m, n), jnp.float32),
        compiler_params=pltpu.CompilerParams(
            dimension_semantics=("arbitrary",),
            vmem_limit_bytes=56 * 1024 * 1024),
        name="norm_proj",
    )(x2d, norm_w.reshape(1, d), w_bf16)


PEER_NSEL = PEER_HEADS * PEER_TOPK
PEER_TB = 8
PEER_SLOTS = 3
PEER_AHEAD = PEER_SLOTS - 1
LANES = 128
SUBLANES = 8
_SQRT_HALF = 0.7071067811865476


def _peer_gather_kernel(idx_cur, idx_mid, idx_nxt, xn_ref, g_ref, xres_ref, uv_hbm,
                        o_ref, buf, part_scr, wb_scr, sem):
    i = pl.program_id(0)
    nb = pl.num_programs(0)
    d = xn_ref.shape[-1]
    n_groups = PEER_NSEL // SUBLANES
    n_chunks = d // LANES
    n_units = 2 * PEER_TB * n_groups
    rows_per_unit = 3
    rows_between = PEER_TB * PEER_NSEL - n_units * rows_per_unit

    def start_row(idx_ref, s, r):
        tt, k = divmod(r, PEER_NSEL)
        e = idx_ref[r]
        pltpu.make_async_copy(uv_hbm.at[e, pl.ds(0, 1)],
                              buf.at[s, tt, k // SUBLANES, pl.ds(k % SUBLANES, 1)],
                              sem.at[s]).start(priority=r % 2)

    @pl.when(i == 0)
    def _():
        for s, idx_ref in ((0, idx_cur), (1, idx_mid)):
            for tt in range(PEER_TB):
                def body(kk, carry, tt=tt, s=s, idx_ref=idx_ref):
                    for j in range(SUBLANES):
                        e = idx_ref[tt * PEER_NSEL + kk * SUBLANES + j]
                        pltpu.make_async_copy(uv_hbm.at[e, pl.ds(0, 1)],
                                              buf.at[s, tt, kk, pl.ds(j, 1)], sem.at[s]).start()
                    return carry
                lax.fori_loop(0, n_groups, body, 0)

    def step(slot):
        nxt = (slot + PEER_AHEAD) % PEER_SLOTS
        pltpu.make_async_copy(buf.at[slot], buf.at[slot], sem.at[slot]).wait()
        issued = [0]

        def start_rows(n):
            for r in range(issued[0], issued[0] + n):
                start_row(idx_nxt, nxt, r)
            issued[0] += n

        for tok in range(PEER_TB):
            for kk in range(n_groups):
                prod = buf[slot, tok, kk, :, 0:d] * xn_ref[tok:tok + 1, :]
                part = prod[:, 0:LANES]
                for c in range(1, n_chunks):
                    part = part + prod[:, c * LANES:(c + 1) * LANES]
                part_scr[tok, kk * SUBLANES:(kk + 1) * SUBLANES, :] = part
                start_rows(rows_per_unit)

        start_rows(rows_between)
        lane = lax.broadcasted_iota(jnp.int32, (PEER_NSEL, PEER_TB), 1)
        hm = jnp.zeros((PEER_NSEL, PEER_TB), jnp.float32)
        for tok in range(PEER_TB):
            h = jnp.sum(part_scr[tok], axis=1, keepdims=True)
            hm = jnp.where(lane == tok, h, hm)
        eye = (lax.broadcasted_iota(jnp.int32, (PEER_NSEL, PEER_NSEL), 0)
               == lax.broadcasted_iota(jnp.int32, (PEER_NSEL, PEER_NSEL), 1)).astype(jnp.float32)
        g_t = lax.dot_general(eye, g_ref[...], (((1,), (1,)), ((), ())),
                              precision=lax.Precision.HIGHEST,
                              preferred_element_type=jnp.float32)
        w = g_t * (0.5 * hm * (1.0 + lax.erf(hm * _SQRT_HALF)))
        for tok in range(PEER_TB):
            wb_scr[tok] = jnp.broadcast_to(w[:, tok:tok + 1], (PEER_NSEL, LANES))

        for tok in range(PEER_TB):
            acc = jnp.zeros((SUBLANES, d), jnp.float32)
            for kk in range(n_groups):
                wg = wb_scr[tok, kk * SUBLANES:(kk + 1) * SUBLANES, :]
                rows = buf[slot, tok, kk, :, d:2 * d]
                acc = acc + rows * jnp.concatenate([wg] * n_chunks, axis=1)
                start_rows(rows_per_unit)
            o_ref[tok:tok + 1, :] = xres_ref[tok:tok + 1, :] + jnp.sum(acc, axis=0, keepdims=True)

    for slot in range(PEER_SLOTS):
        @pl.when(i % PEER_SLOTS == slot)
        def _(slot=slot):
            step(slot)

    @pl.when(i == nb - 1)
    def _():
        for k in range(1, PEER_AHEAD + 1):
            s = (i + k) % PEER_SLOTS
            pltpu.make_async_copy(buf.at[s], buf.at[s], sem.at[s]).wait()


def pack_uv(u, v):
    uv = jnp.concatenate([u, v], axis=-1)
    return jnp.pad(uv[:, None, :], ((0, 0), (0, SUBLANES - 1), (0, 0)))


def peer_gather(xn, idx, gates, xres, uv):
    m, d = xn.shape
    nb = m // PEER_TB
    tok = lambda i: (i, 0)
    idx_flat = idx.reshape(m * PEER_NSEL)
    return pl.pallas_call(
        _peer_gather_kernel,
        grid=(nb,),
        in_specs=[
            pl.BlockSpec((PEER_TB * PEER_NSEL,), lambda i: (i,), memory_space=pltpu.SMEM),
            pl.BlockSpec((PEER_TB * PEER_NSEL,), lambda i: (jnp.minimum(i + 1, nb - 1),),
                         memory_space=pltpu.SMEM),
            pl.BlockSpec((PEER_TB * PEER_NSEL,), lambda i: (jnp.minimum(i + 2, nb - 1),),
                         memory_space=pltpu.SMEM),
            pl.BlockSpec((PEER_TB, d), tok),
            pl.BlockSpec((PEER_TB, PEER_NSEL), tok),
            pl.BlockSpec((PEER_TB, d), tok),
            pl.BlockSpec(memory_space=pl.ANY),
        ],
        out_specs=pl.BlockSpec((PEER_TB, d), tok),
        out_shape=jax.ShapeDtypeStruct((m, d), jnp.float32),
        scratch_shapes=[
            pltpu.VMEM((PEER_SLOTS, PEER_TB, PEER_NSEL // SUBLANES, SUBLANES, 2 * d), jnp.float32),
            pltpu.VMEM((PEER_TB, PEER_NSEL, LANES), jnp.float32),
            pltpu.VMEM((PEER_TB, PEER_NSEL, LANES), jnp.float32),
            pltpu.SemaphoreType.DMA((PEER_SLOTS,)),
        ],
        compiler_params=pltpu.CompilerParams(
            dimension_semantics=("arbitrary",),
            vmem_limit_bytes=48 * 1024 * 1024,
            disable_bounds_checks=True),
        name="peer_gather",
    )(idx_flat, idx_flat, idx_flat, xn, gates, xres, uv)


ROUTE_TM = 256
ROUTE_GROUP = LANES
N_SUBSETS = PEER_HEADS * 2


_POS_SENTINEL = 1 << 20


def _take_top(vals, pos, payload=None):
    m = jnp.max(vals, axis=0, keepdims=True)
    p = jnp.min(jnp.where(vals == m, pos, _POS_SENTINEL), axis=0, keepdims=True)
    hit = pos == p
    picked = p if payload is None else jnp.max(jnp.where(hit, payload, -1), axis=0, keepdims=True)
    return m, picked, jnp.where(hit, -jnp.inf, vals)


def _route_kernel(x_ref, y_ref, wo_ref, nw_ref, wq_ref, keys_ref,
                  xo_ref, xn_ref, idx_ref, g_ref,
                  q_scr, sh_scr, ih_scr, idx_scr, g_scr):
    xa = x_ref[...] + jnp.dot(y_ref[...].astype(jnp.bfloat16), wo_ref[...],
                              preferred_element_type=jnp.float32)
    xo_ref[...] = xa
    xn = xa * lax.rsqrt(jnp.mean(xa * xa, axis=-1, keepdims=True) + EPS) * nw_ref[...]
    xn_ref[...] = xn
    q_scr[...] = jnp.dot(xn.astype(jnp.bfloat16), wq_ref[...],
                         preferred_element_type=jnp.float32).astype(jnp.bfloat16)

    key_row = lax.broadcasted_iota(jnp.int32, (N_KEYS, ROUTE_GROUP), 0)

    n_groups = ROUTE_TM // ROUTE_GROUP
    kk = PEER_TOPK
    row8 = lax.broadcasted_iota(jnp.int32, (SUBLANES, ROUTE_GROUP), 0)

    def head_body(h, carry):
        sc = [[None, None] for _ in range(n_groups)]
        for gi in range(n_groups):
            for j in range(2):
                c0 = pl.multiple_of((2 * h + j) * PEER_HALF, PEER_HALF)
                qc = q_scr[gi * ROUTE_GROUP:(gi + 1) * ROUTE_GROUP, pl.ds(c0, PEER_HALF)]
                sc[gi][j] = lax.dot_general(keys_ref[2 * h + j], qc, (((1,), (1,)), ((), ())),
                                            preferred_element_type=jnp.float32)
        for r in range(kk):
            for gi in range(n_groups):
                for j in range(2):
                    m, ki, sc[gi][j] = _take_top(sc[gi][j], key_row)
                    sh_scr[gi, j, r:r + 1, :] = m
                    ih_scr[gi, j, r:r + 1, :] = ki

        cand, cidx = [], []
        poss = [row8, row8 + SUBLANES] + [row8 + a * kk for a in range(1, SUBLANES)]
        poss.append((row8 + SUBLANES) * kk)
        cpos = jnp.concatenate(poss, axis=0)
        for gi in range(n_groups):
            sh, ih = sh_scr.at[gi], ih_scr.at[gi]
            s1_lo, s1_hi = sh[1, 0:SUBLANES, :], sh[1, SUBLANES:kk, :]
            i1_lo, i1_hi = ih[1, 0:SUBLANES, :], ih[1, SUBLANES:kk, :]
            vals = [sh[0, 0:1, :] + s1_lo, sh[0, 0:1, :] + s1_hi]
            cids = [ih[0, 0:1, :] * N_KEYS + i1_lo, ih[0, 0:1, :] * N_KEYS + i1_hi]
            for a in range(1, SUBLANES):
                v = sh[0, a:a + 1, :] + s1_lo
                if kk // (a + 1) < SUBLANES:
                    v = jnp.where(row8 < kk // (a + 1), v, -jnp.inf)
                vals.append(v)
                cids.append(ih[0, a:a + 1, :] * N_KEYS + i1_lo)
            vals.append(sh[0, SUBLANES:kk, :] + sh[1, 0:1, :])
            cids.append(ih[0, SUBLANES:kk, :] * N_KEYS + ih[1, 0:1, :])
            cand.append(jnp.concatenate(vals, axis=0))
            cidx.append(jnp.concatenate(cids, axis=0))
        tops = [[] for _ in range(n_groups)]
        ids = [[] for _ in range(n_groups)]
        for r in range(kk):
            for gi in range(n_groups):
                m, e, cand[gi] = _take_top(cand[gi], cpos, cidx[gi])
                tops[gi].append(m)
                ids[gi].append(e)
        r0 = pl.multiple_of(h * kk, kk)
        for gi in range(n_groups):
            top_s = jnp.concatenate(tops[gi], axis=0)
            ex = jnp.exp(top_s - tops[gi][0])
            idx_scr[gi, pl.ds(r0, kk), :] = jnp.concatenate(ids[gi], axis=0)
            g_scr[gi, pl.ds(r0, kk), :] = ex / jnp.sum(ex, axis=0, keepdims=True)
        return carry
    lax.fori_loop(0, PEER_HEADS, head_body, 0)

    for gi in range(n_groups):
        idx_ref[gi * ROUTE_GROUP:(gi + 1) * ROUTE_GROUP, :] = idx_scr[gi].T
        g_ref[gi * ROUTE_GROUP:(gi + 1) * ROUTE_GROUP, :] = g_scr[gi].T


def peer_route(x2d, ycat, w_out_bf16, ffn_norm_w, w_q_bf16, keys_bf16):
    m, d = x2d.shape
    tm = ROUTE_TM
    tok = lambda i: (i, 0)
    full2 = lambda i: (0, 0)
    return pl.pallas_call(
        _route_kernel,
        grid=(m // tm,),
        in_specs=[
            pl.BlockSpec((tm, d), tok),
            pl.BlockSpec((tm, MIX_WIDTH), tok),
            pl.BlockSpec((MIX_WIDTH, d), full2),
            pl.BlockSpec((1, d), full2),
            pl.BlockSpec((d, PEER_HEADS * PEER_KEY_DIM), full2),
            pl.BlockSpec((N_SUBSETS, N_KEYS, PEER_HALF), lambda i: (0, 0, 0)),
        ],
        out_specs=[
            pl.BlockSpec((tm, d), tok),
            pl.BlockSpec((tm, d), tok),
            pl.BlockSpec((tm, PEER_NSEL), tok),
            pl.BlockSpec((tm, PEER_NSEL), tok),
        ],
        out_shape=[
            jax.ShapeDtypeStruct((m, d), jnp.float32),
            jax.ShapeDtypeStruct((m, d), jnp.float32),
            jax.ShapeDtypeStruct((m, PEER_NSEL), jnp.int32),
            jax.ShapeDtypeStruct((m, PEER_NSEL), jnp.float32),
        ],
        scratch_shapes=[
            pltpu.VMEM((tm, PEER_HEADS * PEER_KEY_DIM), jnp.bfloat16),
            pltpu.VMEM((tm // ROUTE_GROUP, 2, PEER_TOPK, ROUTE_GROUP), jnp.float32),
            pltpu.VMEM((tm // ROUTE_GROUP, 2, PEER_TOPK, ROUTE_GROUP), jnp.int32),
            pltpu.VMEM((tm // ROUTE_GROUP, PEER_NSEL, ROUTE_GROUP), jnp.int32),
            pltpu.VMEM((tm // ROUTE_GROUP, PEER_NSEL, ROUTE_GROUP), jnp.float32),
        ],
        compiler_params=pltpu.CompilerParams(
            dimension_semantics=("arbitrary",),
            vmem_limit_bytes=48 * 1024 * 1024),
        name="peer_route",
    )(x2d, ycat, w_out_bf16, ffn_norm_w.reshape(1, d), w_q_bf16, keys_bf16)


ATT_T = 256
ATT_TK = 512


def _diff_attn_kernel(lam_ref, q0_ref, q1_ref, k_ref, v_ref, sw_ref, o_ref,
                      m_scr, l_scr, acc_scr, *, out_scale):
    qb = pl.program_id(2)
    m_scr[...] = jnp.full(m_scr.shape, -jnp.inf, jnp.float32)
    l_scr[...] = jnp.zeros(l_scr.shape, jnp.float32)
    acc_scr[...] = jnp.zeros(acc_scr.shape, jnp.float32)
    n_full = (qb * ATT_T) // ATT_TK
    key_chunk = (lax.broadcasted_iota(jnp.int32, (ATT_TK, ATT_T), 0) + n_full * ATT_TK) // CHUNK
    qry_chunk = (lax.broadcasted_iota(jnp.int32, (ATT_TK, ATT_T), 1) + qb * ATT_T) // CHUNK
    diag_mask = key_chunk <= qry_chunk

    def update(kb, masked):
        k0 = pl.multiple_of(kb * ATT_TK, ATT_TK)
        ks = k_ref[0, pl.ds(k0, ATT_TK), :]
        vt = v_ref[0, :, pl.ds(k0, ATT_TK)]
        scores = [jnp.dot(ks, q_ref[0], preferred_element_type=jnp.float32)
                  for q_ref in (q0_ref, q1_ref)]
        m_old = [m_scr[0], m_scr[1]]
        l_old = [l_scr[0], l_scr[1]]
        acc_old = [acc_scr[0], acc_scr[1]]
        m_new, alpha, probs = [], [], []
        for j in range(2):
            s = jnp.where(diag_mask, scores[j], -jnp.inf) if masked else scores[j]
            m_new.append(jnp.maximum(m_old[j], jnp.max(s, axis=0, keepdims=True)))
            alpha.append(jnp.exp(m_old[j] - m_new[j]))
            probs.append(jnp.exp(s - m_new[j]))
        pv = [jnp.dot(vt, probs[j].astype(jnp.bfloat16), preferred_element_type=jnp.float32)
              for j in range(2)]
        for j in range(2):
            m_scr[j] = m_new[j]
            l_scr[j] = alpha[j] * l_old[j] + jnp.sum(probs[j], axis=0, keepdims=True)
            acc_scr[j] = alpha[j] * acc_old[j] + pv[j]

    def body(kb, carry):
        update(kb, False)
        return carry
    lax.fori_loop(0, n_full, body, 0)
    update(n_full, True)

    o = acc_scr[0] / l_scr[0] - lam_ref[0] * (acc_scr[1] / l_scr[1])
    o = o * lax.rsqrt(jnp.mean(o * o, axis=0, keepdims=True) + EPS) * sw_ref[...]
    o_ref[0] = (o * out_scale).T


def diff_attention_core(q0, q1, k, v, lam, subln_w, out_scale):
    b, s, _ = k.shape
    qspec = pl.BlockSpec((1, 2 * DIFF_QKDIM, ATT_T), lambda bi, h, i: (bi, h, i))
    kspec = pl.BlockSpec((1, s, 2 * DIFF_QKDIM), lambda bi, h, i: (bi, 0, h))
    vspec = pl.BlockSpec((1, DIFF_VDIM, s), lambda bi, h, i: (bi, h, 0))
    return pl.pallas_call(
        functools.partial(_diff_attn_kernel, out_scale=out_scale),
        grid=(b, DIFF_HEADS, s // ATT_T),
        in_specs=[
            pl.BlockSpec(memory_space=pltpu.SMEM),
            qspec, qspec, kspec, vspec,
            pl.BlockSpec((DIFF_VDIM, 1), lambda bi, h, i: (0, 0)),
        ],
        out_specs=pl.BlockSpec((1, ATT_T, DIFF_VDIM), lambda bi, h, i: (bi, i, h)),
        out_shape=jax.ShapeDtypeStruct((b, s, DIFF_WIDTH), jnp.float32),
        scratch_shapes=[
            pltpu.VMEM((2, 1, ATT_T), jnp.float32),
            pltpu.VMEM((2, 1, ATT_T), jnp.float32),
            pltpu.VMEM((2, DIFF_VDIM, ATT_T), jnp.float32),
        ],
        compiler_params=pltpu.CompilerParams(
            dimension_semantics=("arbitrary", "arbitrary", "arbitrary"),
            vmem_limit_bytes=32 * 1024 * 1024),
        name="diff_attention",
    )(lam, q0, q1, k, v, subln_w.reshape(DIFF_VDIM, 1))


SSD_T = 256
SSD_BC = 2 * SSD_GROUPS * SSD_STATE
SSD_GW = SSD_WIDTH // SSD_GROUPS
_HI = lax.Precision.HIGHEST


def _ssd_kernel(z_ref, xs_ref, bc_ref, dtc_ref, dtr_ref, cw_ref, cb_ref, dtb_c_ref, dtb_r_ref,
                al_c_ref, al_r_ref, dskip_ref, nw_ref, o_ref, xe_scr, st_scr, y_scr):
    t = SSD_T
    f32, bf16 = jnp.float32, jnp.bfloat16

    @pl.when(pl.program_id(1) == 0)
    def _():
        xe_scr[0:SUBLANES, :] = jnp.zeros((SUBLANES, SSD_CONV_CH), f32)
        st_scr[...] = jnp.zeros(st_scr.shape, f32)

    @pl.when(pl.program_id(1) > 0)
    def _():
        xe_scr[0:SUBLANES, :] = xe_scr[t:t + SUBLANES, :]

    xe_scr[SUBLANES:SUBLANES + t, 0:SSD_WIDTH] = xs_ref[0]
    xe_scr[SUBLANES:SUBLANES + t, SSD_WIDTH:SSD_CONV_CH] = bc_ref[0]
    conv = cb_ref[...]
    for w in range(CONV_WIDTH):
        r0 = SUBLANES - (CONV_WIDTH - 1) + w
        conv = conv + cw_ref[w:w + 1, :] * xe_scr[r0:r0 + t, :]
    xc = jax.nn.silu(conv)
    xs = xc[:, 0:SSD_WIDTH]

    dt_c = jax.nn.softplus(dtc_ref[0] + dtb_c_ref[...])
    dta_c = dt_c * (-jnp.exp(al_c_ref[...]))
    dta_r = jax.nn.softplus(dtr_ref[0] + dtb_r_ref[...]) * (-jnp.exp(al_r_ref[...]))
    row_i = lax.broadcasted_iota(jnp.int32, (t, t), 0)
    col_i = lax.broadcasted_iota(jnp.int32, (t, t), 1)
    same = (row_i // CHUNK) == (col_i // CHUNK)
    lower = (same & (col_i <= row_i)).astype(f32)
    upper = (same & (row_i <= col_i)).astype(f32)
    acs_c = jnp.dot(lower, dta_c, precision=_HI, preferred_element_type=f32)
    acs_r = jnp.dot(dta_r, upper, precision=_HI, preferred_element_type=f32)
    end_c = jnp.dot(same.astype(f32), dta_c, precision=_HI, preferred_element_type=f32)

    expand = (lax.broadcasted_iota(jnp.int32, (SSD_HEADS, SSD_WIDTH), 1) // SSD_HEAD_DIM
              == lax.broadcasted_iota(jnp.int32, (SSD_HEADS, SSD_WIDTH), 0)).astype(f32)
    widen = lambda a: jnp.dot(a, expand, precision=_HI, preferred_element_type=f32)
    xdt = xs * widen(dt_c)
    e_acs = widen(jnp.exp(acs_c))
    x_end = xdt * widen(jnp.exp(end_c - acs_c))
    e_end = widen(jnp.exp(end_c))

    causal = (lax.broadcasted_iota(jnp.int32, (CHUNK, CHUNK), 1)
              <= lax.broadcasted_iota(jnp.int32, (CHUNK, CHUNK), 0))
    head_of_lane = lax.broadcasted_iota(jnp.int32, (CHUNK, SSD_GW), 1) // SSD_HEAD_DIM
    nt = (((1,), (1,)), ((), ()))
    for c in range(t // CHUNK):
        r0 = c * CHUNK
        rows = slice(r0, r0 + CHUNK)
        new_state = []
        for g in range(SSD_GROUPS):
            cols = slice(g * SSD_GW, (g + 1) * SSD_GW)
            bm = xc[rows, SSD_WIDTH + g * SSD_STATE:SSD_WIDTH + (g + 1) * SSD_STATE].astype(bf16)
            cm = xc[rows, SSD_WIDTH + (SSD_GROUPS + g) * SSD_STATE:
                    SSD_WIDTH + (SSD_GROUPS + g + 1) * SSD_STATE].astype(bf16)
            cb = lax.dot_general(cm, bm, nt, preferred_element_type=f32)
            xdt_g = xdt[rows, cols]
            y = jnp.zeros((CHUNK, SSD_GW), f32)
            for hh in range(SSD_HEADS_PER_GROUP):
                h = g * SSD_HEADS_PER_GROUP + hh
                seg = acs_c[rows, h:h + 1] - acs_r[h:h + 1, r0:r0 + CHUNK]
                decay = jnp.exp(jnp.where(causal, seg, -jnp.inf))
                x_h = jnp.where(head_of_lane == hh, xdt_g, 0.0).astype(bf16)
                y = y + jnp.dot((cb * decay).astype(bf16), x_h, preferred_element_type=f32)
            y = y + e_acs[rows, cols] * jnp.dot(cm, st_scr[:, cols].astype(bf16),
                                                preferred_element_type=f32)
            y_scr[rows, cols] = y
            new_state.append(lax.dot_general(bm, x_end[rows, cols].astype(bf16),
                                             (((0,), (0,)), ((), ())),
                                             preferred_element_type=f32))
        st_scr[...] = st_scr[...] * e_end[r0:r0 + 1, :] + jnp.concatenate(new_state, axis=1)

    y = y_scr[...] + xs * dskip_ref[...]
    y = y * jax.nn.silu(z_ref[0])
    for g in range(SSD_GROUPS):
        cols = slice(g * SSD_GW, (g + 1) * SSD_GW)
        yg = y[:, cols]
        yg = yg * lax.rsqrt(jnp.mean(yg * yg, axis=-1, keepdims=True) + EPS)
        o_ref[0, :, cols] = yg * nw_ref[:, cols]


def ssd_branch(proj, dt_raw, conv_w, conv_b, dt_bias, a_log, d_skip, norm_w):
    b, s, _ = proj.shape
    h = SSD_HEADS
    w = SSD_WIDTH
    blk = lambda col: pl.BlockSpec((1, SSD_T, w), lambda bi, j, col=col: (bi, j, col))
    const = lambda shape: pl.BlockSpec(shape, lambda bi, j: (0,) * len(shape))
    return pl.pallas_call(
        _ssd_kernel,
        grid=(b, s // SSD_T),
        in_specs=[
            blk(0), blk(1), blk(2),
            pl.BlockSpec((1, SSD_T, h), lambda bi, j: (bi, j, 0)),
            pl.BlockSpec((1, h, SSD_T), lambda bi, j: (bi, 0, j)),
            const((CONV_WIDTH, SSD_CONV_CH)), const((1, SSD_CONV_CH)),
            const((1, h)), const((h, 1)), const((1, h)), const((h, 1)),
            const((1, w)), const((1, w)),
        ],
        out_specs=pl.BlockSpec((1, SSD_T, w), lambda bi, j: (bi, j, 0)),
        out_shape=jax.ShapeDtypeStruct((b, s, w), jnp.float32),
        scratch_shapes=[
            pltpu.VMEM((SUBLANES + SSD_T, SSD_CONV_CH), jnp.float32),
            pltpu.VMEM((SSD_STATE, w), jnp.float32),
            pltpu.VMEM((SSD_T, w), jnp.float32),
        ],
        compiler_params=pltpu.CompilerParams(
            dimension_semantics=("arbitrary", "arbitrary"),
            vmem_limit_bytes=32 * 1024 * 1024),
        name="ssd_branch",
    )(proj, proj, proj, dt_raw, jnp.swapaxes(dt_raw, 1, 2), conv_w, conv_b.reshape(1, -1),
      dt_bias.reshape(1, h), dt_bias.reshape(h, 1), a_log.reshape(1, h), a_log.reshape(h, 1),
      jnp.repeat(d_skip, SSD_HEAD_DIM).reshape(1, w), norm_w.reshape(1, w))


def rmsnorm(x, w):
    xf = x.astype(jnp.float32)
    y = xf * lax.rsqrt(jnp.mean(xf * xf, axis=-1, keepdims=True) + EPS)
    return (y * w.astype(jnp.float32)).astype(x.dtype)


def rotary_tables(seq):
    inv = jnp.power(ROPE_THETA, -jnp.arange(0, ROPE_DIM, 2, dtype=jnp.float32) / ROPE_DIM)
    ang = jnp.arange(seq, dtype=jnp.float32)[:, None] * inv[None, :]
    return jnp.cos(ang), jnp.sin(ang)


def partial_rotary(x, cos, sin):
    half = ROPE_DIM // 2
    c = cos[:, None, None, :]
    s = sin[:, None, None, :]
    xf = x.astype(jnp.float32)
    x1, x2, xp = xf[..., :half], xf[..., half:ROPE_DIM], xf[..., ROPE_DIM:]
    out = jnp.concatenate([x1 * c - x2 * s, x1 * s + x2 * c, xp], axis=-1)
    return out.astype(x.dtype)


def diff_attention(q, k, v, qn_w, kn_w, lq1, lk1, lq2, lk2, subln_w, lambda_init, cos, sin):
    f32 = jnp.float32
    b, s, _ = q.shape
    q = q.reshape(b, s, DIFF_HEADS, 2, DIFF_QKDIM)
    k = k.reshape(b, s, DIFF_HEADS, 2, DIFF_QKDIM)
    q = partial_rotary(rmsnorm(q, qn_w), cos, sin).astype(f32)
    kf = partial_rotary(rmsnorm(k, kn_w), cos, sin).astype(f32)
    lam = (jnp.exp(jnp.sum(lq1.astype(f32) * lk1.astype(f32)))
           - jnp.exp(jnp.sum(lq2.astype(f32) * lk2.astype(f32))) + lambda_init)
    scale = DIFF_QKDIM ** -0.5
    branch = jnp.arange(2)
    to_cols = lambda a: jnp.swapaxes(a.reshape(b, s, -1).astype(jnp.bfloat16), 1, 2)
    q0 = to_cols(q * scale * (branch == 0)[:, None])
    q1 = to_cols(q * scale * (branch == 1)[:, None])
    kb = kf.reshape(b, s, QK_COLS).astype(jnp.bfloat16)
    vb = to_cols(v)
    return diff_attention_core(q0, q1, kb, vb, lam.reshape(1), subln_w, 1.0 - lambda_init)


def kernel(x, mix_norm_w, w_in, conv_w, conv_b, dt_bias, a_log, d_skip, ssd_norm_w, q_norm_w, k_norm_w, lambda_q1, lambda_k1, lambda_q2, lambda_k2, subln_w, w_out, ffn_norm_w, peer_w_q, peer_sub_keys, peer_u, peer_v):
    b, s, d = x.shape
    cos, sin = rotary_tables(s)
    layer = 0
    lambda_init = 0.8 - 0.6 * math.exp(-0.3 * layer)
    proj = norm_proj(x.reshape(b * s, d), mix_norm_w[layer],
                     w_in[layer].astype(jnp.bfloat16)).reshape(b, s, IN_COLS)
    dt_raw = proj[..., SPLITS[1]:SPLITS[2]]
    q = proj[..., SPLITS[2]:SPLITS[3]]
    k = proj[..., SPLITS[3]:SPLITS[4]]
    v = proj[..., SPLITS[4]:]
    y_ssd = ssd_branch(proj, dt_raw, conv_w[layer], conv_b[layer], dt_bias[layer],
                       a_log[layer], d_skip[layer], ssd_norm_w[layer])
    y_diff = diff_attention(q, k, v, q_norm_w[layer], k_norm_w[layer], lambda_q1[layer],
                            lambda_k1[layer], lambda_q2[layer], lambda_k2[layer],
                            subln_w[layer], lambda_init, cos, sin)
    ycat = jnp.concatenate([y_ssd, y_diff], axis=-1).reshape(b * s, MIX_WIDTH)
    keys = peer_sub_keys[layer].reshape(N_SUBSETS, N_KEYS, PEER_HALF).astype(jnp.bfloat16)
    x2, xn, idx, gates = peer_route(x.reshape(b * s, d), ycat, w_out[layer].astype(jnp.bfloat16),
                                    ffn_norm_w[layer], peer_w_q[layer].astype(jnp.bfloat16), keys)
    out = peer_gather(xn, idx, gates, x2, pack_uv(peer_u[layer], peer_v[layer]))
    return out.reshape(b, s, d)
```

```python
import functools
import math

import jax
import jax.numpy as jnp
from jax import lax
from jax.experimental import pallas as pl
from jax.experimental.pallas import tpu as pltpu

D_MODEL = 1024
CHUNK = 64
EPS = 1e-6

SSD_WIDTH = D_MODEL // 2
SSD_HEAD_DIM = 64
SSD_HEADS = SSD_WIDTH // SSD_HEAD_DIM
SSD_GROUPS = 2
SSD_HEADS_PER_GROUP = SSD_HEADS // SSD_GROUPS
SSD_STATE = 128
CONV_WIDTH = 4
SSD_CONV_CH = SSD_WIDTH + 2 * SSD_GROUPS * SSD_STATE

DIFF_WIDTH = D_MODEL - SSD_WIDTH
DIFF_HEADS = 4
DIFF_VDIM = DIFF_WIDTH // DIFF_HEADS
DIFF_QKDIM = DIFF_VDIM // 2
ROPE_DIM = DIFF_QKDIM // 4
ROPE_THETA = 500000.0

QK_COLS = DIFF_HEADS * 2 * DIFF_QKDIM
SPLITS = (SSD_WIDTH,
          SSD_WIDTH + SSD_CONV_CH,
          SSD_WIDTH + SSD_CONV_CH + SSD_HEADS,
          SSD_WIDTH + SSD_CONV_CH + SSD_HEADS + QK_COLS,
          SSD_WIDTH + SSD_CONV_CH + SSD_HEADS + 2 * QK_COLS)
IN_COLS = SPLITS[-1] + DIFF_WIDTH
MIX_WIDTH = SSD_WIDTH + DIFF_WIDTH

PEER_HEADS = 8
N_KEYS = 128
N_EXPERTS = N_KEYS * N_KEYS
PEER_KEY_DIM = 256
PEER_HALF = PEER_KEY_DIM // 2
PEER_TOPK = 16


def _norm_proj_kernel(x_ref, nw_ref, w_ref, o_ref):
    xf = x_ref[...]
    y = xf * lax.rsqrt(jnp.mean(xf * xf, axis=-1, keepdims=True) + EPS)
    y = (y * nw_ref[...]).astype(jnp.bfloat16)
    o_ref[...] = jnp.dot(y, w_ref[...], preferred_element_type=jnp.float32)


def norm_proj(x2d, norm_w, w_bf16, tm=512):
    m, d = x2d.shape
    n = w_bf16.shape[1]
    return pl.pallas_call(
        _norm_proj_kernel,
        grid=(m // tm,),
        in_specs=[
            pl.BlockSpec((tm, d), lambda i: (i, 0)),
            pl.BlockSpec((1, d), lambda i: (0, 0)),
            pl.BlockSpec((d, n), lambda i: (0, 0)),
        ],
        out_specs=pl.BlockSpec((tm, n), lambda i: (i, 0)),
        out_shape=jax.ShapeDtypeStruct((m, n), jnp.float32),
        compiler_params=pltpu.CompilerParams(
            dimension_semantics=("arbitrary",),
            vmem_limit_bytes=56 * 1024 * 1024),
        name="norm_proj",
    )(x2d, norm_w.reshape(1, d), w_bf16)


PEER_NSEL = PEER_HEADS * PEER_TOPK
PEER_TB = 8
PEER_SLOTS = 3
PEER_AHEAD = PEER_SLOTS - 1
LANES = 128
SUBLANES = 8
_SQRT_HALF = 0.7071067811865476


def _peer_gather_kernel(idx_cur, idx_mid, idx_nxt, xn_ref, g_ref, xres_ref, uv_hbm,
                        o_ref, buf, part_scr, wb_scr, sem):
    i = pl.program_id(0)
    nb = pl.num_programs(0)
    d = xn_ref.shape[-1]
    n_groups = PEER_NSEL // SUBLANES
    n_chunks = d // LANES
    n_units = 2 * PEER_TB * n_groups
    rows_per_unit = 3
    rows_between = PEER_TB * PEER_NSEL - n_units * rows_per_unit

    def start_row(idx_ref, s, r):
        tt, k = divmod(r, PEER_NSEL)
        e = idx_ref[r]
        pltpu.make_async_copy(uv_hbm.at[e, pl.ds(0, 1)],
                              buf.at[s, tt, k // SUBLANES, pl.ds(k % SUBLANES, 1)],
                              sem.at[s]).start(priority=r % 2)

    @pl.when(i == 0)
    def _():
        for s, idx_ref in ((0, idx_cur), (1, idx_mid)):
            for tt in range(PEER_TB):
                def body(kk, carry, tt=tt, s=s, idx_ref=idx_ref):
                    for j in range(SUBLANES):
                        e = idx_ref[tt * PEER_NSEL + kk * SUBLANES + j]
                        pltpu.make_async_copy(uv_hbm.at[e, pl.ds(0, 1)],
                                              buf.at[s, tt, kk, pl.ds(j, 1)], sem.at[s]).start()
                    return carry
                lax.fori_loop(0, n_groups, body, 0)

    def step(slot):
        nxt = (slot + PEER_AHEAD) % PEER_SLOTS
        pltpu.make_async_copy(buf.at[slot], buf.at[slot], sem.at[slot]).wait()
        issued = [0]

        def start_rows(n):
            for r in range(issued[0], issued[0] + n):
                start_row(idx_nxt, nxt, r)
            issued[0] += n

        for tok in range(PEER_TB):
            for kk in range(n_groups):
                prod = buf[slot, tok, kk, :, 0:d] * xn_ref[tok:tok + 1, :]
                part = prod[:, 0:LANES]
                for c in range(1, n_chunks):
                    part = part + prod[:, c * LANES:(c + 1) * LANES]
                part_scr[tok, kk * SUBLANES:(kk + 1) * SUBLANES, :] = part
                start_rows(rows_per_unit)

        start_rows(rows_between)
        lane = lax.broadcasted_iota(jnp.int32, (PEER_NSEL, PEER_TB), 1)
        hm = jnp.zeros((PEER_NSEL, PEER_TB), jnp.float32)
        for tok in range(PEER_TB):
            h = jnp.sum(part_scr[tok], axis=1, keepdims=True)
            hm = jnp.where(lane == tok, h, hm)
        eye = (lax.broadcasted_iota(jnp.int32, (PEER_NSEL, PEER_NSEL), 0)
               == lax.broadcasted_iota(jnp.int32, (PEER_NSEL, PEER_NSEL), 1)).astype(jnp.float32)
        g_t = lax.dot_general(eye, g_ref[...], (((1,), (1,)), ((), ())),
                              precision=lax.Precision.HIGHEST,
                              preferred_element_type=jnp.float32)
        w = g_t * (0.5 * hm * (1.0 + lax.erf(hm * _SQRT_HALF)))
        for tok in range(PEER_TB):
            wb_scr[tok] = jnp.broadcast_to(w[:, tok:tok + 1], (PEER_NSEL, LANES))

        for tok in range(PEER_TB):
            acc = jnp.zeros((SUBLANES, d), jnp.float32)
            for kk in range(n_groups):
                wg = wb_scr[tok, kk * SUBLANES:(kk + 1) * SUBLANES, :]
                rows = buf[slot, tok, kk, :, d:2 * d]
                acc = acc + rows * jnp.concatenate([wg] * n_chunks, axis=1)
                start_rows(rows_per_unit)
            o_ref[tok:tok + 1, :] = xres_ref[tok:tok + 1, :] + jnp.sum(acc, axis=0, keepdims=True)

    for slot in range(PEER_SLOTS):
        @pl.when(i % PEER_SLOTS == slot)
        def _(slot=slot):
            step(slot)

    @pl.when(i == nb - 1)
    def _():
        for k in range(1, PEER_AHEAD + 1):
            s = (i + k) % PEER_SLOTS
            pltpu.make_async_copy(buf.at[s], buf.at[s], sem.at[s]).wait()


def pack_uv(u, v):
    uv = jnp.concatenate([u, v], axis=-1)
    return jnp.pad(uv[:, None, :], ((0, 0), (0, SUBLANES - 1), (0, 0)))


def peer_gather(xn, idx, gates, xres, uv):
    m, d = xn.shape
    nb = m // PEER_TB
    tok = lambda i: (i, 0)
    idx_flat = idx.reshape(m * PEER_NSEL)
    return pl.pallas_call(
        _peer_gather_kernel,
        grid=(nb,),
        in_specs=[
            pl.BlockSpec((PEER_TB * PEER_NSEL,), lambda i: (i,), memory_space=pltpu.SMEM),
            pl.BlockSpec((PEER_TB * PEER_NSEL,), lambda i: (jnp.minimum(i + 1, nb - 1),),
                         memory_space=pltpu.SMEM),
            pl.BlockSpec((PEER_TB * PEER_NSEL,), lambda i: (jnp.minimum(i + 2, nb - 1),),
                         memory_space=pltpu.SMEM),
            pl.BlockSpec((PEER_TB, d), tok),
            pl.BlockSpec((PEER_TB, PEER_NSEL), tok),
            pl.BlockSpec((PEER_TB, d), tok),
            pl.BlockSpec(memory_space=pl.ANY),
        ],
        out_specs=pl.BlockSpec((PEER_TB, d), tok),
        out_shape=jax.ShapeDtypeStruct((m, d), jnp.float32),
        scratch_shapes=[
            pltpu.VMEM((PEER_SLOTS, PEER_TB, PEER_NSEL // SUBLANES, SUBLANES, 2 * d), jnp.float32),
            pltpu.VMEM((PEER_TB, PEER_NSEL, LANES), jnp.float32),
            pltpu.VMEM((PEER_TB, PEER_NSEL, LANES), jnp.float32),
            pltpu.SemaphoreType.DMA((PEER_SLOTS,)),
        ],
        compiler_params=pltpu.CompilerParams(
            dimension_semantics=("arbitrary",),
            vmem_limit_bytes=48 * 1024 * 1024,
            disable_bounds_checks=True),
        name="peer_gather",
    )(idx_flat, idx_flat, idx_flat, xn, gates, xres, uv)


ROUTE_TM = 256
ROUTE_GROUP = LANES
N_SUBSETS = PEER_HEADS * 2


_POS_SENTINEL = 1 << 20


def _take_top(vals, pos, payload=None):
    m = jnp.max(vals, axis=0, keepdims=True)
    p = jnp.min(jnp.where(vals == m, pos, _POS_SENTINEL), axis=0, keepdims=True)
    hit = pos == p
    picked = p if payload is None else jnp.max(jnp.where(hit, payload, -1), axis=0, keepdims=True)
    return m, picked, jnp.where(hit, -jnp.inf, vals)


def _route_kernel(x_ref, ys_ref, yd_ref, wo_ref, nw_ref, wq_ref, keys_ref,
                  xo_ref, xn_ref, idx_ref, g_ref,
                  q_scr, sh_scr, ih_scr, idx_scr, g_scr):
    xa = (x_ref[...]
          + jnp.dot(ys_ref[...].astype(jnp.bfloat16), wo_ref[0:SSD_WIDTH, :],
                    preferred_element_type=jnp.float32)
          + jnp.dot(yd_ref[...].astype(jnp.bfloat16), wo_ref[SSD_WIDTH:MIX_WIDTH, :],
                    preferred_element_type=jnp.float32))
    xo_ref[...] = xa
    xn = xa * lax.rsqrt(jnp.mean(xa * xa, axis=-1, keepdims=True) + EPS) * nw_ref[...]
    xn_ref[...] = xn
    q_scr[...] = jnp.dot(xn.astype(jnp.bfloat16), wq_ref[...],
                         preferred_element_type=jnp.float32).astype(jnp.bfloat16)

    key_row = lax.broadcasted_iota(jnp.int32, (N_KEYS, ROUTE_GROUP), 0)

    n_groups = ROUTE_TM // ROUTE_GROUP
    kk = PEER_TOPK
    row8 = lax.broadcasted_iota(jnp.int32, (SUBLANES, ROUTE_GROUP), 0)

    def head_body(h, carry):
        sc = [[None, None] for _ in range(n_groups)]
        for gi in range(n_groups):
            for j in range(2):
                c0 = pl.multiple_of((2 * h + j) * PEER_HALF, PEER_HALF)
                qc = q_scr[gi * ROUTE_GROUP:(gi + 1) * ROUTE_GROUP, pl.ds(c0, PEER_HALF)]
                sc[gi][j] = lax.dot_general(keys_ref[2 * h + j], qc, (((1,), (1,)), ((), ())),
                                            preferred_element_type=jnp.float32)
        for r in range(kk):
            for gi in range(n_groups):
                for j in range(2):
                    m, ki, sc[gi][j] = _take_top(sc[gi][j], key_row)
                    sh_scr[gi, j, r:r + 1, :] = m
                    ih_scr[gi, j, r:r + 1, :] = ki

        cand, cidx = [], []
        poss = [row8, row8 + SUBLANES] + [row8 + a * kk for a in range(1, SUBLANES)]
        poss.append((row8 + SUBLANES) * kk)
        cpos = jnp.concatenate(poss, axis=0)
        for gi in range(n_groups):
            sh, ih = sh_scr.at[gi], ih_scr.at[gi]
            s1_lo, s1_hi = sh[1, 0:SUBLANES, :], sh[1, SUBLANES:kk, :]
            i1_lo, i1_hi = ih[1, 0:SUBLANES, :], ih[1, SUBLANES:kk, :]
            vals = [sh[0, 0:1, :] + s1_lo, sh[0, 0:1, :] + s1_hi]
            cids = [ih[0, 0:1, :] * N_KEYS + i1_lo, ih[0, 0:1, :] * N_KEYS + i1_hi]
            for a in range(1, SUBLANES):
                v = sh[0, a:a + 1, :] + s1_lo
                if kk // (a + 1) < SUBLANES:
                    v = jnp.where(row8 < kk // (a + 1), v, -jnp.inf)
                vals.append(v)
                cids.append(ih[0, a:a + 1, :] * N_KEYS + i1_lo)
            vals.append(sh[0, SUBLANES:kk, :] + sh[1, 0:1, :])
            cids.append(ih[0, SUBLANES:kk, :] * N_KEYS + ih[1, 0:1, :])
            cand.append(jnp.concatenate(vals, axis=0))
            cidx.append(jnp.concatenate(cids, axis=0))
        tops = [[] for _ in range(n_groups)]
        ids = [[] for _ in range(n_groups)]
        for r in range(kk):
            for gi in range(n_groups):
                m, e, cand[gi] = _take_top(cand[gi], cpos, cidx[gi])
                tops[gi].append(m)
                ids[gi].append(e)
        r0 = pl.multiple_of(h * kk, kk)
        for gi in range(n_groups):
            top_s = jnp.concatenate(tops[gi], axis=0)
            ex = jnp.exp(top_s - tops[gi][0])
            idx_scr[gi, pl.ds(r0, kk), :] = jnp.concatenate(ids[gi], axis=0)
            g_scr[gi, pl.ds(r0, kk), :] = ex / jnp.sum(ex, axis=0, keepdims=True)
        return carry
    lax.fori_loop(0, PEER_HEADS, head_body, 0)

    for gi in range(n_groups):
        idx_ref[gi * ROUTE_GROUP:(gi + 1) * ROUTE_GROUP, :] = idx_scr[gi].T
        g_ref[gi * ROUTE_GROUP:(gi + 1) * ROUTE_GROUP, :] = g_scr[gi].T


def peer_route(x2d, y_ssd, y_diff, w_out_bf16, ffn_norm_w, w_q_bf16, keys_bf16):
    m, d = x2d.shape
    tm = ROUTE_TM
    tok = lambda i: (i, 0)
    full2 = lambda i: (0, 0)
    return pl.pallas_call(
        _route_kernel,
        grid=(m // tm,),
        in_specs=[
            pl.BlockSpec((tm, d), tok),
            pl.BlockSpec((tm, SSD_WIDTH), tok),
            pl.BlockSpec((tm, DIFF_WIDTH), tok),
            pl.BlockSpec((MIX_WIDTH, d), full2),
            pl.BlockSpec((1, d), full2),
            pl.BlockSpec((d, PEER_HEADS * PEER_KEY_DIM), full2),
            pl.BlockSpec((N_SUBSETS, N_KEYS, PEER_HALF), lambda i: (0, 0, 0)),
        ],
        out_specs=[
            pl.BlockSpec((tm, d), tok),
            pl.BlockSpec((tm, d), tok),
            pl.BlockSpec((tm, PEER_NSEL), tok),
            pl.BlockSpec((tm, PEER_NSEL), tok),
        ],
        out_shape=[
            jax.ShapeDtypeStruct((m, d), jnp.float32),
            jax.ShapeDtypeStruct((m, d), jnp.float32),
            jax.ShapeDtypeStruct((m, PEER_NSEL), jnp.int32),
            jax.ShapeDtypeStruct((m, PEER_NSEL), jnp.float32),
        ],
        scratch_shapes=[
            pltpu.VMEM((tm, PEER_HEADS * PEER_KEY_DIM), jnp.bfloat16),
            pltpu.VMEM((tm // ROUTE_GROUP, 2, PEER_TOPK, ROUTE_GROUP), jnp.float32),
            pltpu.VMEM((tm // ROUTE_GROUP, 2, PEER_TOPK, ROUTE_GROUP), jnp.int32),
            pltpu.VMEM((tm // ROUTE_GROUP, PEER_NSEL, ROUTE_GROUP), jnp.int32),
            pltpu.VMEM((tm // ROUTE_GROUP, PEER_NSEL, ROUTE_GROUP), jnp.float32),
        ],
        compiler_params=pltpu.CompilerParams(
            dimension_semantics=("arbitrary",),
            vmem_limit_bytes=48 * 1024 * 1024),
        name="peer_route",
    )(x2d, y_ssd, y_diff, w_out_bf16, ffn_norm_w.reshape(1, d), w_q_bf16, keys_bf16)


ATT_T = 256
ATT_TK = 512


def _diff_attn_kernel(lam_ref, q0_ref, q1_ref, k_ref, v_ref, sw_ref, o_ref,
                      m_scr, l_scr, acc_scr, *, out_scale):
    qb = pl.program_id(2)
    m_scr[...] = jnp.full(m_scr.shape, -jnp.inf, jnp.float32)
    l_scr[...] = jnp.zeros(l_scr.shape, jnp.float32)
    acc_scr[...] = jnp.zeros(acc_scr.shape, jnp.float32)
    n_full = (qb * ATT_T) // ATT_TK
    key_chunk = (lax.broadcasted_iota(jnp.int32, (ATT_TK, ATT_T), 0) + n_full * ATT_TK) // CHUNK
    qry_chunk = (lax.broadcasted_iota(jnp.int32, (ATT_TK, ATT_T), 1) + qb * ATT_T) // CHUNK
    diag_mask = key_chunk <= qry_chunk

    def update(kb, masked):
        k0 = pl.multiple_of(kb * ATT_TK, ATT_TK)
        ks = k_ref[0, pl.ds(k0, ATT_TK), :]
        vt = v_ref[0, :, pl.ds(k0, ATT_TK)]
        scores = [jnp.dot(ks, q_ref[0], preferred_element_type=jnp.float32)
                  for q_ref in (q0_ref, q1_ref)]
        m_old = [m_scr[0], m_scr[1]]
        l_old = [l_scr[0], l_scr[1]]
        acc_old = [acc_scr[0], acc_scr[1]]
        m_new, alpha, probs = [], [], []
        for j in range(2):
            s = jnp.where(diag_mask, scores[j], -jnp.inf) if masked else scores[j]
            m_new.append(jnp.maximum(m_old[j], jnp.max(s, axis=0, keepdims=True)))
            alpha.append(jnp.exp(m_old[j] - m_new[j]))
            probs.append(jnp.exp(s - m_new[j]))
        pv = [jnp.dot(vt, probs[j].astype(jnp.bfloat16), preferred_element_type=jnp.float32)
              for j in range(2)]
        for j in range(2):
            m_scr[j] = m_new[j]
            l_scr[j] = alpha[j] * l_old[j] + jnp.sum(probs[j], axis=0, keepdims=True)
            acc_scr[j] = alpha[j] * acc_old[j] + pv[j]

    def body(kb, carry):
        update(kb, False)
        return carry
    lax.fori_loop(0, n_full, body, 0)
    update(n_full, True)

    o = acc_scr[0] / l_scr[0] - lam_ref[0] * (acc_scr[1] / l_scr[1])
    o = o * lax.rsqrt(jnp.mean(o * o, axis=0, keepdims=True) + EPS) * sw_ref[...]
    o_ref[0] = (o * out_scale).T


def diff_attention_core(q0, q1, k, v, lam, subln_w, out_scale):
    b, s, _ = k.shape
    qspec = pl.BlockSpec((1, 2 * DIFF_QKDIM, ATT_T), lambda bi, h, i: (bi, h, i))
    kspec = pl.BlockSpec((1, s, 2 * DIFF_QKDIM), lambda bi, h, i: (bi, 0, h))
    vspec = pl.BlockSpec((1, DIFF_VDIM, s), lambda bi, h, i: (bi, h, 0))
    return pl.pallas_call(
        functools.partial(_diff_attn_kernel, out_scale=out_scale),
        grid=(b, DIFF_HEADS, s // ATT_T),
        in_specs=[
            pl.BlockSpec(memory_space=pltpu.SMEM),
            qspec, qspec, kspec, vspec,
            pl.BlockSpec((DIFF_VDIM, 1), lambda bi, h, i: (0, 0)),
        ],
        out_specs=pl.BlockSpec((1, ATT_T, DIFF_VDIM), lambda bi, h, i: (bi, i, h)),
        out_shape=jax.ShapeDtypeStruct((b, s, DIFF_WIDTH), jnp.float32),
        scratch_shapes=[
            pltpu.VMEM((2, 1, ATT_T), jnp.float32),
            pltpu.VMEM((2, 1, ATT_T), jnp.float32),
            pltpu.VMEM((2, DIFF_VDIM, ATT_T), jnp.float32),
        ],
        compiler_params=pltpu.CompilerParams(
            dimension_semantics=("arbitrary", "arbitrary", "arbitrary"),
            vmem_limit_bytes=32 * 1024 * 1024),
        name="diff_attention",
    )(lam, q0, q1, k, v, subln_w.reshape(DIFF_VDIM, 1))


SSD_T = 256
SSD_BC = 2 * SSD_GROUPS * SSD_STATE
SSD_GW = SSD_WIDTH // SSD_GROUPS
_HI = lax.Precision.HIGHEST


def _ssd_kernel(z_ref, xs_ref, bc_ref, dtc_ref, dtr_ref, cw_ref, cb_ref, dtb_c_ref, dtb_r_ref,
                al_c_ref, al_r_ref, dskip_ref, nw_ref, o_ref, xe_scr, st_scr, y_scr):
    t = SSD_T
    f32, bf16 = jnp.float32, jnp.bfloat16

    @pl.when(pl.program_id(1) == 0)
    def _():
        xe_scr[0:SUBLANES, :] = jnp.zeros((SUBLANES, SSD_CONV_CH), f32)
        st_scr[...] = jnp.zeros(st_scr.shape, f32)

    @pl.when(pl.program_id(1) > 0)
    def _():
        xe_scr[0:SUBLANES, :] = xe_scr[t:t + SUBLANES, :]

    xe_scr[SUBLANES:SUBLANES + t, 0:SSD_WIDTH] = xs_ref[0]
    xe_scr[SUBLANES:SUBLANES + t, SSD_WIDTH:SSD_CONV_CH] = bc_ref[0]
    conv = cb_ref[...]
    for w in range(CONV_WIDTH):
        r0 = SUBLANES - (CONV_WIDTH - 1) + w
        conv = conv + cw_ref[w:w + 1, :] * xe_scr[r0:r0 + t, :]
    xc = jax.nn.silu(conv)
    xs = xc[:, 0:SSD_WIDTH]

    dt_c = jax.nn.softplus(dtc_ref[0] + dtb_c_ref[...])
    dta_c = dt_c * (-jnp.exp(al_c_ref[...]))
    dta_r = jax.nn.softplus(dtr_ref[0] + dtb_r_ref[...]) * (-jnp.exp(al_r_ref[...]))
    row_i = lax.broadcasted_iota(jnp.int32, (t, t), 0)
    col_i = lax.broadcasted_iota(jnp.int32, (t, t), 1)
    same = (row_i // CHUNK) == (col_i // CHUNK)
    lower = (same & (col_i <= row_i)).astype(f32)
    upper = (same & (row_i <= col_i)).astype(f32)
    acs_c = jnp.dot(lower, dta_c, precision=_HI, preferred_element_type=f32)
    acs_r = jnp.dot(dta_r, upper, precision=_HI, preferred_element_type=f32)
    end_c = jnp.dot(same.astype(f32), dta_c, precision=_HI, preferred_element_type=f32)

    expand = (lax.broadcasted_iota(jnp.int32, (SSD_HEADS, SSD_WIDTH), 1) // SSD_HEAD_DIM
              == lax.broadcasted_iota(jnp.int32, (SSD_HEADS, SSD_WIDTH), 0)).astype(f32)
    widen = lambda a: jnp.dot(a, expand, precision=_HI, preferred_element_type=f32)
    xdt = xs * widen(dt_c)
    e_acs = widen(jnp.exp(acs_c))
    x_end = xdt * widen(jnp.exp(end_c - acs_c))
    e_end = widen(jnp.exp(end_c))

    causal = (lax.broadcasted_iota(jnp.int32, (CHUNK, CHUNK), 1)
              <= lax.broadcasted_iota(jnp.int32, (CHUNK, CHUNK), 0))
    head_of_lane = lax.broadcasted_iota(jnp.int32, (CHUNK, SSD_GW), 1) // SSD_HEAD_DIM
    nt = (((1,), (1,)), ((), ()))
    for c in range(t // CHUNK):
        r0 = c * CHUNK
        rows = slice(r0, r0 + CHUNK)
        new_state = []
        for g in range(SSD_GROUPS):
            cols = slice(g * SSD_GW, (g + 1) * SSD_GW)
            bm = xc[rows, SSD_WIDTH + g * SSD_STATE:SSD_WIDTH + (g + 1) * SSD_STATE].astype(bf16)
            cm = xc[rows, SSD_WIDTH + (SSD_GROUPS + g) * SSD_STATE:
                    SSD_WIDTH + (SSD_GROUPS + g + 1) * SSD_STATE].astype(bf16)
            cb = lax.dot_general(cm, bm, nt, preferred_element_type=f32)
            xdt_g = xdt[rows, cols]
            y = jnp.zeros((CHUNK, SSD_GW), f32)
            for hh in range(SSD_HEADS_PER_GROUP):
                h = g * SSD_HEADS_PER_GROUP + hh
                seg = acs_c[rows, h:h + 1] - acs_r[h:h + 1, r0:r0 + CHUNK]
                decay = jnp.exp(jnp.where(causal, seg, -jnp.inf))
                x_h = jnp.where(head_of_lane == hh, xdt_g, 0.0).astype(bf16)
                y = y + jnp.dot((cb * decay).astype(bf16), x_h, preferred_element_type=f32)
            y = y + e_acs[rows, cols] * jnp.dot(cm, st_scr[:, cols].astype(bf16),
                                                preferred_element_type=f32)
            y_scr[rows, cols] = y
            new_state.append(lax.dot_general(bm, x_end[rows, cols].astype(bf16),
                                             (((0,), (0,)), ((), ())),
                                             preferred_element_type=f32))
        st_scr[...] = st_scr[...] * e_end[r0:r0 + 1, :] + jnp.concatenate(new_state, axis=1)

    y = y_scr[...] + xs * dskip_ref[...]
    y = y * jax.nn.silu(z_ref[0])
    for g in range(SSD_GROUPS):
        cols = slice(g * SSD_GW, (g + 1) * SSD_GW)
        yg = y[:, cols]
        yg = yg * lax.rsqrt(jnp.mean(yg * yg, axis=-1, keepdims=True) + EPS)
        o_ref[0, :, cols] = yg * nw_ref[:, cols]


def ssd_branch(proj, dt_raw, conv_w, conv_b, dt_bias, a_log, d_skip, norm_w):
    b, s, _ = proj.shape
    h = SSD_HEADS
    w = SSD_WIDTH
    blk = lambda col: pl.BlockSpec((1, SSD_T, w), lambda bi, j, col=col: (bi, j, col))
    const = lambda shape: pl.BlockSpec(shape, lambda bi, j: (0,) * len(shape))
    return pl.pallas_call(
        _ssd_kernel,
        grid=(b, s // SSD_T),
        in_specs=[
            blk(0), blk(1), blk(2),
            pl.BlockSpec((1, SSD_T, h), lambda bi, j: (bi, j, 0)),
            pl.BlockSpec((1, h, SSD_T), lambda bi, j: (bi, 0, j)),
            const((CONV_WIDTH, SSD_CONV_CH)), const((1, SSD_CONV_CH)),
            const((1, h)), const((h, 1)), const((1, h)), const((h, 1)),
            const((1, w)), const((1, w)),
        ],
        out_specs=pl.BlockSpec((1, SSD_T, w), lambda bi, j: (bi, j, 0)),
        out_shape=jax.ShapeDtypeStruct((b, s, w), jnp.float32),
        scratch_shapes=[
            pltpu.VMEM((SUBLANES + SSD_T, SSD_CONV_CH), jnp.float32),
            pltpu.VMEM((SSD_STATE, w), jnp.float32),
            pltpu.VMEM((SSD_T, w), jnp.float32),
        ],
        compiler_params=pltpu.CompilerParams(
            dimension_semantics=("arbitrary", "arbitrary"),
            vmem_limit_bytes=32 * 1024 * 1024),
        name="ssd_branch",
    )(proj, proj, proj, dt_raw, jnp.swapaxes(dt_raw, 1, 2), conv_w, conv_b.reshape(1, -1),
      dt_bias.reshape(1, h), dt_bias.reshape(h, 1), a_log.reshape(1, h), a_log.reshape(h, 1),
      jnp.repeat(d_skip, SSD_HEAD_DIM).reshape(1, w), norm_w.reshape(1, w))


def rmsnorm(x, w):
    xf = x.astype(jnp.float32)
    y = xf * lax.rsqrt(jnp.mean(xf * xf, axis=-1, keepdims=True) + EPS)
    return (y * w.astype(jnp.float32)).astype(x.dtype)


def rotary_tables(seq):
    inv = jnp.power(ROPE_THETA, -jnp.arange(0, ROPE_DIM, 2, dtype=jnp.float32) / ROPE_DIM)
    ang = jnp.arange(seq, dtype=jnp.float32)[:, None] * inv[None, :]
    return jnp.cos(ang), jnp.sin(ang)


def partial_rotary(x, cos, sin):
    half = ROPE_DIM // 2
    c = cos[:, None, None, :]
    s = sin[:, None, None, :]
    xf = x.astype(jnp.float32)
    x1, x2, xp = xf[..., :half], xf[..., half:ROPE_DIM], xf[..., ROPE_DIM:]
    out = jnp.concatenate([x1 * c - x2 * s, x1 * s + x2 * c, xp], axis=-1)
    return out.astype(x.dtype)


def diff_attention(q, k, v, qn_w, kn_w, lq1, lk1, lq2, lk2, subln_w, lambda_init, cos, sin):
    f32 = jnp.float32
    b, s, _ = q.shape
    q = q.reshape(b, s, DIFF_HEADS, 2, DIFF_QKDIM)
    k = k.reshape(b, s, DIFF_HEADS, 2, DIFF_QKDIM)
    q = partial_rotary(rmsnorm(q, qn_w), cos, sin).astype(f32)
    kf = partial_rotary(rmsnorm(k, kn_w), cos, sin).astype(f32)
    lam = (jnp.exp(jnp.sum(lq1.astype(f32) * lk1.astype(f32)))
           - jnp.exp(jnp.sum(lq2.astype(f32) * lk2.astype(f32))) + lambda_init)
    scale = DIFF_QKDIM ** -0.5
    branch = jnp.arange(2)
    to_cols = lambda a: jnp.swapaxes(a.reshape(b, s, -1).astype(jnp.bfloat16), 1, 2)
    q0 = to_cols(q * scale * (branch == 0)[:, None])
    q1 = to_cols(q * scale * (branch == 1)[:, None])
    kb = kf.reshape(b, s, QK_COLS).astype(jnp.bfloat16)
    vb = to_cols(v)
    return diff_attention_core(q0, q1, kb, vb, lam.reshape(1), subln_w, 1.0 - lambda_init)


def kernel(x, mix_norm_w, w_in, conv_w, conv_b, dt_bias, a_log, d_skip, ssd_norm_w, q_norm_w, k_norm_w, lambda_q1, lambda_k1, lambda_q2, lambda_k2, subln_w, w_out, ffn_norm_w, peer_w_q, peer_sub_keys, peer_u, peer_v):
    b, s, d = x.shape
    cos, sin = rotary_tables(s)
    layer = 0
    lambda_init = 0.8 - 0.6 * math.exp(-0.3 * layer)
    proj = norm_proj(x.reshape(b * s, d), mix_norm_w[layer],
                     w_in[layer].astype(jnp.bfloat16)).reshape(b, s, IN_COLS)
    dt_raw = proj[..., SPLITS[1]:SPLITS[2]]
    q = proj[..., SPLITS[2]:SPLITS[3]]
    k = proj[..., SPLITS[3]:SPLITS[4]]
    v = proj[..., SPLITS[4]:]
    y_ssd = ssd_branch(proj, dt_raw, conv_w[layer], conv_b[layer], dt_bias[layer],
                       a_log[layer], d_skip[layer], ssd_norm_w[layer])
    y_diff = diff_attention(q, k, v, q_norm_w[layer], k_norm_w[layer], lambda_q1[layer],
                            lambda_k1[layer], lambda_q2[layer], lambda_k2[layer],
                            subln_w[layer], lambda_init, cos, sin)
    keys = peer_sub_keys[layer].reshape(N_SUBSETS, N_KEYS, PEER_HALF).astype(jnp.bfloat16)
    x2, xn, idx, gates = peer_route(x.reshape(b * s, d), y_ssd.reshape(b * s, SSD_WIDTH),
                                    y_diff.reshape(b * s, DIFF_WIDTH),
                                    w_out[layer].astype(jnp.bfloat16),
                                    ffn_norm_w[layer], peer_w_q[layer].astype(jnp.bfloat16), keys)
    out = peer_gather(xn, idx, gates, x2, pack_uv(peer_u[layer], peer_v[layer]))
    return out.reshape(b, s, d)
```
